```python
import jax, jax.numpy as jnp
from jax import lax
import numpy as np

D_MODEL = 2048
BATCH = 2
SEQ = 4096
DEPTH = 1

GLA_HEADS = 4
GLA_DK = 128
GLA_DV = 256
GLA_GATE_RANK = 16
GLA_TAU = 16.0
GLA_CHUNK = 64

MLA_HEADS = 8
Q_LORA = 512
KV_LORA = 512
QK_NOPE = 128
QK_ROPE = 64
V_DIM = 128
ROPE_THETA = 10000.0
Q_BLOCK = 128

MIX_WIDTH = GLA_HEADS * GLA_DV + MLA_HEADS * V_DIM
IN_SIZES = (GLA_HEADS * GLA_DK,
            GLA_HEADS * GLA_DK,
            GLA_HEADS * GLA_DV,
            GLA_HEADS * GLA_DV,
            GLA_GATE_RANK,
            Q_LORA,
            KV_LORA,
            QK_ROPE)
IN_COLS = sum(IN_SIZES)

N_EXPERTS = 32
TOP_K = 4
D_FF = 2048
SWIGLU_LIMIT = 7.0
SWIGLU_ALPHA = 1.702
MOE_BLOCK = 128
EPS = 1e-6

kernel_name = "hymba_gla_mla_moe_block"


def rms_norm(x, g):
    xf = x.astype(jnp.float32)
    y = xf * lax.rsqrt(jnp.mean(xf * xf, axis=-1, keepdims=True) + EPS)
    return (y * g.astype(jnp.float32)).astype(x.dtype)


def split_cols(t, sizes):
    return jnp.split(t, np.cumsum(sizes)[:-1].tolist(), axis=-1)


def apply_rope(t, cos, sin):
    tf = t.astype(jnp.float32)
    t1, t2 = jnp.split(tf, 2, axis=-1)
    return jnp.concatenate([t1 * cos - t2 * sin, t1 * sin + t2 * cos], axis=-1)


def gla_chunked(q, k, v, log_a):
    B, S, H, dk = q.shape
    dv = v.shape[-1]
    N = S // GLA_CHUNK

    def chunks(t):
        return t.astype(jnp.float32).reshape(B, N, GLA_CHUNK, H, -1).transpose(0, 3, 1, 2, 4)

    qc = chunks(q) * (dk ** -0.5)
    kc, vc, gc = chunks(k), chunks(v), chunks(log_a)
    b = jnp.cumsum(gc, axis=3)
    b_last = b[:, :, :, -1:, :]
    q_dec = qc * jnp.exp(b)
    k_intra = kc * jnp.exp(-b)
    k_state = kc * jnp.exp(b_last - b)
    attn = jnp.einsum('bhnid,bhnjd->bhnij', q_dec, k_intra)
    causal = jnp.tril(jnp.ones((GLA_CHUNK, GLA_CHUNK), dtype=bool))
    attn = jnp.where(causal, attn, 0.0)
    o_intra = jnp.einsum('bhnij,bhnjv->bhniv', attn, vc)
    kv_chunk = jnp.einsum('bhnjd,bhnjv->bhndv', k_state, vc)
    decay = jnp.exp(b_last[:, :, :, 0, :])

    def step(state, inp):
        kv_c, dec = inp
        return dec[..., None] * state + kv_c, state

    _, s_prev = lax.scan(step, jnp.zeros((B, H, dk, dv), jnp.float32),
                         (jnp.moveaxis(kv_chunk, 2, 0), jnp.moveaxis(decay, 2, 0)))
    s_prev = jnp.moveaxis(s_prev, 0, 2)
    o = o_intra + jnp.einsum('bhnid,bhndv->bhniv', q_dec, s_prev)
    return o.transpose(0, 2, 3, 1, 4).reshape(B, S, H, dv).astype(v.dtype)


def mla_attention(q_lat, kv_lat, k_rope, cos, sin, q_norm, w_uq, kv_norm, w_ukv):
    B, S, _ = q_lat.shape
    H = MLA_HEADS
    q = (rms_norm(q_lat, q_norm) @ w_uq).reshape(B, S, H, QK_NOPE + QK_ROPE)
    kv = (rms_norm(kv_lat, kv_norm) @ w_ukv).reshape(B, S, H, QK_NOPE + V_DIM)
    q_pe = apply_rope(q[..., QK_NOPE:], cos, sin)
    k_pe = apply_rope(k_rope[:, :, None, :], cos, sin)
    scale = (QK_NOPE + QK_ROPE) ** -0.5
    qf = jnp.concatenate([q[..., :QK_NOPE].astype(jnp.float32), q_pe], axis=-1) * scale
    kf = jnp.concatenate([kv[..., :QK_NOPE].astype(jnp.float32),
                          jnp.broadcast_to(k_pe, (B, S, H, QK_ROPE))], axis=-1)
    vf = kv[..., QK_NOPE:].astype(jnp.float32)
    nqb = S // Q_BLOCK
    qb = qf.reshape(B, nqb, Q_BLOCK, H, QK_NOPE + QK_ROPE).transpose(1, 0, 2, 3, 4)
    key_pos = jnp.arange(S)

    def attend(args):
        qblk, i = args
        s = jnp.einsum('bqhd,bkhd->bhqk', qblk, kf)
        qpos = i * Q_BLOCK + jnp.arange(Q_BLOCK)
        s = jnp.where(key_pos[None, :] <= qpos[:, None], s, -jnp.inf)
        p = jax.nn.softmax(s, axis=-1)
        return jnp.einsum('bhqk,bkhv->bqhv', p, vf)

    out = lax.map(attend, (qb, jnp.arange(nqb)))
    return out.transpose(1, 0, 2, 3, 4).reshape(B, S, H * V_DIM).astype(q_lat.dtype)


def moe_ffn(h, router_w, router_b, w_gu, b_gu, w_dn, b_dn):
    B, S, D = h.shape
    T = B * S
    A = T * TOP_K
    hf = h.reshape(T, D)
    logits = (hf @ router_w + router_b).astype(jnp.float32)
    top_val, top_idx = lax.top_k(logits, TOP_K)
    gates = jax.nn.softmax(top_val, axis=-1)
    e_flat = top_idx.reshape(A)
    tok_flat = jnp.repeat(jnp.arange(T, dtype=jnp.int32), TOP_K)
    g_flat = gates.reshape(A)
    order = jnp.argsort(e_flat)
    e_sorted, tok_sorted, g_sorted = e_flat[order], tok_flat[order], g_flat[order]
    counts = jnp.bincount(e_flat, length=N_EXPERTS)
    padded = (counts + MOE_BLOCK - 1) // MOE_BLOCK * MOE_BLOCK
    start = jnp.cumsum(counts) - counts
    pend = jnp.cumsum(padded)
    pstart = pend - padded
    dest = pstart[e_sorted] + (jnp.arange(A) - start[e_sorted])
    n_blocks = -(-(A + N_EXPERTS * (MOE_BLOCK - 1)) // MOE_BLOCK)
    P = n_blocks * MOE_BLOCK
    tok_buf = jnp.zeros((P,), jnp.int32).at[dest].set(tok_sorted)
    gate_buf = jnp.zeros((P,), jnp.float32).at[dest].set(g_sorted)
    block_expert = jnp.clip(jnp.searchsorted(pend, jnp.arange(n_blocks) * MOE_BLOCK, side='right'),
                            0, N_EXPERTS - 1)

    def expert_block(args):
        toks, e = args
        xb = hf[toks]
        gu = xb @ w_gu[e] + b_gu[e]
        gate, up = gu[:, :D_FF], gu[:, D_FF:]
        gate = jnp.minimum(gate, SWIGLU_LIMIT)
        up = jnp.clip(up, -SWIGLU_LIMIT, SWIGLU_LIMIT)
        act = (up + 1.0) * (gate * jax.nn.sigmoid(gate * SWIGLU_ALPHA))
        return act @ w_dn[e] + b_dn[e]

    y = lax.map(expert_block, (tok_buf.reshape(n_blocks, MOE_BLOCK), block_expert))
    y = y.reshape(P, D) * gate_buf[:, None].astype(y.dtype)
    out = jax.ops.segment_sum(y, tok_buf, num_segments=T)
    return out.reshape(B, S, D).astype(h.dtype)


def setup_inputs(seed: int = 0) -> dict:
    key = jax.random.key(seed)
    ks = jax.random.split(key, 24)
    f32 = jnp.float32
    nrm = lambda k, shape, s: jax.random.normal(k, shape, f32) * s
    L = DEPTH
    x = jax.random.normal(ks[0], (BATCH, SEQ, D_MODEL), f32)
    offsets = jax.random.randint(ks[1], (BATCH, 1), 0, 1024, dtype=jnp.int32)
    positions = offsets + jnp.arange(SEQ, dtype=jnp.int32)[None, :]
    gain = lambda k, shape: 1.0 + nrm(k, shape, 0.02)
    return {
        "x": x,
        "positions": positions,
        "attn_norm": gain(ks[2], (L, D_MODEL)),
        "w_in": nrm(ks[3], (L, D_MODEL, IN_COLS), D_MODEL ** -0.5),
        "gla_gate_up": nrm(ks[4], (L, GLA_GATE_RANK, GLA_HEADS * GLA_DK), GLA_GATE_RANK ** -0.5),
        "gla_gate_bias": nrm(ks[5], (L, GLA_HEADS * GLA_DK), 0.1),
        "gla_out_norm": gain(ks[6], (L, GLA_DV)),
        "mla_q_norm": gain(ks[7], (L, Q_LORA)),
        "w_uq": nrm(ks[8], (L, Q_LORA, MLA_HEADS * (QK_NOPE + QK_ROPE)), Q_LORA ** -0.5),
        "mla_kv_norm": gain(ks[9], (L, KV_LORA)),
        "w_ukv": nrm(ks[10], (L, KV_LORA, MLA_HEADS * (QK_NOPE + V_DIM)), KV_LORA ** -0.5),
        "mla_out_norm": gain(ks[11], (L, MLA_HEADS * V_DIM)),
        "w_out": nrm(ks[12], (L, MIX_WIDTH, D_MODEL), MIX_WIDTH ** -0.5),
        "ffn_norm": gain(ks[13], (L, D_MODEL)),
        "router_w": nrm(ks[14], (L, D_MODEL, N_EXPERTS), D_MODEL ** -0.5),
        "router_b": nrm(ks[15], (L, N_EXPERTS), 0.01),
        "w_gate_up": nrm(ks[16], (L, N_EXPERTS, D_MODEL, 2 * D_FF), D_MODEL ** -0.5),
        "b_gate_up": nrm(ks[17], (L, N_EXPERTS, 2 * D_FF), 0.01),
        "w_down": nrm(ks[18], (L, N_EXPERTS, D_FF, D_MODEL), D_FF ** -0.5),
        "b_down": nrm(ks[19], (L, N_EXPERTS, D_MODEL), 0.01),
        "final_norm": gain(ks[20], (D_MODEL,)),
    }


def reference(x, positions, attn_norm, w_in, gla_gate_up, gla_gate_bias, gla_out_norm,
              mla_q_norm, w_uq, mla_kv_norm, w_ukv, mla_out_norm, w_out, ffn_norm,
              router_w, router_b, w_gate_up, b_gate_up, w_down, b_down, final_norm):
    B, S, _ = x.shape
    inv_freq = 1.0 / (ROPE_THETA ** (jnp.arange(0, QK_ROPE, 2, dtype=jnp.float32) / QK_ROPE))
    ang = positions.astype(jnp.float32)[..., None] * inv_freq
    cos = jnp.cos(ang)[:, :, None, :]
    sin = jnp.sin(ang)[:, :, None, :]
    for l in range(DEPTH):
        h = rms_norm(x, attn_norm[l])
        proj = h @ w_in[l]
        gq, gk, gv, gr, gdown, mq, mkv, mkr = split_cols(proj, IN_SIZES)
        log_a = jax.nn.log_sigmoid((gdown @ gla_gate_up[l] + gla_gate_bias[l]).astype(jnp.float32)) / GLA_TAU
        o = gla_chunked(gq.reshape(B, S, GLA_HEADS, GLA_DK), gk.reshape(B, S, GLA_HEADS, GLA_DK),
                        gv.reshape(B, S, GLA_HEADS, GLA_DV), log_a.reshape(B, S, GLA_HEADS, GLA_DK))
        o = rms_norm(o, gla_out_norm[l]) * jax.nn.silu(gr).reshape(B, S, GLA_HEADS, GLA_DV)
        o_gla = o.reshape(B, S, GLA_HEADS * GLA_DV)
        o_mla = mla_attention(mq, mkv, mkr, cos, sin, mla_q_norm[l], w_uq[l], mla_kv_norm[l], w_ukv[l])
        o_mla = rms_norm(o_mla, mla_out_norm[l])
        mix = jnp.concatenate([o_gla.astype(x.dtype), o_mla.astype(x.dtype)], axis=-1)
        x = x + mix @ w_out[l]
        h = rms_norm(x, ffn_norm[l])
        x = x + moe_ffn(h, router_w[l], router_b[l], w_gate_up[l], b_gate_up[l], w_down[l], b_down[l])
    return rms_norm(x, final_norm)
```

```python
import functools

import jax
import jax.numpy as jnp
from jax import lax
from jax.experimental import pallas as pl
from jax.experimental.pallas import tpu as pltpu

F32 = jnp.float32
BF16 = jnp.bfloat16
I32 = jnp.int32
HIGHEST = lax.Precision.HIGHEST

LANE = 128
V7X_VMEM_LIMIT_BYTES = 60000 * 1024

D_MODEL = 2048
GLA_HEADS, GLA_DK, GLA_DV, GLA_RANK, GLA_TAU, GLA_CHUNK = 4, 128, 256, 16, 16.0, 64
MLA_HEADS, Q_LORA, KV_LORA, QK_NOPE, QK_ROPE, V_DIM = 8, 512, 512, 128, 64, 128
ROPE_THETA = 10000.0
N_EXPERTS, TOP_K, D_FF = 32, 4, 2048
SWIGLU_LIMIT, SWIGLU_ALPHA = 7.0, 1.702
EPS = 1e-6
GLA_W = GLA_HEADS * GLA_DK
GLA_VW = GLA_HEADS * GLA_DV
MLA_W = MLA_HEADS * V_DIM
QK_PAD = 2 * LANE

PROJ_OUT_COLS = 34 * LANE
PROJ_W_COLS = 35 * LANE

ROW_TILE = 256
TILES_PER_ITEM = 5
FF_CHUNK = 512
N_FF = D_FF // FF_CHUNK


def _rms(x, g):
    return x * lax.rsqrt(jnp.mean(x * x, axis=-1, keepdims=True) + EPS) * g


def _inproj_kernel(x_ref, g_ref, w_ref, gup_ref, gb_ref, proj_ref, loga_ref):
    h = _rms(x_ref[...], g_ref[...]).astype(BF16)
    p = jnp.dot(h, w_ref[...], preferred_element_type=F32)
    proj_ref[...] = p[:, :PROJ_OUT_COLS].astype(BF16)
    z = jnp.dot(p[:, PROJ_OUT_COLS:].astype(BF16), gup_ref[...], preferred_element_type=F32) + gb_ref[...]
    log_sig = jnp.minimum(z, 0.0) - jnp.log(1.0 + jnp.exp(-jnp.abs(z)))
    loga_ref[...] = log_sig * (1.0 / GLA_TAU)


def _in_proj(xf, g, w_r, gup_pad, gbias, tm=256):
    T = xf.shape[0]
    return pl.pallas_call(
        _inproj_kernel,
        grid=(T // tm,),
        in_specs=[pl.BlockSpec((tm, D_MODEL), lambda i: (i, 0)),
                  pl.BlockSpec((1, D_MODEL), lambda i: (0, 0)),
                  pl.BlockSpec((D_MODEL, PROJ_W_COLS), lambda i: (0, 0)),
                  pl.BlockSpec((LANE, GLA_W), lambda i: (0, 0)),
                  pl.BlockSpec((1, GLA_W), lambda i: (0, 0))],
        out_specs=[pl.BlockSpec((tm, PROJ_OUT_COLS), lambda i: (i, 0)),
                   pl.BlockSpec((tm, GLA_W), lambda i: (i, 0))],
        out_shape=[jax.ShapeDtypeStruct((T, PROJ_OUT_COLS), BF16),
                   jax.ShapeDtypeStruct((T, GLA_W), F32)],
        compiler_params=pltpu.CompilerParams(dimension_semantics=("parallel",),
                                             vmem_limit_bytes=V7X_VMEM_LIMIT_BYTES),
        name="in_proj",
    )(xf, g, w_r, gup_pad, gbias)


def _gla_kernel(q_ref, k_ref, v_ref, r_ref, la_ref, gn_ref, o_ref, st_ref, *, n_chunks):
    @pl.when(pl.program_id(1) == 0)
    def _():
        st_ref[...] = jnp.zeros_like(st_ref)

    C = GLA_CHUNK
    row = lax.broadcasted_iota(I32, (C, C), 0)
    col = lax.broadcasted_iota(I32, (C, C), 1)
    causal = col <= row
    tri = causal.astype(F32)
    for c in range(n_chunks):
        sl = pl.ds(c * C, C)
        b = jnp.dot(tri, la_ref[sl, :], precision=HIGHEST, preferred_element_type=F32)
        b_last = b[C - 1:C, :]
        q_dec = (q_ref[sl, :].astype(F32) * (GLA_DK ** -0.5) * jnp.exp(b)).astype(BF16)
        kf = k_ref[sl, :].astype(F32)
        k_intra = (kf * jnp.exp(-b)).astype(BF16)
        k_state = (kf * jnp.exp(b_last - b)).astype(BF16)
        decay = jnp.exp(b_last)
        for h in range(GLA_HEADS):
            hs = slice(h * GLA_DK, (h + 1) * GLA_DK)
            vs = slice(h * GLA_DV, (h + 1) * GLA_DV)
            v = v_ref[sl, vs]
            attn = lax.dot_general(q_dec[:, hs], k_intra[:, hs], (((1,), (1,)), ((), ())),
                                   preferred_element_type=F32)
            attn = jnp.where(causal, attn, 0.0).astype(BF16)
            st = st_ref[h]
            o = jnp.dot(attn, v, preferred_element_type=F32)
            o = o + lax.dot_general(q_dec[:, hs], st.astype(BF16), (((1,), (1,)), ((), ())),
                                    preferred_element_type=F32)
            kv_t = lax.dot_general(v, k_state[:, hs], (((0,), (0,)), ((), ())),
                                   preferred_element_type=F32)
            st_ref[h] = st * decay[:, hs] + kv_t
            r = r_ref[sl, vs].astype(F32)
            o_ref[sl, vs] = (_rms(o, gn_ref[...]) * (r * jax.nn.sigmoid(r))).astype(BF16)


def _gla(proj, loga, gn, batch, seq, ct=256):
    nct = seq // ct
    T = batch * seq
    row_map = lambda blk: (lambda b, c: (b * nct + c, blk))
    return pl.pallas_call(
        functools.partial(_gla_kernel, n_chunks=ct // GLA_CHUNK),
        grid=(batch, nct),
        in_specs=[pl.BlockSpec((ct, GLA_W), row_map(0)),
                  pl.BlockSpec((ct, GLA_W), row_map(1)),
                  pl.BlockSpec((ct, GLA_VW), row_map(1)),
                  pl.BlockSpec((ct, GLA_VW), row_map(2)),
                  pl.BlockSpec((ct, GLA_W), row_map(0)),
                  pl.BlockSpec((1, GLA_DV), lambda b, c: (0, 0))],
        out_specs=pl.BlockSpec((ct, GLA_VW), row_map(0)),
        out_shape=jax.ShapeDtypeStruct((T, GLA_VW), BF16),
        scratch_shapes=[pltpu.VMEM((GLA_HEADS, GLA_DV, GLA_DK), F32)],
        compiler_params=pltpu.CompilerParams(dimension_semantics=("parallel", "arbitrary")),
        name="gla",
    )(proj, proj, proj, proj, loga, gn)


def _mla_proj_kernel(ql_ref, kvl_ref, kr_ref, krs_ref, cos_ref, sin_ref, qn_ref, kvn_ref,
                     wq_ref, wkv_ref, q_ref, k_ref, v_ref):
    cos = cos_ref[...]
    sin = sin_ref[...]
    scale = (QK_NOPE + QK_ROPE) ** -0.5
    qlat = _rms(ql_ref[...].astype(F32), qn_ref[...]).astype(BF16)
    qall = jnp.dot(qlat, wq_ref[...], preferred_element_type=F32)
    kvlat = _rms(kvl_ref[...].astype(F32), kvn_ref[...]).astype(BF16)
    kv = jnp.dot(kvlat, wkv_ref[...], preferred_element_type=F32)
    k_pe = (kr_ref[...].astype(F32) * cos + krs_ref[...].astype(F32) * sin).astype(BF16)
    for h in range(MLA_HEADS):
        hl = slice(h * LANE, (h + 1) * LANE)
        pe = slice(MLA_W + h * LANE, MLA_W + (h + 1) * LANE)
        ps = slice(2 * MLA_W + h * LANE, 2 * MLA_W + (h + 1) * LANE)
        q_ref[h, :, 0:LANE] = (qall[:, hl] * scale).astype(BF16)
        q_ref[h, :, LANE:QK_PAD] = ((qall[:, pe] * cos + qall[:, ps] * sin) * scale).astype(BF16)
        k_ref[h, :, 0:LANE] = kv[:, hl].astype(BF16)
        k_ref[h, :, LANE:QK_PAD] = k_pe
        v_ref[h] = kv[:, MLA_W + h * LANE:MLA_W + (h + 1) * LANE].astype(BF16)


def _mla_proj(proj, cos_t, sin_t, qn, kvn, wq, wkv, tm=512):
    T = proj.shape[0]
    H = MLA_HEADS
    return pl.pallas_call(
        _mla_proj_kernel,
        grid=(T // tm,),
        in_specs=[pl.BlockSpec((tm, Q_LORA), lambda i: (i, 6)),
                  pl.BlockSpec((tm, KV_LORA), lambda i: (i, 7)),
                  pl.BlockSpec((tm, LANE), lambda i: (i, 32)),
                  pl.BlockSpec((tm, LANE), lambda i: (i, 33)),
                  pl.BlockSpec((tm, LANE), lambda i: (i, 0)),
                  pl.BlockSpec((tm, LANE), lambda i: (i, 0)),
                  pl.BlockSpec((1, Q_LORA), lambda i: (0, 0)),
                  pl.BlockSpec((1, KV_LORA), lambda i: (0, 0)),
                  pl.BlockSpec((Q_LORA, 3 * MLA_W), lambda i: (0, 0)),
                  pl.BlockSpec((KV_LORA, 2 * MLA_W), lambda i: (0, 0))],
        out_specs=[pl.BlockSpec((H, tm, QK_PAD), lambda i: (0, i, 0)),
                   pl.BlockSpec((H, tm, QK_PAD), lambda i: (0, i, 0)),
                   pl.BlockSpec((H, tm, V_DIM), lambda i: (0, i, 0))],
        out_shape=[jax.ShapeDtypeStruct((H, T, QK_PAD), BF16),
                   jax.ShapeDtypeStruct((H, T, QK_PAD), BF16),
                   jax.ShapeDtypeStruct((H, T, V_DIM), BF16)],
        compiler_params=pltpu.CompilerParams(dimension_semantics=("parallel",)),
        name="mla_proj",
    )(proj, proj, proj, proj, cos_t, sin_t, qn, kvn, wq, wkv)


def _attn_kernel(q_ref, k_ref, v_ref, o_ref, m_ref, l_ref, acc_ref, *, tq, tk):
    qi = pl.program_id(2)
    q = q_ref[0]
    m_ref[...] = jnp.full_like(m_ref, -jnp.inf)
    l_ref[...] = jnp.zeros_like(l_ref)
    acc_ref[...] = jnp.zeros_like(acc_ref)

    def step(start, mask):
        kc = k_ref[0, pl.ds(start, tk), :]
        s = lax.dot_general(q, kc, (((1,), (1,)), ((), ())), preferred_element_type=F32)
        if mask is not None:
            s = jnp.where(mask, s, -jnp.inf)
        m_old = m_ref[...]
        m_new = jnp.maximum(m_old, jnp.max(s, axis=-1, keepdims=True))
        p = jnp.exp(s - m_new)
        alpha = jnp.exp(m_old - m_new)
        l_ref[...] = alpha * l_ref[...] + jnp.sum(p, axis=-1, keepdims=True)
        acc_ref[...] = alpha * acc_ref[...] + jnp.dot(p.astype(BF16), v_ref[0, pl.ds(start, tk), :],
                                                      preferred_element_type=F32)
        m_ref[...] = m_new

    n_sub = tq // tk

    def body(c, carry):
        step(pl.multiple_of(c * tk, tk), None)
        return carry

    lax.fori_loop(0, qi * n_sub, body, 0)
    row = lax.broadcasted_iota(I32, (tq, tk), 0)
    col = lax.broadcasted_iota(I32, (tq, tk), 1)
    for j in range(n_sub):
        step(pl.multiple_of(qi * tq + j * tk, tk), (col + j * tk) <= row)
    o_ref[...] = acc_ref[...] / l_ref[...]


def _mla_attn(q, k, v, batch, seq, tq=512, tk=512):
    H, T, _ = q.shape
    nq = seq // tq
    return pl.pallas_call(
        functools.partial(_attn_kernel, tq=tq, tk=tk),
        grid=(batch, H, nq),
        in_specs=[pl.BlockSpec((1, tq, QK_PAD), lambda b, h, i: (h, b * nq + i, 0)),
                  pl.BlockSpec((1, seq, QK_PAD), lambda b, h, i: (h, b, 0)),
                  pl.BlockSpec((1, seq, V_DIM), lambda b, h, i: (h, b, 0))],
        out_specs=pl.BlockSpec((tq, V_DIM), lambda b, h, i: (b * nq + i, h)),
        out_shape=jax.ShapeDtypeStruct((T, MLA_W), F32),
        scratch_shapes=[pltpu.VMEM((tq, 1), F32), pltpu.VMEM((tq, 1), F32), pltpu.VMEM((tq, V_DIM), F32)],
        compiler_params=pltpu.CompilerParams(dimension_semantics=("parallel", "parallel", "arbitrary")),
        name="mla_attn",
    )(q, k, v)


def _out_router_kernel(x_ref, og_ref, om_ref, wo_ref, mn_ref, fn_ref, rw_ref, rb_ref,
                       x2_ref, h2_ref, topi_ref, gate_ref, rank_ref, cnt_ref, carry_ref, *, tm):
    @pl.when(pl.program_id(0) == 0)
    def _():
        carry_ref[...] = jnp.zeros_like(carry_ref)

    om = _rms(om_ref[...], mn_ref[...]).astype(BF16)
    mix = jnp.dot(og_ref[...], wo_ref[0:GLA_VW, :], preferred_element_type=F32)
    mix = mix + jnp.dot(om, wo_ref[GLA_VW:GLA_VW + MLA_W, :], preferred_element_type=F32)
    x2 = x_ref[...] + mix
    x2_ref[...] = x2
    h2 = _rms(x2, fn_ref[...])
    h2_ref[...] = h2
    logits = jnp.dot(h2, rw_ref[...], precision=HIGHEST, preferred_element_type=F32) + rb_ref[...]
    lane = lax.broadcasted_iota(I32, (tm, LANE), 1).astype(F32)
    cur = jnp.where(lane < N_EXPERTS, logits, -jnp.inf)
    vals, idxs = [], []
    for _ in range(TOP_K):
        m = jnp.max(cur, axis=-1, keepdims=True)
        idx = jnp.min(jnp.where(cur == m, lane, float(N_EXPERTS - 1)), axis=-1, keepdims=True)
        vals.append(m)
        idxs.append(idx)
        cur = jnp.where(lane == idx, -jnp.inf, cur)
    es = [jnp.exp(v - vals[0]) for v in vals]
    den = es[0] + es[1] + es[2] + es[3]
    gates = jnp.zeros((tm, LANE), F32)
    topi = jnp.zeros((tm, LANE), F32)
    multi = jnp.zeros((tm, LANE), F32)
    for k in range(TOP_K):
        gates = jnp.where(lane == k, es[k] / den, gates)
        topi = jnp.where(lane == k, idxs[k], topi)
        multi = multi + (lane == idxs[k]).astype(F32)
    gate_ref[...] = gates
    topi_ref[...] = topi.astype(I32)
    r = lax.broadcasted_iota(I32, (tm, tm), 0)
    c = lax.broadcasted_iota(I32, (tm, tm), 1)
    strict = (c < r).astype(BF16)
    pref = jnp.dot(strict, multi.astype(BF16), preferred_element_type=F32) + carry_ref[...]
    rank = jnp.zeros((tm, LANE), F32)
    for k in range(TOP_K):
        rk = jnp.sum(jnp.where(lane == idxs[k], pref, 0.0), axis=-1, keepdims=True)
        rank = jnp.where(lane == k, rk, rank)
    rank_ref[...] = rank.astype(I32)
    carry_ref[...] = carry_ref[...] + jnp.sum(multi, axis=0, keepdims=True)
    cnt_ref[...] = carry_ref[...].astype(I32)


def _out_router(xf, o_gla, o_mla, w_out, mn, fn, rw_pad, rb_pad, tm=256):
    T = xf.shape[0]
    row = lambda i: (i, 0)
    fixed = lambda i: (0, 0)
    return pl.pallas_call(
        functools.partial(_out_router_kernel, tm=tm),
        grid=(T // tm,),
        in_specs=[pl.BlockSpec((tm, D_MODEL), row),
                  pl.BlockSpec((tm, GLA_VW), row),
                  pl.BlockSpec((tm, MLA_W), row),
                  pl.BlockSpec((GLA_VW + MLA_W, D_MODEL), fixed),
                  pl.BlockSpec((1, MLA_W), fixed),
                  pl.BlockSpec((1, D_MODEL), fixed),
                  pl.BlockSpec((D_MODEL, LANE), fixed),
                  pl.BlockSpec((1, LANE), fixed)],
        out_specs=[pl.BlockSpec((tm, D_MODEL), row),
                   pl.BlockSpec((tm, D_MODEL), row),
                   pl.BlockSpec((tm, LANE), row),
                   pl.BlockSpec((tm, LANE), row),
                   pl.BlockSpec((tm, LANE), row),
                   pl.BlockSpec((1, LANE), fixed)],
        out_shape=[jax.ShapeDtypeStruct((T, D_MODEL), F32),
                   jax.ShapeDtypeStruct((T, D_MODEL), F32),
                   jax.ShapeDtypeStruct((T, LANE), I32),
                   jax.ShapeDtypeStruct((T, LANE), F32),
                   jax.ShapeDtypeStruct((T, LANE), I32),
                   jax.ShapeDtypeStruct((1, LANE), I32)],
        scratch_shapes=[pltpu.VMEM((1, LANE), F32)],
        compiler_params=pltpu.CompilerParams(dimension_semantics=("arbitrary",),
                                             vmem_limit_bytes=V7X_VMEM_LIMIT_BYTES),
        name="out_router",
    )(xf, o_gla, o_mla, w_out, mn, fn, rw_pad, rb_pad)


def _row_copy(src, src_row, dst, dst_row, sem):
    return pltpu.make_async_copy(src.at[pl.ds(src_row, 1), :], dst.at[pl.ds(dst_row, 1), :], sem)


def _dispatch_kernel(pend_ref, inv_ref, h_ref, xs_ref, zero_ref, sem, *, tm, n_tiles):
    @pl.when(pl.program_id(0) == 0)
    def _():
        zero_ref[...] = jnp.zeros_like(zero_ref)

        def fill(e, wait):
            prev = jnp.where(e == 0, 0, pend_ref[jnp.maximum(e - 1, 0)])

            @pl.when(pend_ref[e] > prev)
            def _():
                cp = pltpu.make_async_copy(
                    zero_ref, xs_ref.at[pl.ds(pl.multiple_of(pend_ref[e] - ROW_TILE, ROW_TILE), ROW_TILE), :], sem)
                if wait:
                    cp.wait()
                else:
                    cp.start()

        def tail(t, wait):
            cp = pltpu.make_async_copy(zero_ref, xs_ref.at[pl.ds(pl.multiple_of(t * ROW_TILE, ROW_TILE), ROW_TILE), :], sem)
            if wait:
                cp.wait()
            else:
                cp.start()

        first_unused = pend_ref[N_EXPERTS - 1] // ROW_TILE
        lax.fori_loop(0, N_EXPERTS, lambda e, c: (fill(e, False), c)[1], 0)
        lax.fori_loop(first_unused, n_tiles, lambda t, c: (tail(t, False), c)[1], 0)
        lax.fori_loop(0, N_EXPERTS, lambda e, c: (fill(e, True), c)[1], 0)
        lax.fori_loop(first_unused, n_tiles, lambda t, c: (tail(t, True), c)[1], 0)

    def issue(r, c):
        for k in range(TOP_K):
            _row_copy(h_ref, r, xs_ref, inv_ref[0, 0, r * TOP_K + k], sem).start()
        return c

    def drain(r, c):
        for k in range(TOP_K):
            _row_copy(h_ref, r, xs_ref, 0, sem).wait()
        return c

    lax.fori_loop(0, tm, issue, 0)
    lax.fori_loop(0, tm, drain, 0)


def _dispatch(pend, inv, h2, n_rows, tm=256):
    T = h2.shape[0]
    inv3 = inv.reshape(T // tm, 1, tm * TOP_K)
    return pl.pallas_call(
        functools.partial(_dispatch_kernel, tm=tm, n_tiles=n_rows // ROW_TILE),
        grid=(T // tm,),
        in_specs=[pl.BlockSpec(memory_space=pltpu.SMEM),
                  pl.BlockSpec((1, 1, tm * TOP_K), lambda i: (i, 0, 0), memory_space=pltpu.SMEM),
                  pl.BlockSpec((tm, D_MODEL), lambda i: (i, 0))],
        out_specs=pl.BlockSpec(memory_space=pl.ANY),
        out_shape=jax.ShapeDtypeStruct((n_rows, D_MODEL), F32),
        scratch_shapes=[pltpu.VMEM((ROW_TILE, D_MODEL), F32), pltpu.SemaphoreType.DMA(())],
        compiler_params=pltpu.CompilerParams(dimension_semantics=("arbitrary",)),
        name="dispatch",
    )(pend, inv3, h2)


def _tile_copy(src, src_row, dst, dst_row, sem):
    return pltpu.make_async_copy(src.at[pl.ds(src_row, ROW_TILE), :], dst.at[pl.ds(dst_row, ROW_TILE), :], sem)


def _experts_kernel(ie_ref, row0_ref, nt_ref, xs_ref, wg_ref, wu_ref, wd_ref, bg_ref, bu_ref, bd_ref,
                    y_ref, x_bf, acc, wg_bf, wu_bf, wd_bf, sem):
    i = pl.program_id(0)
    f = pl.program_id(1)
    nt = nt_ref[i]
    row0 = row0_ref[i]

    def for_tiles(fn):
        lax.fori_loop(0, nt, lambda t, c: (fn(t, pl.multiple_of(t * ROW_TILE, ROW_TILE)), c)[1], 0)

    @pl.when((i == 0) & (f == 0))
    def _():
        acc[pl.ds(0, ROW_TILE), :] = jnp.zeros((ROW_TILE, D_MODEL), F32)
        first_unused = row0_ref[row0_ref.shape[0] - 1] // ROW_TILE
        n_tiles = y_ref.shape[0] // ROW_TILE

        def tail(t, wait):
            cp = _tile_copy(acc, 0, y_ref, pl.multiple_of(t * ROW_TILE, ROW_TILE), sem)
            if wait:
                cp.wait()
            else:
                cp.start()

        lax.fori_loop(first_unused, n_tiles, lambda t, c: (tail(t, False), c)[1], 0)
        lax.fori_loop(first_unused, n_tiles, lambda t, c: (tail(t, True), c)[1], 0)

    @pl.when(nt > 0)
    def _():
        @pl.when(f == 0)
        def _():
            for_tiles(lambda t, r: _tile_copy(xs_ref, pl.multiple_of(row0 + r, ROW_TILE), acc, r, sem).start())
            for_tiles(lambda t, r: _tile_copy(xs_ref, 0, acc, r, sem).wait())

            def convert(t, r):
                x_bf[pl.ds(r, ROW_TILE), :] = acc[pl.ds(r, ROW_TILE), :].astype(BF16)

            for_tiles(convert)

            def init(t, r):
                acc[pl.ds(r, ROW_TILE), :] = jnp.broadcast_to(bd_ref[0], (ROW_TILE, D_MODEL))

            for_tiles(init)

        wg_bf[...] = wg_ref[0].astype(BF16)
        wu_bf[...] = wu_ref[0].astype(BF16)
        wd_bf[...] = wd_ref[0].astype(BF16)

        def tile(t, r):
            x = x_bf[pl.ds(r, ROW_TILE), :]
            gate = jnp.dot(x, wg_bf[...], preferred_element_type=F32) + bg_ref[0]
            up = jnp.dot(x, wu_bf[...], preferred_element_type=F32) + bu_ref[0]
            gate = jnp.minimum(gate, SWIGLU_LIMIT)
            up = jnp.clip(up, -SWIGLU_LIMIT, SWIGLU_LIMIT)
            act = (up + 1.0) * (gate * jax.nn.sigmoid(gate * SWIGLU_ALPHA))
            acc[pl.ds(r, ROW_TILE), :] += jnp.dot(act.astype(BF16), wd_bf[...], preferred_element_type=F32)

        for_tiles(tile)

        @pl.when(f == N_FF - 1)
        def _():
            for_tiles(lambda t, r: _tile_copy(acc, r, y_ref, pl.multiple_of(row0 + r, ROW_TILE), sem).start())
            for_tiles(lambda t, r: _tile_copy(acc, r, y_ref, 0, sem).wait())


def _experts(item_e, item_row0, item_nt, xs, w_gu, b_gu, w_dn, b_dn):
    n_items = item_e.shape[0]
    n_rows = xs.shape[0]
    E = N_EXPERTS
    item_rows = TILES_PER_ITEM * ROW_TILE
    fsel = lambda i, f, nt: jnp.where(nt[i] > 0, f, N_FF - 1)
    grid_spec = pltpu.PrefetchScalarGridSpec(
        num_scalar_prefetch=3,
        grid=(n_items, N_FF),
        in_specs=[pl.BlockSpec(memory_space=pl.ANY),
                  pl.BlockSpec((1, D_MODEL, FF_CHUNK), lambda i, f, ie, r0, nt: (ie[i], 0, fsel(i, f, nt))),
                  pl.BlockSpec((1, D_MODEL, FF_CHUNK), lambda i, f, ie, r0, nt: (ie[i], 0, N_FF + fsel(i, f, nt))),
                  pl.BlockSpec((1, FF_CHUNK, D_MODEL), lambda i, f, ie, r0, nt: (ie[i], fsel(i, f, nt), 0)),
                  pl.BlockSpec((1, 1, FF_CHUNK), lambda i, f, ie, r0, nt: (ie[i], 0, fsel(i, f, nt))),
                  pl.BlockSpec((1, 1, FF_CHUNK), lambda i, f, ie, r0, nt: (ie[i], 0, N_FF + fsel(i, f, nt))),
                  pl.BlockSpec((1, 1, D_MODEL), lambda i, f, ie, r0, nt: (ie[i], 0, 0))],
        out_specs=pl.BlockSpec(memory_space=pl.ANY),
        scratch_shapes=[pltpu.VMEM((item_rows, D_MODEL), BF16),
                        pltpu.VMEM((item_rows, D_MODEL), F32),
                        pltpu.VMEM((D_MODEL, FF_CHUNK), BF16),
                        pltpu.VMEM((D_MODEL, FF_CHUNK), BF16),
                        pltpu.VMEM((FF_CHUNK, D_MODEL), BF16),
                        pltpu.SemaphoreType.DMA(())],
    )
    return pl.pallas_call(
        _experts_kernel,
        grid_spec=grid_spec,
        out_shape=jax.ShapeDtypeStruct((n_rows, D_MODEL), F32),
        compiler_params=pltpu.CompilerParams(dimension_semantics=("arbitrary", "arbitrary"),
                                             vmem_limit_bytes=V7X_VMEM_LIMIT_BYTES),
        name="experts",
    )(item_e, item_row0, item_nt, xs, w_gu, w_gu, w_dn,
      b_gu.reshape(E, 1, 2 * D_FF), b_gu.reshape(E, 1, 2 * D_FF), b_dn.reshape(E, 1, D_MODEL))


def _combine_kernel(inv_ref, x2_ref, gate_ref, fn_ref, y_ref, o_ref, rows, sem, *, tm, last_layer):
    def issue(r, c):
        for k in range(TOP_K):
            pltpu.make_async_copy(y_ref.at[pl.ds(inv_ref[0, 0, r * TOP_K + k], 1), :],
                                  rows.at[k, pl.ds(r, 1), :], sem).start()
        return c

    def drain(r, c):
        for k in range(TOP_K):
            pltpu.make_async_copy(y_ref.at[pl.ds(0, 1), :], rows.at[k, pl.ds(r, 1), :], sem).wait()
        return c

    lax.fori_loop(0, tm, issue, 0)
    lax.fori_loop(0, tm, drain, 0)
    g = gate_ref[...]
    out = x2_ref[...]
    for k in range(TOP_K):
        out = out + g[:, k:k + 1] * rows[k]
    o_ref[...] = _rms(out, fn_ref[...]) if last_layer else out


def _combine(inv, x2, gates, final_norm, y, last_layer, tm=128):
    T = x2.shape[0]
    inv3 = inv.reshape(T // tm, 1, tm * TOP_K)
    return pl.pallas_call(
        functools.partial(_combine_kernel, tm=tm, last_layer=last_layer),
        grid=(T // tm,),
        in_specs=[pl.BlockSpec((1, 1, tm * TOP_K), lambda i: (i, 0, 0), memory_space=pltpu.SMEM),
                  pl.BlockSpec((tm, D_MODEL), lambda i: (i, 0)),
                  pl.BlockSpec((tm, LANE), lambda i: (i, 0)),
                  pl.BlockSpec((1, D_MODEL), lambda i: (0, 0)),
                  pl.BlockSpec(memory_space=pl.ANY)],
        out_specs=pl.BlockSpec((tm, D_MODEL), lambda i: (i, 0)),
        out_shape=jax.ShapeDtypeStruct((T, D_MODEL), F32),
        scratch_shapes=[pltpu.VMEM((TOP_K, tm, D_MODEL), F32), pltpu.SemaphoreType.DMA(())],
        compiler_params=pltpu.CompilerParams(dimension_semantics=("arbitrary",)),
        name="combine",
    )(inv3, x2, gates, final_norm, y)


def _reorder_w_in(w_in):
    sizes = (GLA_W, GLA_W, GLA_VW, GLA_VW, GLA_RANK, Q_LORA, KV_LORA, QK_ROPE)
    offs = [0]
    for s in sizes:
        offs.append(offs[-1] + s)
    gq, gk, gv, gr, gd, mq, mkv, mkr = [w_in[:, offs[j]:offs[j + 1]] for j in range(len(sizes))]
    half = QK_ROPE // 2
    z = lambda n: jnp.zeros((D_MODEL, n), w_in.dtype)
    mkr_sw = jnp.concatenate([mkr[:, half:], mkr[:, :half]], axis=1)
    return jnp.concatenate([gq, gk, gv, gr, mq, mkv, mkr, z(LANE - QK_ROPE), mkr_sw, z(LANE - QK_ROPE),
                            gd, z(LANE - GLA_RANK)], axis=1).astype(BF16)


def _reorder_w_uq(w_uq):
    w = w_uq.reshape(Q_LORA, MLA_HEADS, QK_NOPE + QK_ROPE)
    nope = w[:, :, :QK_NOPE].reshape(Q_LORA, MLA_W)
    pe = w[:, :, QK_NOPE:]
    half = QK_ROPE // 2
    z = jnp.zeros((Q_LORA, MLA_HEADS, LANE - QK_ROPE), w_uq.dtype)
    pe_pad = jnp.concatenate([pe, z], axis=2).reshape(Q_LORA, MLA_W)
    pe_sw = jnp.concatenate([pe[:, :, half:], pe[:, :, :half], z], axis=2).reshape(Q_LORA, MLA_W)
    return jnp.concatenate([nope, pe_pad, pe_sw], axis=1).astype(BF16)


def _reorder_w_ukv(w_ukv):
    w = w_ukv.reshape(KV_LORA, MLA_HEADS, QK_NOPE + V_DIM)
    return jnp.concatenate([w[:, :, :QK_NOPE].reshape(KV_LORA, MLA_W),
                            w[:, :, QK_NOPE:].reshape(KV_LORA, MLA_W)], axis=1).astype(BF16)


def _rope_tables(positions):
    half = QK_ROPE // 2
    inv_freq = 1.0 / (ROPE_THETA ** (jnp.arange(0, QK_ROPE, 2, dtype=F32) / QK_ROPE))
    ang = positions.astype(F32).reshape(-1, 1) * inv_freq
    cos, sin = jnp.cos(ang), jnp.sin(ang)
    z = jnp.zeros((ang.shape[0], LANE - QK_ROPE), F32)
    del half
    return jnp.concatenate([cos, cos, z], axis=1), jnp.concatenate([-sin, sin, z], axis=1)


def _routing_plan(counts, topi, rank, n_items):
    padded = (counts + ROW_TILE - 1) // ROW_TILE * ROW_TILE
    pend = jnp.cumsum(padded)
    pstart = pend - padded
    inv = pstart[topi] + rank
    nt_e = padded // ROW_TILE
    ns_e = (nt_e + TILES_PER_ITEM - 1) // TILES_PER_ITEM
    cs = jnp.cumsum(ns_e)
    j = jnp.arange(n_items, dtype=I32)
    valid = j < cs[-1]
    jj = jnp.where(valid, j, cs[-1] - 1)
    e_j = jnp.sum((jj[:, None] >= cs[None, :]).astype(I32), axis=1)
    local = jj - (cs[e_j] - ns_e[e_j])
    row0 = pstart[e_j] + local * (TILES_PER_ITEM * ROW_TILE)
    nt_j = jnp.where(valid, jnp.clip(nt_e[e_j] - local * TILES_PER_ITEM, 0, TILES_PER_ITEM), 0)
    row0_and_used = jnp.concatenate([row0, pend[-1:]])
    return pend.astype(I32), inv.astype(I32), e_j.astype(I32), row0_and_used.astype(I32), nt_j.astype(I32)


def kernel(x, positions, attn_norm, w_in, gla_gate_up, gla_gate_bias, gla_out_norm, mla_q_norm, w_uq, mla_kv_norm, w_ukv, mla_out_norm, w_out, ffn_norm, router_w, router_b, w_gate_up, b_gate_up, w_down, b_down, final_norm):
    B, S, D = x.shape
    T = B * S
    A = T * TOP_K
    n_tiles = (A + N_EXPERTS * (ROW_TILE - 1)) // ROW_TILE
    n_rows = n_tiles * ROW_TILE
    n_items = (n_tiles + N_EXPERTS * (TILES_PER_ITEM - 1)) // TILES_PER_ITEM
    row2 = lambda v: v.reshape(1, -1)

    xf = x.reshape(T, D)
    cos_t, sin_t = _rope_tables(positions)
    for l in range(w_in.shape[0]):
        gup_pad = jnp.zeros((LANE, GLA_W), F32).at[:GLA_RANK].set(gla_gate_up[l]).astype(BF16)
        proj, loga = _in_proj(xf, row2(attn_norm[l]), _reorder_w_in(w_in[l]), gup_pad, row2(gla_gate_bias[l]))
        o_gla = _gla(proj, loga, row2(gla_out_norm[l]), B, S)
        q, k, v = _mla_proj(proj, cos_t, sin_t, row2(mla_q_norm[l]), row2(mla_kv_norm[l]),
                            _reorder_w_uq(w_uq[l]), _reorder_w_ukv(w_ukv[l]))
        o_mla = _mla_attn(q, k, v, B, S)
        rw_pad = jnp.zeros((D, LANE), F32).at[:, :N_EXPERTS].set(router_w[l])
        rb_pad = jnp.zeros((1, LANE), F32).at[0, :N_EXPERTS].set(router_b[l])
        x2, h2, topi, gates, rank, counts = _out_router(
            xf, o_gla, o_mla, w_out[l].astype(BF16), row2(mla_out_norm[l]), row2(ffn_norm[l]), rw_pad, rb_pad)
        pend, inv, item_e, item_row0, item_nt = _routing_plan(
            counts[0, :N_EXPERTS], topi[:, :TOP_K], rank[:, :TOP_K], n_items)
        xs = _dispatch(pend, inv, h2, n_rows)
        y = _experts(item_e, item_row0, item_nt, xs, w_gate_up[l], b_gate_up[l], w_down[l], b_down[l])
        xf = _combine(inv, x2, gates, row2(final_norm), y, last_layer=(l == w_in.shape[0] - 1))
    return xf.reshape(B, S, D)
```

```python
import functools

import jax
import jax.numpy as jnp
from jax import lax
from jax.experimental import pallas as pl
from jax.experimental.pallas import tpu as pltpu

F32 = jnp.float32
BF16 = jnp.bfloat16
I32 = jnp.int32
HIGHEST = lax.Precision.HIGHEST

LANE = 128
V7X_VMEM_LIMIT_BYTES = 60000 * 1024

D_MODEL = 2048
GLA_HEADS, GLA_DK, GLA_DV, GLA_RANK, GLA_TAU, GLA_CHUNK = 4, 128, 256, 16, 16.0, 64
MLA_HEADS, Q_LORA, KV_LORA, QK_NOPE, QK_ROPE, V_DIM = 8, 512, 512, 128, 64, 128
ROPE_THETA = 10000.0
N_EXPERTS, TOP_K, D_FF = 32, 4, 2048
SWIGLU_LIMIT, SWIGLU_ALPHA = 7.0, 1.702
EPS = 1e-6
LOG2_E = 1.4426950408889634
GLA_W = GLA_HEADS * GLA_DK
GLA_VW = GLA_HEADS * GLA_DV
MLA_W = MLA_HEADS * V_DIM
QK_PAD = 2 * LANE

PROJ_OUT_COLS = 34 * LANE
PROJ_W_COLS = 35 * LANE

ROW_TILE = 256
TILES_PER_ITEM = 5
FF_CHUNK = 512
N_FF = D_FF // FF_CHUNK


def _rms(x, g):
    return x * lax.rsqrt(jnp.mean(x * x, axis=-1, keepdims=True) + EPS) * g


def _inproj_kernel(x_ref, g_ref, w_ref, gup_ref, gb_ref, proj_ref, loga_ref):
    h = _rms(x_ref[...], g_ref[...]).astype(BF16)
    p = jnp.dot(h, w_ref[...], preferred_element_type=F32)
    proj_ref[...] = p[:, :PROJ_OUT_COLS].astype(BF16)
    z = jnp.dot(p[:, PROJ_OUT_COLS:].astype(BF16), gup_ref[...], preferred_element_type=F32) + gb_ref[...]
    log_sig = jnp.minimum(z, 0.0) - jnp.log(1.0 + jnp.exp(-jnp.abs(z)))
    loga_ref[...] = log_sig * (1.0 / GLA_TAU)


def _in_proj(xf, g, w_r, gup_pad, gbias, tm=256):
    T = xf.shape[0]
    return pl.pallas_call(
        _inproj_kernel,
        grid=(T // tm,),
        in_specs=[pl.BlockSpec((tm, D_MODEL), lambda i: (i, 0)),
                  pl.BlockSpec((1, D_MODEL), lambda i: (0, 0)),
                  pl.BlockSpec((D_MODEL, PROJ_W_COLS), lambda i: (0, 0)),
                  pl.BlockSpec((LANE, GLA_W), lambda i: (0, 0)),
                  pl.BlockSpec((1, GLA_W), lambda i: (0, 0))],
        out_specs=[pl.BlockSpec((tm, PROJ_OUT_COLS), lambda i: (i, 0)),
                   pl.BlockSpec((tm, GLA_W), lambda i: (i, 0))],
        out_shape=[jax.ShapeDtypeStruct((T, PROJ_OUT_COLS), BF16),
                   jax.ShapeDtypeStruct((T, GLA_W), F32)],
        compiler_params=pltpu.CompilerParams(dimension_semantics=("parallel",),
                                             vmem_limit_bytes=V7X_VMEM_LIMIT_BYTES),
        name="in_proj",
    )(xf, g, w_r, gup_pad, gbias)


def _gla_kernel(q_ref, k_ref, v_ref, r_ref, la_ref, gn_ref, o_ref, st_ref, *, n_chunks):
    @pl.when(pl.program_id(1) == 0)
    def _():
        st_ref[...] = jnp.zeros_like(st_ref)

    C = GLA_CHUNK
    row = lax.broadcasted_iota(I32, (C, C), 0)
    col = lax.broadcasted_iota(I32, (C, C), 1)
    causal = col <= row
    tri = causal.astype(F32)
    for c in range(n_chunks):
        sl = pl.ds(c * C, C)
        b = jnp.dot(tri, la_ref[sl, :], precision=HIGHEST, preferred_element_type=F32)
        b_last = b[C - 1:C, :]
        q_dec = (q_ref[sl, :].astype(F32) * (GLA_DK ** -0.5) * jnp.exp(b)).astype(BF16)
        kf = k_ref[sl, :].astype(F32)
        k_intra = (kf * jnp.exp(-b)).astype(BF16)
        k_state = (kf * jnp.exp(b_last - b)).astype(BF16)
        decay = jnp.exp(b_last)
        for h in range(GLA_HEADS):
            hs = slice(h * GLA_DK, (h + 1) * GLA_DK)
            vs = slice(h * GLA_DV, (h + 1) * GLA_DV)
            v = v_ref[sl, vs]
            attn = lax.dot_general(q_dec[:, hs], k_intra[:, hs], (((1,), (1,)), ((), ())),
                                   preferred_element_type=F32)
            attn = jnp.where(causal, attn, 0.0).astype(BF16)
            st = st_ref[h]
            o = jnp.dot(attn, v, preferred_element_type=F32)
            o = o + lax.dot_general(q_dec[:, hs], st.astype(BF16), (((1,), (1,)), ((), ())),
                                    preferred_element_type=F32)
            kv_t = lax.dot_general(v, k_state[:, hs], (((0,), (0,)), ((), ())),
                                   preferred_element_type=F32)
            st_ref[h] = st * decay[:, hs] + kv_t
            r = r_ref[sl, vs].astype(F32)
            o_ref[sl, vs] = (_rms(o, gn_ref[...]) * (r * jax.nn.sigmoid(r))).astype(BF16)


def _gla(proj, loga, gn, batch, seq, ct=256):
    nct = seq // ct
    T = batch * seq
    row_map = lambda blk: (lambda b, c: (b * nct + c, blk))
    return pl.pallas_call(
        functools.partial(_gla_kernel, n_chunks=ct // GLA_CHUNK),
        grid=(batch, nct),
        in_specs=[pl.BlockSpec((ct, GLA_W), row_map(0)),
                  pl.BlockSpec((ct, GLA_W), row_map(1)),
                  pl.BlockSpec((ct, GLA_VW), row_map(1)),
                  pl.BlockSpec((ct, GLA_VW), row_map(2)),
                  pl.BlockSpec((ct, GLA_W), row_map(0)),
                  pl.BlockSpec((1, GLA_DV), lambda b, c: (0, 0))],
        out_specs=pl.BlockSpec((ct, GLA_VW), row_map(0)),
        out_shape=jax.ShapeDtypeStruct((T, GLA_VW), BF16),
        scratch_shapes=[pltpu.VMEM((GLA_HEADS, GLA_DV, GLA_DK), F32)],
        compiler_params=pltpu.CompilerParams(dimension_semantics=("parallel", "arbitrary")),
        name="gla",
    )(proj, proj, proj, proj, loga, gn)


def _mla_proj_kernel(ql_ref, kvl_ref, kr_ref, krs_ref, cos_ref, sin_ref, qn_ref, kvn_ref,
                     wq_ref, wkv_ref, q_ref, k_ref, vt_ref):
    cos = cos_ref[...]
    sin = sin_ref[...]
    scale = (QK_NOPE + QK_ROPE) ** -0.5 * LOG2_E
    qlat = _rms(ql_ref[...].astype(F32), qn_ref[...]).astype(BF16)
    qall = jnp.dot(qlat, wq_ref[...], preferred_element_type=F32)
    kvlat = _rms(kvl_ref[...].astype(F32), kvn_ref[...]).astype(BF16)
    kv = jnp.dot(kvlat, wkv_ref[...], preferred_element_type=F32)
    k_pe = (kr_ref[...].astype(F32) * cos + krs_ref[...].astype(F32) * sin).astype(BF16)
    for h in range(MLA_HEADS):
        hl = slice(h * LANE, (h + 1) * LANE)
        pe = slice(MLA_W + h * LANE, MLA_W + (h + 1) * LANE)
        ps = slice(2 * MLA_W + h * LANE, 2 * MLA_W + (h + 1) * LANE)
        q_ref[h, :, 0:LANE] = (qall[:, hl] * scale).astype(BF16)
        q_ref[h, :, LANE:QK_PAD] = ((qall[:, pe] * cos + qall[:, ps] * sin) * scale).astype(BF16)
        k_ref[h, :, 0:LANE] = kv[:, hl].astype(BF16)
        k_ref[h, :, LANE:QK_PAD] = k_pe
        vt_ref[h] = kv[:, MLA_W + h * LANE:MLA_W + (h + 1) * LANE].T.astype(BF16)


def _mla_proj(proj, cos_t, sin_t, qn, kvn, wq, wkv, tm=512):
    T = proj.shape[0]
    H = MLA_HEADS
    return pl.pallas_call(
        _mla_proj_kernel,
        grid=(T // tm,),
        in_specs=[pl.BlockSpec((tm, Q_LORA), lambda i: (i, 6)),
                  pl.BlockSpec((tm, KV_LORA), lambda i: (i, 7)),
                  pl.BlockSpec((tm, LANE), lambda i: (i, 32)),
                  pl.BlockSpec((tm, LANE), lambda i: (i, 33)),
                  pl.BlockSpec((tm, LANE), lambda i: (i, 0)),
                  pl.BlockSpec((tm, LANE), lambda i: (i, 0)),
                  pl.BlockSpec((1, Q_LORA), lambda i: (0, 0)),
                  pl.BlockSpec((1, KV_LORA), lambda i: (0, 0)),
                  pl.BlockSpec((Q_LORA, 3 * MLA_W), lambda i: (0, 0)),
                  pl.BlockSpec((KV_LORA, 2 * MLA_W), lambda i: (0, 0))],
        out_specs=[pl.BlockSpec((H, tm, QK_PAD), lambda i: (0, i, 0)),
                   pl.BlockSpec((H, tm, QK_PAD), lambda i: (0, i, 0)),
                   pl.BlockSpec((H, V_DIM, tm), lambda i: (0, 0, i))],
        out_shape=[jax.ShapeDtypeStruct((H, T, QK_PAD), BF16),
                   jax.ShapeDtypeStruct((H, T, QK_PAD), BF16),
                   jax.ShapeDtypeStruct((H, V_DIM, T), BF16)],
        compiler_params=pltpu.CompilerParams(dimension_semantics=("parallel",)),
        name="mla_proj",
    )(proj, proj, proj, proj, cos_t, sin_t, qn, kvn, wq, wkv)


def _attn_kernel(q_ref, k_ref, vt_ref, o_ref, s_ref, m_ref, l_ref, acc_ref, *, tq):
    qi = pl.program_id(2)
    q = q_ref[0]
    m_ref[...] = jnp.full_like(m_ref, -jnp.inf)
    l_ref[...] = jnp.zeros_like(l_ref)
    acc_ref[...] = jnp.zeros_like(acc_ref)

    def scores(c):
        kc = k_ref[0, pl.ds(pl.multiple_of(c * tq, tq), tq), :]
        return lax.dot_general(kc, q, (((1,), (1,)), ((), ())), preferred_element_type=F32)

    def reduce_chunk(s_t, c):
        m_old = m_ref[...]
        m_new = jnp.maximum(m_old, jnp.max(s_t, axis=0, keepdims=True))
        p_t = jnp.exp2(s_t - m_new)
        alpha = jnp.exp2(m_old - m_new)
        l_ref[...] = alpha * l_ref[...] + jnp.sum(p_t, axis=0, keepdims=True)
        vt = vt_ref[0, :, pl.ds(pl.multiple_of(c * tq, tq), tq)]
        acc_ref[...] = alpha * acc_ref[...] + jnp.dot(vt, p_t.astype(BF16), preferred_element_type=F32)
        m_ref[...] = m_new

    s_ref[0] = scores(0)

    def body(c, carry):
        s_t = s_ref[c % 2]
        s_ref[(c + 1) % 2] = scores(c + 1)
        reduce_chunk(s_t, c)
        return carry

    lax.fori_loop(0, qi, body, 0)
    key = lax.broadcasted_iota(I32, (tq, tq), 0)
    qry = lax.broadcasted_iota(I32, (tq, tq), 1)
    reduce_chunk(jnp.where(key <= qry, s_ref[qi % 2], -jnp.inf), qi)
    o_ref[...] = (acc_ref[...] / l_ref[...]).T


def _mla_attn(q, k, vt, batch, seq, tq=512):
    H, T, _ = q.shape
    nq = seq // tq
    return pl.pallas_call(
        functools.partial(_attn_kernel, tq=tq),
        grid=(batch, H, nq),
        in_specs=[pl.BlockSpec((1, tq, QK_PAD), lambda b, h, i: (h, b * nq + i, 0)),
                  pl.BlockSpec((1, seq, QK_PAD), lambda b, h, i: (h, b, 0)),
                  pl.BlockSpec((1, V_DIM, seq), lambda b, h, i: (h, 0, b))],
        out_specs=pl.BlockSpec((tq, V_DIM), lambda b, h, i: (b * nq + i, h)),
        out_shape=jax.ShapeDtypeStruct((T, MLA_W), F32),
        scratch_shapes=[pltpu.VMEM((2, tq, tq), F32), pltpu.VMEM((1, tq), F32), pltpu.VMEM((1, tq), F32),
                        pltpu.VMEM((V_DIM, tq), F32)],
        compiler_params=pltpu.CompilerParams(dimension_semantics=("parallel", "parallel", "arbitrary")),
        name="mla_attn",
    )(q, k, vt)


def _out_router_kernel(x_ref, og_ref, om_ref, wo_ref, mn_ref, fn_ref, rwh_ref, rwl_ref, rb_ref,
                       x2_ref, h2_ref, topi_ref, gate_ref, rank_ref, cnt_ref, carry_ref, *, tm):
    @pl.when(pl.program_id(0) == 0)
    def _():
        carry_ref[...] = jnp.zeros_like(carry_ref)

    om = _rms(om_ref[...], mn_ref[...]).astype(BF16)
    mix = jnp.dot(og_ref[...], wo_ref[0:GLA_VW, :], preferred_element_type=F32)
    mix = mix + jnp.dot(om, wo_ref[GLA_VW:GLA_VW + MLA_W, :], preferred_element_type=F32)
    x2 = x_ref[...] + mix
    x2_ref[...] = x2
    h2 = _rms(x2, fn_ref[...])
    h2_ref[...] = h2
    h2_hi = h2.astype(BF16)
    h2_lo = (h2 - h2_hi.astype(F32)).astype(BF16)
    logits = (jnp.dot(h2_hi, rwh_ref[...], preferred_element_type=F32)
              + jnp.dot(h2_lo, rwh_ref[...], preferred_element_type=F32)
              + jnp.dot(h2_hi, rwl_ref[...], preferred_element_type=F32)) + rb_ref[...]
    lane = lax.broadcasted_iota(I32, (tm, LANE), 1).astype(F32)
    cur = jnp.where(lane < N_EXPERTS, logits, -jnp.inf)
    vals, idxs = [], []
    for _ in range(TOP_K):
        m = jnp.max(cur, axis=-1, keepdims=True)
        idx = jnp.min(jnp.where(cur == m, lane, float(N_EXPERTS - 1)), axis=-1, keepdims=True)
        vals.append(m)
        idxs.append(idx)
        cur = jnp.where(lane == idx, -jnp.inf, cur)
    es = [jnp.exp(v - vals[0]) for v in vals]
    den = es[0] + es[1] + es[2] + es[3]
    gates = jnp.zeros((tm, LANE), F32)
    topi = jnp.zeros((tm, LANE), F32)
    multi = jnp.zeros((tm, LANE), F32)
    for k in range(TOP_K):
        gates = jnp.where(lane == k, es[k] / den, gates)
        topi = jnp.where(lane == k, idxs[k], topi)
        multi = multi + (lane == idxs[k]).astype(F32)
    gate_ref[...] = gates
    topi_ref[...] = topi.astype(I32)
    r = lax.broadcasted_iota(I32, (tm, tm), 0)
    c = lax.broadcasted_iota(I32, (tm, tm), 1)
    strict = (c < r).astype(BF16)
    pref = jnp.dot(strict, multi.astype(BF16), preferred_element_type=F32) + carry_ref[...]
    rank = jnp.zeros((tm, LANE), F32)
    for k in range(TOP_K):
        rk = jnp.sum(jnp.where(lane == idxs[k], pref, 0.0), axis=-1, keepdims=True)
        rank = jnp.where(lane == k, rk, rank)
    rank_ref[...] = rank.astype(I32)
    carry_ref[...] = carry_ref[...] + jnp.sum(multi, axis=0, keepdims=True)
    cnt_ref[...] = carry_ref[...].astype(I32)


def _out_router(xf, o_gla, o_mla, w_out, mn, fn, rw_hi, rw_lo, rb_pad, tm=256):
    T = xf.shape[0]
    row = lambda i: (i, 0)
    fixed = lambda i: (0, 0)
    return pl.pallas_call(
        functools.partial(_out_router_kernel, tm=tm),
        grid=(T // tm,),
        in_specs=[pl.BlockSpec((tm, D_MODEL), row),
                  pl.BlockSpec((tm, GLA_VW), row),
                  pl.BlockSpec((tm, MLA_W), row),
                  pl.BlockSpec((GLA_VW + MLA_W, D_MODEL), fixed),
                  pl.BlockSpec((1, MLA_W), fixed),
                  pl.BlockSpec((1, D_MODEL), fixed),
                  pl.BlockSpec((D_MODEL, LANE), fixed),
                  pl.BlockSpec((D_MODEL, LANE), fixed),
                  pl.BlockSpec((1, LANE), fixed)],
        out_specs=[pl.BlockSpec((tm, D_MODEL), row),
                   pl.BlockSpec((tm, D_MODEL), row),
                   pl.BlockSpec((tm, LANE), row),
                   pl.BlockSpec((tm, LANE), row),
                   pl.BlockSpec((tm, LANE), row),
                   pl.BlockSpec((1, LANE), fixed)],
        out_shape=[jax.ShapeDtypeStruct((T, D_MODEL), F32),
                   jax.ShapeDtypeStruct((T, D_MODEL), F32),
                   jax.ShapeDtypeStruct((T, LANE), I32),
                   jax.ShapeDtypeStruct((T, LANE), F32),
                   jax.ShapeDtypeStruct((T, LANE), I32),
                   jax.ShapeDtypeStruct((1, LANE), I32)],
        scratch_shapes=[pltpu.VMEM((1, LANE), F32)],
        compiler_params=pltpu.CompilerParams(dimension_semantics=("arbitrary",),
                                             vmem_limit_bytes=V7X_VMEM_LIMIT_BYTES),
        name="out_router",
    )(xf, o_gla, o_mla, w_out, mn, fn, rw_hi, rw_lo, rb_pad)


def _row_copy(src, src_row, dst, dst_row, sem):
    return pltpu.make_async_copy(src.at[pl.ds(src_row, 1), :], dst.at[pl.ds(dst_row, 1), :], sem)


def _dispatch_kernel(pend_ref, inv_ref, h_ref, xs_ref, zero_ref, sem, *, tm, n_tiles):
    @pl.when(pl.program_id(0) == 0)
    def _():
        zero_ref[...] = jnp.zeros_like(zero_ref)

        def fill(e, wait):
            prev = jnp.where(e == 0, 0, pend_ref[jnp.maximum(e - 1, 0)])

            @pl.when(pend_ref[e] > prev)
            def _():
                cp = pltpu.make_async_copy(
                    zero_ref, xs_ref.at[pl.ds(pl.multiple_of(pend_ref[e] - ROW_TILE, ROW_TILE), ROW_TILE), :], sem)
                if wait:
                    cp.wait()
                else:
                    cp.start()

        def tail(t, wait):
            cp = pltpu.make_async_copy(zero_ref, xs_ref.at[pl.ds(pl.multiple_of(t * ROW_TILE, ROW_TILE), ROW_TILE), :], sem)
            if wait:
                cp.wait()
            else:
                cp.start()

        first_unused = pend_ref[N_EXPERTS - 1] // ROW_TILE
        lax.fori_loop(0, N_EXPERTS, lambda e, c: (fill(e, False), c)[1], 0)
        lax.fori_loop(first_unused, n_tiles, lambda t, c: (tail(t, False), c)[1], 0)
        lax.fori_loop(0, N_EXPERTS, lambda e, c: (fill(e, True), c)[1], 0)
        lax.fori_loop(first_unused, n_tiles, lambda t, c: (tail(t, True), c)[1], 0)

    def issue(r, c):
        for k in range(TOP_K):
            _row_copy(h_ref, r, xs_ref, inv_ref[0, 0, r * TOP_K + k], sem).start(priority=k % 2)
        return c

    def drain(r, c):
        for k in range(TOP_K):
            _row_copy(h_ref, r, xs_ref, 0, sem).wait()
        return c

    lax.fori_loop(0, tm, issue, 0)
    lax.fori_loop(0, tm, drain, 0)


def _dispatch(pend, inv, h2, n_rows, tm=256):
    T = h2.shape[0]
    inv3 = inv.reshape(T // tm, 1, tm * TOP_K)
    return pl.pallas_call(
        functools.partial(_dispatch_kernel, tm=tm, n_tiles=n_rows // ROW_TILE),
        grid=(T // tm,),
        in_specs=[pl.BlockSpec(memory_space=pltpu.SMEM),
                  pl.BlockSpec((1, 1, tm * TOP_K), lambda i: (i, 0, 0), memory_space=pltpu.SMEM),
                  pl.BlockSpec((tm, D_MODEL), lambda i: (i, 0))],
        out_specs=pl.BlockSpec(memory_space=pl.ANY),
        out_shape=jax.ShapeDtypeStruct((n_rows, D_MODEL), F32),
        scratch_shapes=[pltpu.VMEM((ROW_TILE, D_MODEL), F32), pltpu.SemaphoreType.DMA(())],
        compiler_params=pltpu.CompilerParams(dimension_semantics=("arbitrary",)),
        name="dispatch",
    )(pend, inv3, h2)


def _experts_kernel(ie_ref, row0_ref, nt_ref, xs_ref, wg_ref, wu_ref, wd_ref, bg_ref, bu_ref, bd_ref,
                    y_ref, x_stage, x_bf, acc, sem_x, sem_y):
    i = pl.program_id(0)
    f = pl.program_id(1)
    n_items = nt_ref.shape[0]
    nt = nt_ref[i]
    row0 = row0_ref[i]
    nt_prev = jnp.where(i > 0, nt_ref[jnp.maximum(i - 1, 0)], 0)
    nxt = jnp.minimum(i + 1, n_items - 1)
    nt_next = jnp.where(i + 1 < n_items, nt_ref[nxt], 0)
    row0_next = row0_ref[nxt]

    def rows_of(t):
        return pl.ds(pl.multiple_of(t * ROW_TILE, ROW_TILE), ROW_TILE)

    def x_copy(first_row, t):
        return pltpu.make_async_copy(xs_ref.at[pl.ds(pl.multiple_of(first_row + t * ROW_TILE, ROW_TILE), ROW_TILE), :],
                                     x_stage.at[rows_of(t), :], sem_x)

    def y_copy(t, dst_tile):
        return pltpu.make_async_copy(acc.at[rows_of(t), :], y_ref.at[rows_of(dst_tile), :], sem_y.at[t])

    def loop(lo, hi, fn):
        lax.fori_loop(lo, hi, lambda t, c: (fn(t), c)[1], 0)

    @pl.when((i == 0) & (f == 0))
    def _():
        acc[pl.ds(0, ROW_TILE), :] = jnp.zeros((ROW_TILE, D_MODEL), F32)
        first_unused = row0_ref[row0_ref.shape[0] - 1] // ROW_TILE
        n_tiles = y_ref.shape[0] // ROW_TILE
        loop(first_unused, n_tiles, lambda t: y_copy(0, t).start())
        loop(first_unused, n_tiles, lambda t: y_copy(0, t).wait())

    def tile(t, first, last):
        if first:
            @pl.when(t < nt_prev)
            def _():
                y_copy(t, 0).wait()
        x = x_bf[rows_of(t), :]
        gate = jnp.dot(x, wg_ref[0].astype(BF16), preferred_element_type=F32) + bg_ref[0]
        up = jnp.dot(x, wu_ref[0].astype(BF16), preferred_element_type=F32) + bu_ref[0]
        gate = jnp.minimum(gate, SWIGLU_LIMIT)
        up = jnp.clip(up, -SWIGLU_LIMIT, SWIGLU_LIMIT)
        act = (up + 1.0) * (gate * jax.nn.sigmoid(gate * SWIGLU_ALPHA))
        contrib = jnp.dot(act.astype(BF16), wd_ref[0].astype(BF16), preferred_element_type=F32)
        if first:
            acc[rows_of(t), :] = contrib + bd_ref[0]
        else:
            acc[rows_of(t), :] += contrib
        if last:
            y_copy(t, row0 // ROW_TILE + t).start()

    @pl.when(nt > 0)
    def _():
        @pl.when(f == 0)
        def _():
            @pl.when(i == 0)
            def _():
                loop(0, nt, lambda t: x_copy(row0, t).start())

            loop(0, nt, lambda t: x_copy(row0, t).wait())

            def convert(t):
                x_bf[rows_of(t), :] = x_stage[rows_of(t), :].astype(BF16)

            loop(0, nt, convert)
            loop(0, nt, lambda t: tile(t, True, False))
            loop(nt, nt_prev, lambda t: y_copy(t, 0).wait())

        @pl.when((f > 0) & (f < N_FF - 1))
        def _():
            loop(0, nt, lambda t: tile(t, False, False))

        @pl.when(f == N_FF - 1)
        def _():
            loop(0, nt_next, lambda t: x_copy(row0_next, t).start())
            loop(0, nt, lambda t: tile(t, False, True))

            @pl.when(nt_next == 0)
            def _():
                loop(0, nt, lambda t: y_copy(t, 0).wait())


def _experts(item_e, item_row0, item_nt, xs, w_gu, b_gu, w_dn, b_dn):
    n_items = item_e.shape[0]
    n_rows = xs.shape[0]
    E = N_EXPERTS
    item_rows = TILES_PER_ITEM * ROW_TILE
    fsel = lambda i, f, nt: jnp.where(nt[i] > 0, f, N_FF - 1)
    grid_spec = pltpu.PrefetchScalarGridSpec(
        num_scalar_prefetch=3,
        grid=(n_items, N_FF),
        in_specs=[pl.BlockSpec(memory_space=pl.ANY),
                  pl.BlockSpec((1, D_MODEL, FF_CHUNK), lambda i, f, ie, r0, nt: (ie[i], 0, fsel(i, f, nt))),
                  pl.BlockSpec((1, D_MODEL, FF_CHUNK), lambda i, f, ie, r0, nt: (ie[i], 0, N_FF + fsel(i, f, nt))),
                  pl.BlockSpec((1, FF_CHUNK, D_MODEL), lambda i, f, ie, r0, nt: (ie[i], fsel(i, f, nt), 0)),
                  pl.BlockSpec((1, 1, FF_CHUNK), lambda i, f, ie, r0, nt: (ie[i], 0, fsel(i, f, nt))),
                  pl.BlockSpec((1, 1, FF_CHUNK), lambda i, f, ie, r0, nt: (ie[i], 0, N_FF + fsel(i, f, nt))),
                  pl.BlockSpec((1, 1, D_MODEL), lambda i, f, ie, r0, nt: (ie[i], 0, 0))],
        out_specs=pl.BlockSpec(memory_space=pl.ANY),
        scratch_shapes=[pltpu.VMEM((item_rows, D_MODEL), F32),
                        pltpu.VMEM((item_rows, D_MODEL), BF16),
                        pltpu.VMEM((item_rows, D_MODEL), F32),
                        pltpu.SemaphoreType.DMA(()),
                        pltpu.SemaphoreType.DMA((TILES_PER_ITEM,))],
    )
    return pl.pallas_call(
        _experts_kernel,
        grid_spec=grid_spec,
        out_shape=jax.ShapeDtypeStruct((n_rows, D_MODEL), F32),
        compiler_params=pltpu.CompilerParams(dimension_semantics=("arbitrary", "arbitrary"),
                                             vmem_limit_bytes=V7X_VMEM_LIMIT_BYTES),
        name="experts",
    )(item_e, item_row0, item_nt, xs, w_gu, w_gu, w_dn,
      b_gu.reshape(E, 1, 2 * D_FF), b_gu.reshape(E, 1, 2 * D_FF), b_dn.reshape(E, 1, D_MODEL))


def _combine_kernel(inv_ref, x2_ref, gate_ref, fn_ref, y_ref, o_ref, rows, sem, *, tm, last_layer):
    def issue(r, c):
        for k in range(TOP_K):
            pltpu.make_async_copy(y_ref.at[pl.ds(inv_ref[0, 0, r * TOP_K + k], 1), :],
                                  rows.at[k, pl.ds(r, 1), :], sem).start(priority=k % 2)
        return c

    def drain(r, c):
        for k in range(TOP_K):
            pltpu.make_async_copy(y_ref.at[pl.ds(0, 1), :], rows.at[k, pl.ds(r, 1), :], sem).wait()
        return c

    lax.fori_loop(0, tm, issue, 0)
    lax.fori_loop(0, tm, drain, 0)
    g = gate_ref[...]
    out = x2_ref[...]
    for k in range(TOP_K):
        out = out + g[:, k:k + 1] * rows[k]
    o_ref[...] = _rms(out, fn_ref[...]) if last_layer else out


def _combine(inv, x2, gates, final_norm, y, last_layer, tm=128):
    T = x2.shape[0]
    inv3 = inv.reshape(T // tm, 1, tm * TOP_K)
    return pl.pallas_call(
        functools.partial(_combine_kernel, tm=tm, last_layer=last_layer),
        grid=(T // tm,),
        in_specs=[pl.BlockSpec((1, 1, tm * TOP_K), lambda i: (i, 0, 0), memory_space=pltpu.SMEM),
                  pl.BlockSpec((tm, D_MODEL), lambda i: (i, 0)),
                  pl.BlockSpec((tm, LANE), lambda i: (i, 0)),
                  pl.BlockSpec((1, D_MODEL), lambda i: (0, 0)),
                  pl.BlockSpec(memory_space=pl.ANY)],
        out_specs=pl.BlockSpec((tm, D_MODEL), lambda i: (i, 0)),
        out_shape=jax.ShapeDtypeStruct((T, D_MODEL), F32),
        scratch_shapes=[pltpu.VMEM((TOP_K, tm, D_MODEL), F32), pltpu.SemaphoreType.DMA(())],
        compiler_params=pltpu.CompilerParams(dimension_semantics=("arbitrary",)),
        name="combine",
    )(inv3, x2, gates, final_norm, y)


def _reorder_w_in(w_in):
    sizes = (GLA_W, GLA_W, GLA_VW, GLA_VW, GLA_RANK, Q_LORA, KV_LORA, QK_ROPE)
    offs = [0]
    for s in sizes:
        offs.append(offs[-1] + s)
    gq, gk, gv, gr, gd, mq, mkv, mkr = [w_in[:, offs[j]:offs[j + 1]] for j in range(len(sizes))]
    half = QK_ROPE // 2
    z = lambda n: jnp.zeros((D_MODEL, n), w_in.dtype)
    mkr_sw = jnp.concatenate([mkr[:, half:], mkr[:, :half]], axis=1)
    return jnp.concatenate([gq, gk, gv, gr, mq, mkv, mkr, z(LANE - QK_ROPE), mkr_sw, z(LANE - QK_ROPE),
                            gd, z(LANE - GLA_RANK)], axis=1).astype(BF16)


def _reorder_w_uq(w_uq):
    w = w_uq.reshape(Q_LORA, MLA_HEADS, QK_NOPE + QK_ROPE)
    nope = w[:, :, :QK_NOPE].reshape(Q_LORA, MLA_W)
    pe = w[:, :, QK_NOPE:]
    half = QK_ROPE // 2
    z = jnp.zeros((Q_LORA, MLA_HEADS, LANE - QK_ROPE), w_uq.dtype)
    pe_pad = jnp.concatenate([pe, z], axis=2).reshape(Q_LORA, MLA_W)
    pe_sw = jnp.concatenate([pe[:, :, half:], pe[:, :, :half], z], axis=2).reshape(Q_LORA, MLA_W)
    return jnp.concatenate([nope, pe_pad, pe_sw], axis=1).astype(BF16)


def _reorder_w_ukv(w_ukv):
    w = w_ukv.reshape(KV_LORA, MLA_HEADS, QK_NOPE + V_DIM)
    return jnp.concatenate([w[:, :, :QK_NOPE].reshape(KV_LORA, MLA_W),
                            w[:, :, QK_NOPE:].reshape(KV_LORA, MLA_W)], axis=1).astype(BF16)


def _rope_tables(positions):
    half = QK_ROPE // 2
    inv_freq = 1.0 / (ROPE_THETA ** (jnp.arange(0, QK_ROPE, 2, dtype=F32) / QK_ROPE))
    ang = positions.astype(F32).reshape(-1, 1) * inv_freq
    cos, sin = jnp.cos(ang), jnp.sin(ang)
    z = jnp.zeros((ang.shape[0], LANE - QK_ROPE), F32)
    del half
    return jnp.concatenate([cos, cos, z], axis=1), jnp.concatenate([-sin, sin, z], axis=1)


def _routing_plan(counts, topi, rank, n_items):
    padded = (counts + ROW_TILE - 1) // ROW_TILE * ROW_TILE
    pend = jnp.cumsum(padded)
    pstart = pend - padded
    inv = pstart[topi] + rank
    nt_e = padded // ROW_TILE
    ns_e = (nt_e + TILES_PER_ITEM - 1) // TILES_PER_ITEM
    cs = jnp.cumsum(ns_e)
    j = jnp.arange(n_items, dtype=I32)
    valid = j < cs[-1]
    jj = jnp.where(valid, j, cs[-1] - 1)
    e_j = jnp.sum((jj[:, None] >= cs[None, :]).astype(I32), axis=1)
    local = jj - (cs[e_j] - ns_e[e_j])
    row0 = pstart[e_j] + local * (TILES_PER_ITEM * ROW_TILE)
    nt_j = jnp.where(valid, jnp.clip(nt_e[e_j] - local * TILES_PER_ITEM, 0, TILES_PER_ITEM), 0)
    row0_and_used = jnp.concatenate([row0, pend[-1:]])
    return pend.astype(I32), inv.astype(I32), e_j.astype(I32), row0_and_used.astype(I32), nt_j.astype(I32)


def kernel(x, positions, attn_norm, w_in, gla_gate_up, gla_gate_bias, gla_out_norm, mla_q_norm, w_uq, mla_kv_norm, w_ukv, mla_out_norm, w_out, ffn_norm, router_w, router_b, w_gate_up, b_gate_up, w_down, b_down, final_norm):
    B, S, D = x.shape
    T = B * S
    A = T * TOP_K
    n_tiles = (A + N_EXPERTS * (ROW_TILE - 1)) // ROW_TILE
    n_rows = n_tiles * ROW_TILE
    n_items = (n_tiles + N_EXPERTS * (TILES_PER_ITEM - 1)) // TILES_PER_ITEM
    row2 = lambda v: v.reshape(1, -1)

    xf = x.reshape(T, D)
    cos_t, sin_t = _rope_tables(positions)
    for l in range(w_in.shape[0]):
        gup_pad = jnp.zeros((LANE, GLA_W), F32).at[:GLA_RANK].set(gla_gate_up[l]).astype(BF16)
        proj, loga = _in_proj(xf, row2(attn_norm[l]), _reorder_w_in(w_in[l]), gup_pad, row2(gla_gate_bias[l]))
        o_gla = _gla(proj, loga, row2(gla_out_norm[l]), B, S)
        q, k, v = _mla_proj(proj, cos_t, sin_t, row2(mla_q_norm[l]), row2(mla_kv_norm[l]),
                            _reorder_w_uq(w_uq[l]), _reorder_w_ukv(w_ukv[l]))
        o_mla = _mla_attn(q, k, v, B, S)
        rw_pad = jnp.zeros((D, LANE), F32).at[:, :N_EXPERTS].set(router_w[l])
        rw_hi = rw_pad.astype(BF16)
        rw_lo = (rw_pad - rw_hi.astype(F32)).astype(BF16)
        rb_pad = jnp.zeros((1, LANE), F32).at[0, :N_EXPERTS].set(router_b[l])
        x2, h2, topi, gates, rank, counts = _out_router(
            xf, o_gla, o_mla, w_out[l].astype(BF16), row2(mla_out_norm[l]), row2(ffn_norm[l]), rw_hi, rw_lo, rb_pad)
        pend, inv, item_e, item_row0, item_nt = _routing_plan(
            counts[0, :N_EXPERTS], topi[:, :TOP_K], rank[:, :TOP_K], n_items)
        xs = _dispatch(pend, inv, h2, n_rows)
        y = _experts(item_e, item_row0, item_nt, xs, w_gate_up[l], b_gate_up[l], w_down[l], b_down[l])
        xf = _combine(inv, x2, gates, row2(final_norm), y, last_layer=(l == w_in.shape[0] - 1))
    return xf.reshape(B, S, D)
```

```python
import functools

import jax
import jax.numpy as jnp
from jax import lax
from jax.experimental import pallas as pl
from jax.experimental.pallas import tpu as pltpu

F32 = jnp.float32
BF16 = jnp.bfloat16
I32 = jnp.int32
HIGHEST = lax.Precision.HIGHEST

LANE = 128
SUBLANE = 8
V7X_VMEM_LIMIT_BYTES = 60000 * 1024

D_MODEL = 2048
GLA_HEADS, GLA_DK, GLA_DV, GLA_RANK, GLA_TAU, GLA_CHUNK = 4, 128, 256, 16, 16.0, 64
MLA_HEADS, Q_LORA, KV_LORA, QK_NOPE, QK_ROPE, V_DIM = 8, 512, 512, 128, 64, 128
ROPE_THETA = 10000.0
N_EXPERTS, TOP_K, D_FF = 32, 4, 2048
SWIGLU_LIMIT, SWIGLU_ALPHA = 7.0, 1.702
EPS = 1e-6
LOG2_E = 1.4426950408889634
GLA_W = GLA_HEADS * GLA_DK
GLA_VW = GLA_HEADS * GLA_DV
MLA_W = MLA_HEADS * V_DIM
QK_PAD = 2 * LANE

PROJ_OUT_COLS = 34 * LANE
PROJ_W_COLS = 35 * LANE

ROW_TILE = 256
TILES_PER_ITEM = 5
FF_CHUNK = 512
N_FF = D_FF // FF_CHUNK


def _rms(x, g):
    return x * lax.rsqrt(jnp.mean(x * x, axis=-1, keepdims=True) + EPS) * g


def _inproj_kernel(x_ref, g_ref, w_ref, gup_ref, gb_ref, proj_ref, loga_ref):
    h = _rms(x_ref[...], g_ref[...]).astype(BF16)
    p = jnp.dot(h, w_ref[...], preferred_element_type=F32)
    proj_ref[...] = p[:, :PROJ_OUT_COLS].astype(BF16)
    z = jnp.dot(p[:, PROJ_OUT_COLS:].astype(BF16), gup_ref[...], preferred_element_type=F32) + gb_ref[...]
    log_sig = jnp.minimum(z, 0.0) - jnp.log(1.0 + jnp.exp(-jnp.abs(z)))
    loga_ref[...] = log_sig * (1.0 / GLA_TAU)


def _in_proj(xf, g, w_r, gup_pad, gbias, tm=256):
    T = xf.shape[0]
    return pl.pallas_call(
        _inproj_kernel,
        grid=(T // tm,),
        in_specs=[pl.BlockSpec((tm, D_MODEL), lambda i: (i, 0)),
                  pl.BlockSpec((1, D_MODEL), lambda i: (0, 0)),
                  pl.BlockSpec((D_MODEL, PROJ_W_COLS), lambda i: (0, 0)),
                  pl.BlockSpec((LANE, GLA_W), lambda i: (0, 0)),
                  pl.BlockSpec((1, GLA_W), lambda i: (0, 0))],
        out_specs=[pl.BlockSpec((tm, PROJ_OUT_COLS), lambda i: (i, 0)),
                   pl.BlockSpec((tm, GLA_W), lambda i: (i, 0))],
        out_shape=[jax.ShapeDtypeStruct((T, PROJ_OUT_COLS), BF16),
                   jax.ShapeDtypeStruct((T, GLA_W), F32)],
        compiler_params=pltpu.CompilerParams(dimension_semantics=("parallel",),
                                             vmem_limit_bytes=V7X_VMEM_LIMIT_BYTES),
        name="in_proj",
    )(xf, g, w_r, gup_pad, gbias)


def _gla_kernel(q_ref, k_ref, v_ref, r_ref, la_ref, gn_ref, o_ref, st_ref, *, n_chunks):
    @pl.when(pl.program_id(1) == 0)
    def _():
        st_ref[...] = jnp.zeros_like(st_ref)

    C = GLA_CHUNK
    row = lax.broadcasted_iota(I32, (C, C), 0)
    col = lax.broadcasted_iota(I32, (C, C), 1)
    causal = col <= row
    tri = causal.astype(F32)
    for c in range(n_chunks):
        sl = pl.ds(c * C, C)
        b = jnp.dot(tri, la_ref[sl, :], precision=HIGHEST, preferred_element_type=F32)
        b_last = b[C - 1:C, :]
        q_dec = (q_ref[sl, :].astype(F32) * (GLA_DK ** -0.5) * jnp.exp(b)).astype(BF16)
        kf = k_ref[sl, :].astype(F32)
        k_intra = (kf * jnp.exp(-b)).astype(BF16)
        k_state = (kf * jnp.exp(b_last - b)).astype(BF16)
        decay = jnp.exp(b_last)
        for h in range(GLA_HEADS):
            hs = slice(h * GLA_DK, (h + 1) * GLA_DK)
            vs = slice(h * GLA_DV, (h + 1) * GLA_DV)
            v = v_ref[sl, vs]
            attn = lax.dot_general(q_dec[:, hs], k_intra[:, hs], (((1,), (1,)), ((), ())),
                                   preferred_element_type=F32)
            attn = jnp.where(causal, attn, 0.0).astype(BF16)
            st = st_ref[h]
            o = jnp.dot(attn, v, preferred_element_type=F32)
            o = o + lax.dot_general(q_dec[:, hs], st.astype(BF16), (((1,), (1,)), ((), ())),
                                    preferred_element_type=F32)
            kv_t = lax.dot_general(v, k_state[:, hs], (((0,), (0,)), ((), ())),
                                   preferred_element_type=F32)
            st_ref[h] = st * decay[:, hs] + kv_t
            r = r_ref[sl, vs].astype(F32)
            o_ref[sl, vs] = (_rms(o, gn_ref[...]) * (r * jax.nn.sigmoid(r))).astype(BF16)


def _gla(proj, loga, gn, batch, seq, ct=256):
    nct = seq // ct
    T = batch * seq
    row_map = lambda blk: (lambda b, c: (b * nct + c, blk))
    return pl.pallas_call(
        functools.partial(_gla_kernel, n_chunks=ct // GLA_CHUNK),
        grid=(batch, nct),
        in_specs=[pl.BlockSpec((ct, GLA_W), row_map(0)),
                  pl.BlockSpec((ct, GLA_W), row_map(1)),
                  pl.BlockSpec((ct, GLA_VW), row_map(1)),
                  pl.BlockSpec((ct, GLA_VW), row_map(2)),
                  pl.BlockSpec((ct, GLA_W), row_map(0)),
                  pl.BlockSpec((1, GLA_DV), lambda b, c: (0, 0))],
        out_specs=pl.BlockSpec((ct, GLA_VW), row_map(0)),
        out_shape=jax.ShapeDtypeStruct((T, GLA_VW), BF16),
        scratch_shapes=[pltpu.VMEM((GLA_HEADS, GLA_DV, GLA_DK), F32)],
        compiler_params=pltpu.CompilerParams(dimension_semantics=("parallel", "arbitrary")),
        name="gla",
    )(proj, proj, proj, proj, loga, gn)


def _mla_proj_kernel(ql_ref, kvl_ref, kr_ref, krs_ref, cos_ref, sin_ref, qn_ref, kvn_ref,
                     wq_ref, wkv_ref, q_ref, k_ref, vt_ref):
    cos = cos_ref[...]
    sin = sin_ref[...]
    scale = (QK_NOPE + QK_ROPE) ** -0.5 * LOG2_E
    qlat = _rms(ql_ref[...].astype(F32), qn_ref[...]).astype(BF16)
    qall = jnp.dot(qlat, wq_ref[...], preferred_element_type=F32)
    kvlat = _rms(kvl_ref[...].astype(F32), kvn_ref[...]).astype(BF16)
    kv = jnp.dot(kvlat, wkv_ref[...], preferred_element_type=F32)
    k_pe = (kr_ref[...].astype(F32) * cos + krs_ref[...].astype(F32) * sin).astype(BF16)
    for h in range(MLA_HEADS):
        hl = slice(h * LANE, (h + 1) * LANE)
        pe = slice(MLA_W + h * LANE, MLA_W + (h + 1) * LANE)
        ps = slice(2 * MLA_W + h * LANE, 2 * MLA_W + (h + 1) * LANE)
        q_ref[h, :, 0:LANE] = (qall[:, hl] * scale).astype(BF16)
        q_ref[h, :, LANE:QK_PAD] = ((qall[:, pe] * cos + qall[:, ps] * sin) * scale).astype(BF16)
        k_ref[h, :, 0:LANE] = kv[:, hl].astype(BF16)
        k_ref[h, :, LANE:QK_PAD] = k_pe
        vt_ref[h] = kv[:, MLA_W + h * LANE:MLA_W + (h + 1) * LANE].T.astype(BF16)


def _mla_proj(proj, cos_t, sin_t, qn, kvn, wq, wkv, tm=512):
    T = proj.shape[0]
    H = MLA_HEADS
    return pl.pallas_call(
        _mla_proj_kernel,
        grid=(T // tm,),
        in_specs=[pl.BlockSpec((tm, Q_LORA), lambda i: (i, 6)),
                  pl.BlockSpec((tm, KV_LORA), lambda i: (i, 7)),
                  pl.BlockSpec((tm, LANE), lambda i: (i, 32)),
                  pl.BlockSpec((tm, LANE), lambda i: (i, 33)),
                  pl.BlockSpec((tm, LANE), lambda i: (i, 0)),
                  pl.BlockSpec((tm, LANE), lambda i: (i, 0)),
                  pl.BlockSpec((1, Q_LORA), lambda i: (0, 0)),
                  pl.BlockSpec((1, KV_LORA), lambda i: (0, 0)),
                  pl.BlockSpec((Q_LORA, 3 * MLA_W), lambda i: (0, 0)),
                  pl.BlockSpec((KV_LORA, 2 * MLA_W), lambda i: (0, 0))],
        out_specs=[pl.BlockSpec((H, tm, QK_PAD), lambda i: (0, i, 0)),
                   pl.BlockSpec((H, tm, QK_PAD), lambda i: (0, i, 0)),
                   pl.BlockSpec((H, V_DIM, tm), lambda i: (0, 0, i))],
        out_shape=[jax.ShapeDtypeStruct((H, T, QK_PAD), BF16),
                   jax.ShapeDtypeStruct((H, T, QK_PAD), BF16),
                   jax.ShapeDtypeStruct((H, V_DIM, T), BF16)],
        compiler_params=pltpu.CompilerParams(dimension_semantics=("parallel",)),
        name="mla_proj",
    )(proj, proj, proj, proj, cos_t, sin_t, qn, kvn, wq, wkv)


def _attn_kernel(q_ref, k_ref, vt_ref, o_ref, s_ref, m_ref, l_ref, acc_ref, *, tq, heads):
    qi = pl.program_id(2)
    m_ref[...] = jnp.full_like(m_ref, -jnp.inf)
    l_ref[...] = jnp.zeros_like(l_ref)
    acc_ref[...] = jnp.zeros_like(acc_ref)

    qs = [q_ref[h] for h in range(heads)]

    def scores(h, c):
        kc = k_ref[h, pl.ds(pl.multiple_of(c * tq, tq), tq), :]
        return lax.dot_general(kc, qs[h], (((1,), (1,)), ((), ())), preferred_element_type=F32)

    def reduce_chunk(h, s_t, c):
        m_old = m_ref[h]
        m_new = jnp.maximum(m_old, jnp.max(s_t, axis=0, keepdims=True))
        p_t = jnp.exp2(s_t - m_new)
        alpha = jnp.exp2(m_old - m_new)
        l_ref[h] = alpha * l_ref[h] + jnp.sum(p_t, axis=0, keepdims=True)
        vt = vt_ref[h, :, pl.ds(pl.multiple_of(c * tq, tq), tq)]
        acc_ref[h] = alpha * acc_ref[h] + jnp.dot(vt, p_t.astype(BF16), preferred_element_type=F32)
        m_ref[h] = m_new

    for h in range(heads):
        s_ref[h, 0] = scores(h, 0)

    def body(c, carry):
        for h in range(heads):
            s_t = s_ref[h, c % 2]
            s_ref[h, (c + 1) % 2] = scores(h, c + 1)
            reduce_chunk(h, s_t, c)
        return carry

    lax.fori_loop(0, qi, body, 0)
    key = lax.broadcasted_iota(I32, (tq, tq), 0)
    qry = lax.broadcasted_iota(I32, (tq, tq), 1)
    for h in range(heads):
        reduce_chunk(h, jnp.where(key <= qry, s_ref[h, qi % 2], -jnp.inf), qi)
        o_ref[:, h * V_DIM:(h + 1) * V_DIM] = (acc_ref[h] / l_ref[h]).T


def _mla_attn(q, k, vt, batch, seq, tq=512, heads=1):
    H, T, _ = q.shape
    nq = seq // tq
    return pl.pallas_call(
        functools.partial(_attn_kernel, tq=tq, heads=heads),
        grid=(batch, H // heads, nq),
        in_specs=[pl.BlockSpec((heads, tq, QK_PAD), lambda b, h, i: (h, b * nq + i, 0)),
                  pl.BlockSpec((heads, seq, QK_PAD), lambda b, h, i: (h, b, 0)),
                  pl.BlockSpec((heads, V_DIM, seq), lambda b, h, i: (h, 0, b))],
        out_specs=pl.BlockSpec((tq, heads * V_DIM), lambda b, h, i: (b * nq + i, h)),
        out_shape=jax.ShapeDtypeStruct((T, MLA_W), F32),
        scratch_shapes=[pltpu.VMEM((heads, 2, tq, tq), F32), pltpu.VMEM((heads, 1, tq), F32),
                        pltpu.VMEM((heads, 1, tq), F32), pltpu.VMEM((heads, V_DIM, tq), F32)],
        compiler_params=pltpu.CompilerParams(dimension_semantics=("parallel", "parallel", "arbitrary")),
        name="mla_attn",
    )(q, k, vt)


def _out_router_kernel(x_ref, og_ref, om_ref, wo_ref, mn_ref, fn_ref, rwh_ref, rwl_ref, rb_ref,
                       x2_ref, h2_ref, topi_ref, gate_ref, rank_ref, cnt_ref, carry_ref, *, tm):
    @pl.when(pl.program_id(0) == 0)
    def _():
        carry_ref[...] = jnp.zeros_like(carry_ref)

    om = _rms(om_ref[...], mn_ref[...]).astype(BF16)
    mix = jnp.dot(og_ref[...], wo_ref[0:GLA_VW, :], preferred_element_type=F32)
    mix = mix + jnp.dot(om, wo_ref[GLA_VW:GLA_VW + MLA_W, :], preferred_element_type=F32)
    x2 = x_ref[...] + mix
    x2_ref[...] = x2
    h2 = _rms(x2, fn_ref[...])
    h2_ref[...] = h2
    h2_hi = h2.astype(BF16)
    h2_lo = (h2 - h2_hi.astype(F32)).astype(BF16)
    logits = (jnp.dot(h2_hi, rwh_ref[...], preferred_element_type=F32)
              + jnp.dot(h2_lo, rwh_ref[...], preferred_element_type=F32)
              + jnp.dot(h2_hi, rwl_ref[...], preferred_element_type=F32)) + rb_ref[...]
    lane = lax.broadcasted_iota(I32, (tm, LANE), 1).astype(F32)
    cur = jnp.where(lane < N_EXPERTS, logits, -jnp.inf)
    vals, idxs = [], []
    for _ in range(TOP_K):
        m = jnp.max(cur, axis=-1, keepdims=True)
        idx = jnp.min(jnp.where(cur == m, lane, float(N_EXPERTS - 1)), axis=-1, keepdims=True)
        vals.append(m)
        idxs.append(idx)
        cur = jnp.where(lane == idx, -jnp.inf, cur)
    es = [jnp.exp(v - vals[0]) for v in vals]
    den = es[0] + es[1] + es[2] + es[3]
    gates = jnp.zeros((tm, LANE), F32)
    topi = jnp.zeros((tm, LANE), F32)
    multi = jnp.zeros((tm, LANE), F32)
    for k in range(TOP_K):
        gates = jnp.where(lane == k, es[k] / den, gates)
        topi = jnp.where(lane == k, idxs[k], topi)
        multi = multi + (lane == idxs[k]).astype(F32)
    gate_ref[...] = gates
    topi_ref[...] = topi.astype(I32)
    r = lax.broadcasted_iota(I32, (tm, tm), 0)
    c = lax.broadcasted_iota(I32, (tm, tm), 1)
    strict = (c < r).astype(BF16)
    pref = jnp.dot(strict, multi.astype(BF16), preferred_element_type=F32) + carry_ref[...]
    rank = jnp.zeros((tm, LANE), F32)
    for k in range(TOP_K):
        rk = jnp.sum(jnp.where(lane == idxs[k], pref, 0.0), axis=-1, keepdims=True)
        rank = jnp.where(lane == k, rk, rank)
    rank_ref[...] = rank.astype(I32)
    carry_ref[...] = carry_ref[...] + jnp.sum(multi, axis=0, keepdims=True)
    cnt_ref[...] = carry_ref[...].astype(I32)


def _out_router(xf, o_gla, o_mla, w_out, mn, fn, rw_hi, rw_lo, rb_pad, tm=256):
    T = xf.shape[0]
    row = lambda i: (i, 0)
    fixed = lambda i: (0, 0)
    return pl.pallas_call(
        functools.partial(_out_router_kernel, tm=tm),
        grid=(T // tm,),
        in_specs=[pl.BlockSpec((tm, D_MODEL), row),
                  pl.BlockSpec((tm, GLA_VW), row),
                  pl.BlockSpec((tm, MLA_W), row),
                  pl.BlockSpec((GLA_VW + MLA_W, D_MODEL), fixed),
                  pl.BlockSpec((1, MLA_W), fixed),
                  pl.BlockSpec((1, D_MODEL), fixed),
                  pl.BlockSpec((D_MODEL, LANE), fixed),
                  pl.BlockSpec((D_MODEL, LANE), fixed),
                  pl.BlockSpec((1, LANE), fixed)],
        out_specs=[pl.BlockSpec((tm, D_MODEL), row),
                   pl.BlockSpec((tm, D_MODEL), row),
                   pl.BlockSpec((tm, LANE), row),
                   pl.BlockSpec((tm, LANE), row),
                   pl.BlockSpec((tm, LANE), row),
                   pl.BlockSpec((1, LANE), fixed)],
        out_shape=[jax.ShapeDtypeStruct((T, D_MODEL), F32),
                   jax.ShapeDtypeStruct((T, D_MODEL), F32),
                   jax.ShapeDtypeStruct((T, LANE), I32),
                   jax.ShapeDtypeStruct((T, LANE), F32),
                   jax.ShapeDtypeStruct((T, LANE), I32),
                   jax.ShapeDtypeStruct((1, LANE), I32)],
        scratch_shapes=[pltpu.VMEM((1, LANE), F32)],
        compiler_params=pltpu.CompilerParams(dimension_semantics=("arbitrary",),
                                             vmem_limit_bytes=V7X_VMEM_LIMIT_BYTES),
        name="out_router",
    )(xf, o_gla, o_mla, w_out, mn, fn, rw_hi, rw_lo, rb_pad)


N_DISPATCH_SLOTS = 3


def _dispatch_kernel(pend_ref, inv_ref, h_hbm, xs_ref, hbuf, zero_ref, sem_in, sem_out, sem_fill, *, tm, n_tiles):
    i = pl.program_id(0)
    n_steps = pl.num_programs(0)

    groups = tm // SUBLANE

    def load(tile, s):
        return pltpu.make_async_copy(h_hbm.at[pl.ds(tile * groups, groups)], hbuf.at[s], sem_in.at[s])

    def row_copy(g, j, dst_row, s):
        return pltpu.make_async_copy(hbuf.at[s, g, pl.ds(j, 1), :], xs_ref.at[pl.ds(dst_row, 1), :], sem_out.at[s])

    def drain(s):
        def body(g, c):
            for j in range(SUBLANE):
                for k in range(TOP_K):
                    row_copy(g, j, 0, s).wait()
            return c

        lax.fori_loop(0, groups, body, 0)

    @pl.when(i == 0)
    def _():
        load(0, 0).start()
        zero_ref[...] = jnp.zeros_like(zero_ref)

        def fill_copy(t):
            return pltpu.make_async_copy(zero_ref, xs_ref.at[pl.ds(pl.multiple_of(t * ROW_TILE, ROW_TILE), ROW_TILE), :],
                                         sem_fill)

        def fill(e, wait):
            prev = jnp.where(e == 0, 0, pend_ref[jnp.maximum(e - 1, 0)])

            @pl.when(pend_ref[e] > prev)
            def _():
                cp = fill_copy(pend_ref[e] // ROW_TILE - 1)
                if wait:
                    cp.wait()
                else:
                    cp.start()

        def tail(t, wait):
            if wait:
                fill_copy(t).wait()
            else:
                fill_copy(t).start()

        first_unused = pend_ref[N_EXPERTS - 1] // ROW_TILE
        lax.fori_loop(0, N_EXPERTS, lambda e, c: (fill(e, False), c)[1], 0)
        lax.fori_loop(first_unused, n_tiles, lambda t, c: (tail(t, False), c)[1], 0)
        lax.fori_loop(0, N_EXPERTS, lambda e, c: (fill(e, True), c)[1], 0)
        lax.fori_loop(first_unused, n_tiles, lambda t, c: (tail(t, True), c)[1], 0)

    slot = i % N_DISPATCH_SLOTS
    next_slot = (i + 1) % N_DISPATCH_SLOTS

    @pl.when(i >= 2)
    def _():
        drain(next_slot)

    @pl.when(i + 1 < n_steps)
    def _():
        load(i + 1, next_slot).start()

    load(i, slot).wait()

    def issue(g, c):
        for j in range(SUBLANE):
            for k in range(TOP_K):
                dst_row = inv_ref[0, 0, (g * SUBLANE + j) * TOP_K + k]
                row_copy(g, j, dst_row, slot).start(priority=k % 2)
        return c

    lax.fori_loop(0, groups, issue, 0)

    @pl.when(i == n_steps - 1)
    def _():
        @pl.when(i >= 1)
        def _():
            drain((i + 2) % N_DISPATCH_SLOTS)

        drain(slot)


def _dispatch(pend, inv, h2, n_rows, tm=256):
    T = h2.shape[0]
    inv3 = inv.reshape(T // tm, 1, tm * TOP_K)
    return pl.pallas_call(
        functools.partial(_dispatch_kernel, tm=tm, n_tiles=n_rows // ROW_TILE),
        grid=(T // tm,),
        in_specs=[pl.BlockSpec(memory_space=pltpu.SMEM),
                  pl.BlockSpec((1, 1, tm * TOP_K), lambda i: (i, 0, 0), memory_space=pltpu.SMEM),
                  pl.BlockSpec(memory_space=pl.ANY)],
        out_specs=pl.BlockSpec(memory_space=pl.ANY),
        out_shape=jax.ShapeDtypeStruct((n_rows, D_MODEL), F32),
        scratch_shapes=[pltpu.VMEM((N_DISPATCH_SLOTS, tm // SUBLANE, SUBLANE, D_MODEL), F32),
                        pltpu.VMEM((ROW_TILE, D_MODEL), F32),
                        pltpu.SemaphoreType.DMA((N_DISPATCH_SLOTS,)),
                        pltpu.SemaphoreType.DMA((N_DISPATCH_SLOTS,)),
                        pltpu.SemaphoreType.DMA(())],
        compiler_params=pltpu.CompilerParams(dimension_semantics=("arbitrary",)),
        name="dispatch",
    )(pend, inv3, h2.reshape(T // SUBLANE, SUBLANE, D_MODEL))


def _experts_kernel(ie_ref, row0_ref, nt_ref, xs_ref, wg_ref, wu_ref, wd_ref, bg_ref, bu_ref, bd_ref,
                    y_ref, x_stage, x_bf, acc, sem_x, sem_y):
    i = pl.program_id(0)
    f = pl.program_id(1)
    n_items = nt_ref.shape[0]
    nt = nt_ref[i]
    row0 = row0_ref[i]
    nt_prev = jnp.where(i > 0, nt_ref[jnp.maximum(i - 1, 0)], 0)
    nxt = jnp.minimum(i + 1, n_items - 1)
    nt_next = jnp.where(i + 1 < n_items, nt_ref[nxt], 0)
    row0_next = row0_ref[nxt]

    def rows_of(t):
        return pl.ds(pl.multiple_of(t * ROW_TILE, ROW_TILE), ROW_TILE)

    def x_copy(first_row, t):
        return pltpu.make_async_copy(xs_ref.at[pl.ds(pl.multiple_of(first_row + t * ROW_TILE, ROW_TILE), ROW_TILE), :],
                                     x_stage.at[rows_of(t), :], sem_x)

    def y_copy(t, dst_tile):
        return pltpu.make_async_copy(acc.at[rows_of(t), :], y_ref.at[rows_of(dst_tile), :], sem_y.at[t])

    def loop(lo, hi, fn):
        lax.fori_loop(lo, hi, lambda t, c: (fn(t), c)[1], 0)

    @pl.when((i == 0) & (f == 0))
    def _():
        acc[pl.ds(0, ROW_TILE), :] = jnp.zeros((ROW_TILE, D_MODEL), F32)
        first_unused = row0_ref[row0_ref.shape[0] - 1] // ROW_TILE
        n_tiles = y_ref.shape[0] // ROW_TILE
        loop(first_unused, n_tiles, lambda t: y_copy(0, t).start())
        loop(first_unused, n_tiles, lambda t: y_copy(0, t).wait())

    def tile(t, width, first, last):
        if first:
            for j in range(width):
                @pl.when(t + j < nt_prev)
                def _():
                    y_copy(t + j, 0).wait()
        rows = pl.ds(pl.multiple_of(t * ROW_TILE, ROW_TILE), width * ROW_TILE)
        x = x_bf[rows, :]
        gate = jnp.dot(x, wg_ref[0].astype(BF16), preferred_element_type=F32) + bg_ref[0]
        up = jnp.dot(x, wu_ref[0].astype(BF16), preferred_element_type=F32) + bu_ref[0]
        gate = jnp.minimum(gate, SWIGLU_LIMIT)
        up = jnp.clip(up, -SWIGLU_LIMIT, SWIGLU_LIMIT)
        act = (up + 1.0) * (gate * jax.nn.sigmoid(gate * SWIGLU_ALPHA))
        contrib = jnp.dot(act.astype(BF16), wd_ref[0].astype(BF16), preferred_element_type=F32)
        if first:
            acc[rows, :] = contrib + bd_ref[0]
        else:
            acc[rows, :] += contrib
        if last:
            for j in range(width):
                y_copy(t + j, row0 // ROW_TILE + t + j).start()

    def all_tiles(first, last):
        loop(0, nt // 2, lambda p: tile(2 * p, 2, first, last))

        @pl.when(nt % 2 == 1)
        def _():
            tile(nt - 1, 1, first, last)

    @pl.when(nt > 0)
    def _():
        @pl.when(f == 0)
        def _():
            @pl.when(i == 0)
            def _():
                loop(0, nt, lambda t: x_copy(row0, t).start())

            loop(0, nt, lambda t: x_copy(row0, t).wait())

            def convert(t):
                x_bf[rows_of(t), :] = x_stage[rows_of(t), :].astype(BF16)

            loop(0, nt, convert)
            all_tiles(True, False)
            loop(nt, nt_prev, lambda t: y_copy(t, 0).wait())

        @pl.when((f > 0) & (f < N_FF - 1))
        def _():
            all_tiles(False, False)

        @pl.when(f == N_FF - 1)
        def _():
            loop(0, nt_next, lambda t: x_copy(row0_next, t).start())
            all_tiles(False, True)

            @pl.when(nt_next == 0)
            def _():
                loop(0, nt, lambda t: y_copy(t, 0).wait())


def _experts(item_e, item_row0, item_nt, xs, w_gu, b_gu, w_dn, b_dn):
    n_items = item_e.shape[0]
    n_rows = xs.shape[0]
    E = N_EXPERTS
    item_rows = TILES_PER_ITEM * ROW_TILE
    fsel = lambda i, f, nt: jnp.where(nt[i] > 0, f, N_FF - 1)
    grid_spec = pltpu.PrefetchScalarGridSpec(
        num_scalar_prefetch=3,
        grid=(n_items, N_FF),
        in_specs=[pl.BlockSpec(memory_space=pl.ANY),
                  pl.BlockSpec((1, D_MODEL, FF_CHUNK), lambda i, f, ie, r0, nt: (ie[i], 0, fsel(i, f, nt))),
                  pl.BlockSpec((1, D_MODEL, FF_CHUNK), lambda i, f, ie, r0, nt: (ie[i], 0, N_FF + fsel(i, f, nt))),
                  pl.BlockSpec((1, FF_CHUNK, D_MODEL), lambda i, f, ie, r0, nt: (ie[i], fsel(i, f, nt), 0)),
                  pl.BlockSpec((1, 1, FF_CHUNK), lambda i, f, ie, r0, nt: (ie[i], 0, fsel(i, f, nt))),
                  pl.BlockSpec((1, 1, FF_CHUNK), lambda i, f, ie, r0, nt: (ie[i], 0, N_FF + fsel(i, f, nt))),
                  pl.BlockSpec((1, 1, D_MODEL), lambda i, f, ie, r0, nt: (ie[i], 0, 0))],
        out_specs=pl.BlockSpec(memory_space=pl.ANY),
        scratch_shapes=[pltpu.VMEM((item_rows, D_MODEL), F32),
                        pltpu.VMEM((item_rows, D_MODEL), BF16),
                        pltpu.VMEM((item_rows, D_MODEL), F32),
                        pltpu.SemaphoreType.DMA(()),
                        pltpu.SemaphoreType.DMA((TILES_PER_ITEM,))],
    )
    return pl.pallas_call(
        _experts_kernel,
        grid_spec=grid_spec,
        out_shape=jax.ShapeDtypeStruct((n_rows, D_MODEL), F32),
        compiler_params=pltpu.CompilerParams(dimension_semantics=("arbitrary", "arbitrary"),
                                             vmem_limit_bytes=V7X_VMEM_LIMIT_BYTES),
        name="experts",
    )(item_e, item_row0, item_nt, xs, w_gu, w_gu, w_dn,
      b_gu.reshape(E, 1, 2 * D_FF), b_gu.reshape(E, 1, 2 * D_FF), b_dn.reshape(E, 1, D_MODEL))


def _combine_kernel(inv_ref, inv_next_ref, x2_ref, gate_ref, fn_ref, y_ref, o_ref, rows, sem, *, tm, last_layer):
    i = pl.program_id(0)
    slot = i % 2

    groups = tm // SUBLANE

    def row_copy(idx_ref, g, j, k, s):
        return pltpu.make_async_copy(y_ref.at[pl.ds(idx_ref[0, 0, (g * SUBLANE + j) * TOP_K + k], 1), :],
                                     rows.at[s, k, g, pl.ds(j, 1), :], sem.at[s])

    def gather(idx_ref, s):
        def issue(g, c):
            for j in range(SUBLANE):
                for k in range(TOP_K):
                    row_copy(idx_ref, g, j, k, s).start(priority=k % 2)
            return c

        lax.fori_loop(0, groups, issue, 0)

    @pl.when(i == 0)
    def _():
        gather(inv_ref, 0)

    @pl.when(i + 1 < pl.num_programs(0))
    def _():
        gather(inv_next_ref, 1 - slot)

    def drain(g, c):
        for j in range(SUBLANE):
            for k in range(TOP_K):
                row_copy(inv_ref, g, j, k, slot).wait()
        return c

    lax.fori_loop(0, groups, drain, 0)
    gate = gate_ref[...]
    out = x2_ref[...]
    for k in range(TOP_K):
        out = out + gate[:, k:k + 1] * rows[slot, k].reshape(tm, D_MODEL)
    o_ref[...] = _rms(out, fn_ref[...]) if last_layer else out


def _combine(inv, x2, gates, final_norm, y, last_layer, tm=128):
    T = x2.shape[0]
    inv3 = inv.reshape(T // tm, 1, tm * TOP_K)
    return pl.pallas_call(
        functools.partial(_combine_kernel, tm=tm, last_layer=last_layer),
        grid=(T // tm,),
        in_specs=[pl.BlockSpec((1, 1, tm * TOP_K), lambda i: (i, 0, 0), memory_space=pltpu.SMEM),
                  pl.BlockSpec((1, 1, tm * TOP_K), lambda i: (jnp.minimum(i + 1, T // tm - 1), 0, 0),
                               memory_space=pltpu.SMEM),
                  pl.BlockSpec((tm, D_MODEL), lambda i: (i, 0)),
                  pl.BlockSpec((tm, LANE), lambda i: (i, 0)),
                  pl.BlockSpec((1, D_MODEL), lambda i: (0, 0)),
                  pl.BlockSpec(memory_space=pl.ANY)],
        out_specs=pl.BlockSpec((tm, D_MODEL), lambda i: (i, 0)),
        out_shape=jax.ShapeDtypeStruct((T, D_MODEL), F32),
        scratch_shapes=[pltpu.VMEM((2, TOP_K, tm // SUBLANE, SUBLANE, D_MODEL), F32),
                        pltpu.SemaphoreType.DMA((2,))],
        compiler_params=pltpu.CompilerParams(dimension_semantics=("arbitrary",)),
        name="combine",
    )(inv3, inv3, x2, gates, final_norm, y)


def _reorder_w_in_kernel(w_ref, o_ref):
    w = w_ref[...]
    tk = w.shape[0]
    a = 2 * GLA_W + 2 * GLA_VW
    lat = a + GLA_RANK
    kr = lat + Q_LORA + KV_LORA
    half = QK_ROPE // 2
    zeros = lambda n: jnp.zeros((tk, n), F32)
    o_ref[:, 0:a] = w[:, 0:a].astype(BF16)
    o_ref[:, a:a + Q_LORA + KV_LORA] = w[:, lat:kr].astype(BF16)
    o_ref[:, 32 * LANE:33 * LANE] = jnp.concatenate([w[:, kr:kr + QK_ROPE], zeros(LANE - QK_ROPE)], axis=1).astype(BF16)
    o_ref[:, 33 * LANE:34 * LANE] = jnp.concatenate(
        [w[:, kr + half:kr + QK_ROPE], w[:, kr:kr + half], zeros(LANE - QK_ROPE)], axis=1).astype(BF16)
    o_ref[:, 34 * LANE:35 * LANE] = jnp.concatenate([w[:, a:lat], zeros(LANE - GLA_RANK)], axis=1).astype(BF16)


def _reorder_w_in(w_in, tk=256):
    in_cols = w_in.shape[1]
    return pl.pallas_call(
        _reorder_w_in_kernel,
        grid=(D_MODEL // tk,),
        in_specs=[pl.BlockSpec((tk, in_cols), lambda i: (i, 0))],
        out_specs=pl.BlockSpec((tk, PROJ_W_COLS), lambda i: (i, 0)),
        out_shape=jax.ShapeDtypeStruct((D_MODEL, PROJ_W_COLS), BF16),
        compiler_params=pltpu.CompilerParams(dimension_semantics=("parallel",)),
        name="reorder_w_in",
    )(w_in)


def _reorder_w_uq(w_uq):
    w = w_uq.reshape(Q_LORA, MLA_HEADS, QK_NOPE + QK_ROPE)
    nope = w[:, :, :QK_NOPE].reshape(Q_LORA, MLA_W)
    pe = w[:, :, QK_NOPE:]
    half = QK_ROPE // 2
    z = jnp.zeros((Q_LORA, MLA_HEADS, LANE - QK_ROPE), w_uq.dtype)
    pe_pad = jnp.concatenate([pe, z], axis=2).reshape(Q_LORA, MLA_W)
    pe_sw = jnp.concatenate([pe[:, :, half:], pe[:, :, :half], z], axis=2).reshape(Q_LORA, MLA_W)
    return jnp.concatenate([nope, pe_pad, pe_sw], axis=1).astype(BF16)


def _reorder_w_ukv(w_ukv):
    w = w_ukv.reshape(KV_LORA, MLA_HEADS, QK_NOPE + V_DIM)
    return jnp.concatenate([w[:, :, :QK_NOPE].reshape(KV_LORA, MLA_W),
                            w[:, :, QK_NOPE:].reshape(KV_LORA, MLA_W)], axis=1).astype(BF16)


def _rope_tables(positions):
    half = QK_ROPE // 2
    inv_freq = 1.0 / (ROPE_THETA ** (jnp.arange(0, QK_ROPE, 2, dtype=F32) / QK_ROPE))
    ang = positions.astype(F32).reshape(-1, 1) * inv_freq
    cos, sin = jnp.cos(ang), jnp.sin(ang)
    z = jnp.zeros((ang.shape[0], LANE - QK_ROPE), F32)
    del half
    return jnp.concatenate([cos, cos, z], axis=1), jnp.concatenate([-sin, sin, z], axis=1)


def _routing_plan(counts, topi, rank, n_items):
    padded = (counts + ROW_TILE - 1) // ROW_TILE * ROW_TILE
    pend = jnp.cumsum(padded)
    pstart = pend - padded
    inv = pstart[topi] + rank
    nt_e = padded // ROW_TILE
    ns_e = (nt_e + TILES_PER_ITEM - 1) // TILES_PER_ITEM
    cs = jnp.cumsum(ns_e)
    j = jnp.arange(n_items, dtype=I32)
    valid = j < cs[-1]
    jj = jnp.where(valid, j, cs[-1] - 1)
    e_j = jnp.sum((jj[:, None] >= cs[None, :]).astype(I32), axis=1)
    local = jj - (cs[e_j] - ns_e[e_j])
    row0 = pstart[e_j] + local * (TILES_PER_ITEM * ROW_TILE)
    nt_j = jnp.where(valid, jnp.clip(nt_e[e_j] - local * TILES_PER_ITEM, 0, TILES_PER_ITEM), 0)
    row0_and_used = jnp.concatenate([row0, pend[-1:]])
    return pend.astype(I32), inv.astype(I32), e_j.astype(I32), row0_and_used.astype(I32), nt_j.astype(I32)


def kernel(x, positions, attn_norm, w_in, gla_gate_up, gla_gate_bias, gla_out_norm, mla_q_norm, w_uq, mla_kv_norm, w_ukv, mla_out_norm, w_out, ffn_norm, router_w, router_b, w_gate_up, b_gate_up, w_down, b_down, final_norm):
    B, S, D = x.shape
    T = B * S
    A = T * TOP_K
    n_tiles = (A + N_EXPERTS * (ROW_TILE - 1)) // ROW_TILE
    n_rows = n_tiles * ROW_TILE
    n_items = (n_tiles + N_EXPERTS * (TILES_PER_ITEM - 1)) // TILES_PER_ITEM
    row2 = lambda v: v.reshape(1, -1)

    xf = x.reshape(T, D)
    cos_t, sin_t = _rope_tables(positions)
    for l in range(w_in.shape[0]):
        gup_pad = jnp.zeros((LANE, GLA_W), F32).at[:GLA_RANK].set(gla_gate_up[l]).astype(BF16)
        proj, loga = _in_proj(xf, row2(attn_norm[l]), _reorder_w_in(w_in[l]), gup_pad, row2(gla_gate_bias[l]))
        o_gla = _gla(proj, loga, row2(gla_out_norm[l]), B, S)
        q, k, v = _mla_proj(proj, cos_t, sin_t, row2(mla_q_norm[l]), row2(mla_kv_norm[l]),
                            _reorder_w_uq(w_uq[l]), _reorder_w_ukv(w_ukv[l]))
        o_mla = _mla_attn(q, k, v, B, S)
        rw_pad = jnp.zeros((D, LANE), F32).at[:, :N_EXPERTS].set(router_w[l])
        rw_hi = rw_pad.astype(BF16)
        rw_lo = (rw_pad - rw_hi.astype(F32)).astype(BF16)
        rb_pad = jnp.zeros((1, LANE), F32).at[0, :N_EXPERTS].set(router_b[l])
        x2, h2, topi, gates, rank, counts = _out_router(
            xf, o_gla, o_mla, w_out[l].astype(BF16), row2(mla_out_norm[l]), row2(ffn_norm[l]), rw_hi, rw_lo, rb_pad)
        pend, inv, item_e, item_row0, item_nt = _routing_plan(
            counts[0, :N_EXPERTS], topi[:, :TOP_K], rank[:, :TOP_K], n_items)
        xs = _dispatch(pend, inv, h2, n_rows)
        y = _experts(item_e, item_row0, item_nt, xs, w_gate_up[l], b_gate_up[l], w_down[l], b_down[l])
        xf = _combine(inv, x2, gates, row2(final_norm), y, last_layer=(l == w_in.shape[0] - 1))
    return xf.reshape(B, S, D)
```

```python
import functools

import jax
import jax.numpy as jnp
from jax import lax
from jax.experimental import pallas as pl
from jax.experimental.pallas import tpu as pltpu

F32 = jnp.float32
BF16 = jnp.bfloat16
I32 = jnp.int32
HIGHEST = lax.Precision.HIGHEST

LANE = 128
SUBLANE = 8
V7X_VMEM_LIMIT_BYTES = 60000 * 1024

D_MODEL = 2048
GLA_HEADS, GLA_DK, GLA_DV, GLA_RANK, GLA_TAU, GLA_CHUNK = 4, 128, 256, 16, 16.0, 64
MLA_HEADS, Q_LORA, KV_LORA, QK_NOPE, QK_ROPE, V_DIM = 8, 512, 512, 128, 64, 128
ROPE_THETA = 10000.0
N_EXPERTS, TOP_K, D_FF = 32, 4, 2048
SWIGLU_LIMIT, SWIGLU_ALPHA = 7.0, 1.702
EPS = 1e-6
LOG2_E = 1.4426950408889634
GLA_W = GLA_HEADS * GLA_DK
GLA_VW = GLA_HEADS * GLA_DV
MLA_W = MLA_HEADS * V_DIM
QK_PAD = 2 * LANE

PROJ_OUT_COLS = 34 * LANE
PROJ_W_COLS = 35 * LANE

ROW_TILE = 256
TILES_PER_ITEM = 5
FF_CHUNK = 512
N_FF = D_FF // FF_CHUNK


def _rms(x, g):
    return x * lax.rsqrt(jnp.mean(x * x, axis=-1, keepdims=True) + EPS) * g


def _inproj_kernel(x_ref, g_ref, w_ref, gup_ref, gb_ref, proj_ref, loga_ref):
    h = _rms(x_ref[...], g_ref[...]).astype(BF16)
    p = jnp.dot(h, w_ref[...], preferred_element_type=F32)
    proj_ref[...] = p[:, :PROJ_OUT_COLS].astype(BF16)
    z = jnp.dot(p[:, PROJ_OUT_COLS:].astype(BF16), gup_ref[...], preferred_element_type=F32) + gb_ref[...]
    log_sig = jnp.minimum(z, 0.0) - jnp.log(1.0 + jnp.exp(-jnp.abs(z)))
    loga_ref[...] = log_sig * (1.0 / GLA_TAU)


def _in_proj(xf, g, w_r, gup_pad, gbias, tm=256):
    T = xf.shape[0]
    return pl.pallas_call(
        _inproj_kernel,
        grid=(T // tm,),
        in_specs=[pl.BlockSpec((tm, D_MODEL), lambda i: (i, 0)),
                  pl.BlockSpec((1, D_MODEL), lambda i: (0, 0)),
                  pl.BlockSpec((D_MODEL, PROJ_W_COLS), lambda i: (0, 0)),
                  pl.BlockSpec((LANE, GLA_W), lambda i: (0, 0)),
                  pl.BlockSpec((1, GLA_W), lambda i: (0, 0))],
        out_specs=[pl.BlockSpec((tm, PROJ_OUT_COLS), lambda i: (i, 0)),
                   pl.BlockSpec((tm, GLA_W), lambda i: (i, 0))],
        out_shape=[jax.ShapeDtypeStruct((T, PROJ_OUT_COLS), BF16),
                   jax.ShapeDtypeStruct((T, GLA_W), F32)],
        compiler_params=pltpu.CompilerParams(dimension_semantics=("parallel",),
                                             vmem_limit_bytes=V7X_VMEM_LIMIT_BYTES),
        name="in_proj",
    )(xf, g, w_r, gup_pad, gbias)


def _gla_kernel(q_ref, k_ref, v_ref, r_ref, la_ref, gn_ref, o_ref, st_ref, *, n_chunks):
    @pl.when(pl.program_id(1) == 0)
    def _():
        st_ref[...] = jnp.zeros_like(st_ref)

    C = GLA_CHUNK
    row = lax.broadcasted_iota(I32, (C, C), 0)
    col = lax.broadcasted_iota(I32, (C, C), 1)
    causal = col <= row
    tri = causal.astype(F32)
    for c in range(n_chunks):
        sl = pl.ds(c * C, C)
        b = jnp.dot(tri, la_ref[sl, :], precision=HIGHEST, preferred_element_type=F32)
        b_last = b[C - 1:C, :]
        q_dec = (q_ref[sl, :].astype(F32) * (GLA_DK ** -0.5) * jnp.exp(b)).astype(BF16)
        kf = k_ref[sl, :].astype(F32)
        k_intra = (kf * jnp.exp(-b)).astype(BF16)
        k_state = (kf * jnp.exp(b_last - b)).astype(BF16)
        decay = jnp.exp(b_last)
        for h in range(GLA_HEADS):
            hs = slice(h * GLA_DK, (h + 1) * GLA_DK)
            vs = slice(h * GLA_DV, (h + 1) * GLA_DV)
            v = v_ref[sl, vs]
            attn = lax.dot_general(q_dec[:, hs], k_intra[:, hs], (((1,), (1,)), ((), ())),
                                   preferred_element_type=F32)
            attn = jnp.where(causal, attn, 0.0).astype(BF16)
            st = st_ref[h]
            o = jnp.dot(attn, v, preferred_element_type=F32)
            o = o + lax.dot_general(q_dec[:, hs], st.astype(BF16), (((1,), (1,)), ((), ())),
                                    preferred_element_type=F32)
            kv_t = lax.dot_general(v, k_state[:, hs], (((0,), (0,)), ((), ())),
                                   preferred_element_type=F32)
            st_ref[h] = st * decay[:, hs] + kv_t
            r = r_ref[sl, vs].astype(F32)
            o_ref[sl, vs] = (_rms(o, gn_ref[...]) * (r * jax.nn.sigmoid(r))).astype(BF16)


def _gla(proj, loga, gn, batch, seq, ct=256):
    nct = seq // ct
    T = batch * seq
    row_map = lambda blk: (lambda b, c: (b * nct + c, blk))
    return pl.pallas_call(
        functools.partial(_gla_kernel, n_chunks=ct // GLA_CHUNK),
        grid=(batch, nct),
        in_specs=[pl.BlockSpec((ct, GLA_W), row_map(0)),
                  pl.BlockSpec((ct, GLA_W), row_map(1)),
                  pl.BlockSpec((ct, GLA_VW), row_map(1)),
                  pl.BlockSpec((ct, GLA_VW), row_map(2)),
                  pl.BlockSpec((ct, GLA_W), row_map(0)),
                  pl.BlockSpec((1, GLA_DV), lambda b, c: (0, 0))],
        out_specs=pl.BlockSpec((ct, GLA_VW), row_map(0)),
        out_shape=jax.ShapeDtypeStruct((T, GLA_VW), BF16),
        scratch_shapes=[pltpu.VMEM((GLA_HEADS, GLA_DV, GLA_DK), F32)],
        compiler_params=pltpu.CompilerParams(dimension_semantics=("parallel", "arbitrary")),
        name="gla",
    )(proj, proj, proj, proj, loga, gn)


def _mla_proj_kernel(ql_ref, kvl_ref, kr_ref, krs_ref, cos_ref, sin_ref, qn_ref, kvn_ref,
                     wq_ref, wkv_ref, q_ref, k_ref, vt_ref):
    cos = cos_ref[...]
    sin = sin_ref[...]
    scale = (QK_NOPE + QK_ROPE) ** -0.5 * LOG2_E
    qlat = _rms(ql_ref[...].astype(F32), qn_ref[...]).astype(BF16)
    qall = jnp.dot(qlat, wq_ref[...], preferred_element_type=F32)
    kvlat = _rms(kvl_ref[...].astype(F32), kvn_ref[...]).astype(BF16)
    kv = jnp.dot(kvlat, wkv_ref[...], preferred_element_type=F32)
    k_pe = (kr_ref[...].astype(F32) * cos + krs_ref[...].astype(F32) * sin).astype(BF16)
    for h in range(MLA_HEADS):
        hl = slice(h * LANE, (h + 1) * LANE)
        pe = slice(MLA_W + h * LANE, MLA_W + (h + 1) * LANE)
        ps = slice(2 * MLA_W + h * LANE, 2 * MLA_W + (h + 1) * LANE)
        q_ref[h, :, 0:LANE] = (qall[:, hl] * scale).astype(BF16)
        q_ref[h, :, LANE:QK_PAD] = ((qall[:, pe] * cos + qall[:, ps] * sin) * scale).astype(BF16)
        k_ref[h, :, 0:LANE] = kv[:, hl].astype(BF16)
        k_ref[h, :, LANE:QK_PAD] = k_pe
        vt_ref[h] = kv[:, MLA_W + h * LANE:MLA_W + (h + 1) * LANE].T.astype(BF16)


def _mla_proj(proj, cos_t, sin_t, qn, kvn, wq, wkv, tm=512):
    T = proj.shape[0]
    H = MLA_HEADS
    return pl.pallas_call(
        _mla_proj_kernel,
        grid=(T // tm,),
        in_specs=[pl.BlockSpec((tm, Q_LORA), lambda i: (i, 6)),
                  pl.BlockSpec((tm, KV_LORA), lambda i: (i, 7)),
                  pl.BlockSpec((tm, LANE), lambda i: (i, 32)),
                  pl.BlockSpec((tm, LANE), lambda i: (i, 33)),
                  pl.BlockSpec((tm, LANE), lambda i: (i, 0)),
                  pl.BlockSpec((tm, LANE), lambda i: (i, 0)),
                  pl.BlockSpec((1, Q_LORA), lambda i: (0, 0)),
                  pl.BlockSpec((1, KV_LORA), lambda i: (0, 0)),
                  pl.BlockSpec((Q_LORA, 3 * MLA_W), lambda i: (0, 0)),
                  pl.BlockSpec((KV_LORA, 2 * MLA_W), lambda i: (0, 0))],
        out_specs=[pl.BlockSpec((H, tm, QK_PAD), lambda i: (0, i, 0)),
                   pl.BlockSpec((H, tm, QK_PAD), lambda i: (0, i, 0)),
                   pl.BlockSpec((H, V_DIM, tm), lambda i: (0, 0, i))],
        out_shape=[jax.ShapeDtypeStruct((H, T, QK_PAD), BF16),
                   jax.ShapeDtypeStruct((H, T, QK_PAD), BF16),
                   jax.ShapeDtypeStruct((H, V_DIM, T), BF16)],
        compiler_params=pltpu.CompilerParams(dimension_semantics=("parallel",)),
        name="mla_proj",
    )(proj, proj, proj, proj, cos_t, sin_t, qn, kvn, wq, wkv)


def _attn_kernel(q_ref, k_ref, vt_ref, o_ref, s_ref, m_ref, l_ref, acc_ref, *, tq, heads):
    qi = pl.program_id(2)
    m_ref[...] = jnp.full_like(m_ref, -jnp.inf)
    l_ref[...] = jnp.zeros_like(l_ref)
    acc_ref[...] = jnp.zeros_like(acc_ref)

    qs = [q_ref[h] for h in range(heads)]

    def scores(h, c):
        kc = k_ref[h, pl.ds(pl.multiple_of(c * tq, tq), tq), :]
        return lax.dot_general(kc, qs[h], (((1,), (1,)), ((), ())), preferred_element_type=F32)

    def reduce_chunk(h, s_t, c):
        m_old = m_ref[h]
        m_new = jnp.maximum(m_old, jnp.max(s_t, axis=0, keepdims=True))
        p_t = jnp.exp2(s_t - m_new)
        alpha = jnp.exp2(m_old - m_new)
        l_ref[h] = alpha * l_ref[h] + jnp.sum(p_t, axis=0, keepdims=True)
        vt = vt_ref[h, :, pl.ds(pl.multiple_of(c * tq, tq), tq)]
        acc_ref[h] = alpha * acc_ref[h] + jnp.dot(vt, p_t.astype(BF16), preferred_element_type=F32)
        m_ref[h] = m_new

    for h in range(heads):
        s_ref[h, 0] = scores(h, 0)

    def body(c, carry):
        for h in range(heads):
            s_t = s_ref[h, c % 2]
            s_ref[h, (c + 1) % 2] = scores(h, c + 1)
            reduce_chunk(h, s_t, c)
        return carry

    lax.fori_loop(0, qi, body, 0)
    key = lax.broadcasted_iota(I32, (tq, tq), 0)
    qry = lax.broadcasted_iota(I32, (tq, tq), 1)
    for h in range(heads):
        reduce_chunk(h, jnp.where(key <= qry, s_ref[h, qi % 2], -jnp.inf), qi)
        o_ref[:, h * V_DIM:(h + 1) * V_DIM] = (acc_ref[h] / l_ref[h]).T


def _mla_attn(q, k, vt, batch, seq, tq=512, heads=1):
    H, T, _ = q.shape
    nq = seq // tq
    return pl.pallas_call(
        functools.partial(_attn_kernel, tq=tq, heads=heads),
        grid=(batch, H // heads, nq),
        in_specs=[pl.BlockSpec((heads, tq, QK_PAD), lambda b, h, i: (h, b * nq + i, 0)),
                  pl.BlockSpec((heads, seq, QK_PAD), lambda b, h, i: (h, b, 0)),
                  pl.BlockSpec((heads, V_DIM, seq), lambda b, h, i: (h, 0, b))],
        out_specs=pl.BlockSpec((tq, heads * V_DIM), lambda b, h, i: (b * nq + i, h)),
        out_shape=jax.ShapeDtypeStruct((T, MLA_W), F32),
        scratch_shapes=[pltpu.VMEM((heads, 2, tq, tq), F32), pltpu.VMEM((heads, 1, tq), F32),
                        pltpu.VMEM((heads, 1, tq), F32), pltpu.VMEM((heads, V_DIM, tq), F32)],
        compiler_params=pltpu.CompilerParams(dimension_semantics=("parallel", "parallel", "arbitrary")),
        name="mla_attn",
    )(q, k, vt)


def _out_router_kernel(x_ref, og_ref, om_ref, wo_ref, mn_ref, fn_ref, rwh_ref, rwl_ref, rb_ref,
                       x2_ref, h2_ref, topi_ref, gate_ref, rank_ref, cnt_ref, carry_ref, *, tm):
    @pl.when(pl.program_id(0) == 0)
    def _():
        carry_ref[...] = jnp.zeros_like(carry_ref)

    om = _rms(om_ref[...], mn_ref[...]).astype(BF16)
    mix = jnp.dot(og_ref[...], wo_ref[0:GLA_VW, :], preferred_element_type=F32)
    mix = mix + jnp.dot(om, wo_ref[GLA_VW:GLA_VW + MLA_W, :], preferred_element_type=F32)
    x2 = x_ref[...] + mix
    x2_ref[...] = x2
    h2 = _rms(x2, fn_ref[...])
    h2_ref[...] = h2
    h2_hi = h2.astype(BF16)
    h2_lo = (h2 - h2_hi.astype(F32)).astype(BF16)
    logits = (jnp.dot(h2_hi, rwh_ref[...], preferred_element_type=F32)
              + jnp.dot(h2_lo, rwh_ref[...], preferred_element_type=F32)
              + jnp.dot(h2_hi, rwl_ref[...], preferred_element_type=F32)) + rb_ref[...]
    lane = lax.broadcasted_iota(I32, (tm, LANE), 1).astype(F32)
    cur = jnp.where(lane < N_EXPERTS, logits, -jnp.inf)
    vals, idxs = [], []
    for _ in range(TOP_K):
        m = jnp.max(cur, axis=-1, keepdims=True)
        idx = jnp.min(jnp.where(cur == m, lane, float(N_EXPERTS - 1)), axis=-1, keepdims=True)
        vals.append(m)
        idxs.append(idx)
        cur = jnp.where(lane == idx, -jnp.inf, cur)
    es = [jnp.exp(v - vals[0]) for v in vals]
    den = es[0] + es[1] + es[2] + es[3]
    gates = jnp.zeros((tm, LANE), F32)
    topi = jnp.zeros((tm, LANE), F32)
    multi = jnp.zeros((tm, LANE), F32)
    for k in range(TOP_K):
        gates = jnp.where(lane == k, es[k] / den, gates)
        topi = jnp.where(lane == k, idxs[k], topi)
        multi = multi + (lane == idxs[k]).astype(F32)
    gate_ref[...] = gates
    topi_ref[...] = topi.astype(I32)
    r = lax.broadcasted_iota(I32, (tm, tm), 0)
    c = lax.broadcasted_iota(I32, (tm, tm), 1)
    strict = (c < r).astype(BF16)
    pref = jnp.dot(strict, multi.astype(BF16), preferred_element_type=F32) + carry_ref[...]
    rank = jnp.zeros((tm, LANE), F32)
    for k in range(TOP_K):
        rk = jnp.sum(jnp.where(lane == idxs[k], pref, 0.0), axis=-1, keepdims=True)
        rank = jnp.where(lane == k, rk, rank)
    rank_ref[...] = rank.astype(I32)
    carry_ref[...] = carry_ref[...] + jnp.sum(multi, axis=0, keepdims=True)
    cnt_ref[...] = carry_ref[...].astype(I32)


def _out_router(xf, o_gla, o_mla, w_out, mn, fn, rw_hi, rw_lo, rb_pad, tm=256):
    T = xf.shape[0]
    row = lambda i: (i, 0)
    fixed = lambda i: (0, 0)
    return pl.pallas_call(
        functools.partial(_out_router_kernel, tm=tm),
        grid=(T // tm,),
        in_specs=[pl.BlockSpec((tm, D_MODEL), row),
                  pl.BlockSpec((tm, GLA_VW), row),
                  pl.BlockSpec((tm, MLA_W), row),
                  pl.BlockSpec((GLA_VW + MLA_W, D_MODEL), fixed),
                  pl.BlockSpec((1, MLA_W), fixed),
                  pl.BlockSpec((1, D_MODEL), fixed),
                  pl.BlockSpec((D_MODEL, LANE), fixed),
                  pl.BlockSpec((D_MODEL, LANE), fixed),
                  pl.BlockSpec((1, LANE), fixed)],
        out_specs=[pl.BlockSpec((tm, D_MODEL), row),
                   pl.BlockSpec((tm, D_MODEL), row),
                   pl.BlockSpec((tm, LANE), row),
                   pl.BlockSpec((tm, LANE), row),
                   pl.BlockSpec((tm, LANE), row),
                   pl.BlockSpec((1, LANE), fixed)],
        out_shape=[jax.ShapeDtypeStruct((T, D_MODEL), F32),
                   jax.ShapeDtypeStruct((T, D_MODEL), F32),
                   jax.ShapeDtypeStruct((T, LANE), I32),
                   jax.ShapeDtypeStruct((T, LANE), F32),
                   jax.ShapeDtypeStruct((T, LANE), I32),
                   jax.ShapeDtypeStruct((1, LANE), I32)],
        scratch_shapes=[pltpu.VMEM((1, LANE), F32)],
        compiler_params=pltpu.CompilerParams(dimension_semantics=("arbitrary",),
                                             vmem_limit_bytes=V7X_VMEM_LIMIT_BYTES),
        name="out_router",
    )(xf, o_gla, o_mla, w_out, mn, fn, rw_hi, rw_lo, rb_pad)


N_DISPATCH_SLOTS = 3


def _dispatch_kernel(pend_ref, inv_ref, h_hbm, xs_ref, hbuf, zero_ref, sem_in, sem_out, sem_fill, *, tm, n_tiles):
    i = pl.program_id(0)
    n_steps = pl.num_programs(0)

    groups = tm // SUBLANE

    def load(tile, s):
        return pltpu.make_async_copy(h_hbm.at[pl.ds(tile * groups, groups)], hbuf.at[s], sem_in.at[s])

    def row_copy(g, j, dst_row, s):
        return pltpu.make_async_copy(hbuf.at[s, g, pl.ds(j, 1), :], xs_ref.at[pl.ds(dst_row, 1), :], sem_out.at[s])

    def drain(s):
        def body(g, c):
            for j in range(SUBLANE):
                for k in range(TOP_K):
                    row_copy(g, j, 0, s).wait()
            return c

        lax.fori_loop(0, groups, body, 0)

    @pl.when(i == 0)
    def _():
        load(0, 0).start()
        zero_ref[...] = jnp.zeros_like(zero_ref)

        def fill_copy(t):
            return pltpu.make_async_copy(zero_ref, xs_ref.at[pl.ds(pl.multiple_of(t * ROW_TILE, ROW_TILE), ROW_TILE), :],
                                         sem_fill)

        def fill(e, wait):
            prev = jnp.where(e == 0, 0, pend_ref[jnp.maximum(e - 1, 0)])

            @pl.when(pend_ref[e] > prev)
            def _():
                cp = fill_copy(pend_ref[e] // ROW_TILE - 1)
                if wait:
                    cp.wait()
                else:
                    cp.start()

        def tail(t, wait):
            if wait:
                fill_copy(t).wait()
            else:
                fill_copy(t).start()

        first_unused = pend_ref[N_EXPERTS - 1] // ROW_TILE
        lax.fori_loop(0, N_EXPERTS, lambda e, c: (fill(e, False), c)[1], 0)
        lax.fori_loop(first_unused, n_tiles, lambda t, c: (tail(t, False), c)[1], 0)
        lax.fori_loop(0, N_EXPERTS, lambda e, c: (fill(e, True), c)[1], 0)
        lax.fori_loop(first_unused, n_tiles, lambda t, c: (tail(t, True), c)[1], 0)

    slot = i % N_DISPATCH_SLOTS
    next_slot = (i + 1) % N_DISPATCH_SLOTS

    @pl.when(i >= 2)
    def _():
        drain(next_slot)

    @pl.when(i + 1 < n_steps)
    def _():
        load(i + 1, next_slot).start()

    load(i, slot).wait()

    def issue(g, c):
        for j in range(SUBLANE):
            for k in range(TOP_K):
                dst_row = inv_ref[0, 0, (g * SUBLANE + j) * TOP_K + k]
                row_copy(g, j, dst_row, slot).start(priority=k % 2)
        return c

    lax.fori_loop(0, groups, issue, 0)

    @pl.when(i == n_steps - 1)
    def _():
        @pl.when(i >= 1)
        def _():
            drain((i + 2) % N_DISPATCH_SLOTS)

        drain(slot)


def _dispatch(pend, inv, h2, n_rows, tm=256):
    T = h2.shape[0]
    inv3 = inv.reshape(T // tm, 1, tm * TOP_K)
    return pl.pallas_call(
        functools.partial(_dispatch_kernel, tm=tm, n_tiles=n_rows // ROW_TILE),
        grid=(T // tm,),
        in_specs=[pl.BlockSpec(memory_space=pltpu.SMEM),
                  pl.BlockSpec((1, 1, tm * TOP_K), lambda i: (i, 0, 0), memory_space=pltpu.SMEM),
                  pl.BlockSpec(memory_space=pl.ANY)],
        out_specs=pl.BlockSpec(memory_space=pl.ANY),
        out_shape=jax.ShapeDtypeStruct((n_rows, D_MODEL), F32),
        scratch_shapes=[pltpu.VMEM((N_DISPATCH_SLOTS, tm // SUBLANE, SUBLANE, D_MODEL), F32),
                        pltpu.VMEM((ROW_TILE, D_MODEL), F32),
                        pltpu.SemaphoreType.DMA((N_DISPATCH_SLOTS,)),
                        pltpu.SemaphoreType.DMA((N_DISPATCH_SLOTS,)),
                        pltpu.SemaphoreType.DMA(())],
        compiler_params=pltpu.CompilerParams(dimension_semantics=("arbitrary",)),
        name="dispatch",
    )(pend, inv3, h2.reshape(T // SUBLANE, SUBLANE, D_MODEL))


def _experts_kernel(ie_ref, row0_ref, nt_ref, xs_ref, wgu_hbm, wdn_hbm, bgu_ref, bd_ref,
                    y_ref, x_stage, x_bf, acc, wg_buf, wu_buf, wd_buf, sem_x, sem_y, sem_w):
    i = pl.program_id(0)
    n_items = nt_ref.shape[0]
    nt = nt_ref[i]
    row0 = row0_ref[i]
    expert = ie_ref[i]
    nt_prev = jnp.where(i > 0, nt_ref[jnp.maximum(i - 1, 0)], 0)
    nxt = jnp.minimum(i + 1, n_items - 1)
    nt_next = jnp.where(i + 1 < n_items, nt_ref[nxt], 0)
    row0_next = row0_ref[nxt]
    expert_next = ie_ref[nxt]

    def rows_of(t):
        return pl.ds(pl.multiple_of(t * ROW_TILE, ROW_TILE), ROW_TILE)

    def x_copy(first_row, t):
        return pltpu.make_async_copy(xs_ref.at[pl.ds(pl.multiple_of(first_row + t * ROW_TILE, ROW_TILE), ROW_TILE), :],
                                     x_stage.at[rows_of(t), :], sem_x)

    def y_copy(t, dst_tile):
        return pltpu.make_async_copy(acc.at[rows_of(t), :], y_ref.at[rows_of(dst_tile), :], sem_y.at[t])

    def w_copies(e, f):
        s = f % 2
        cols = pl.ds(f * FF_CHUNK, FF_CHUNK)
        return (pltpu.make_async_copy(wgu_hbm.at[e, :, cols], wg_buf.at[s], sem_w.at[s]),
                pltpu.make_async_copy(wgu_hbm.at[e, :, pl.ds(D_FF + f * FF_CHUNK, FF_CHUNK)], wu_buf.at[s], sem_w.at[s]),
                pltpu.make_async_copy(wdn_hbm.at[e, cols, :], wd_buf.at[s], sem_w.at[s]))

    def loop(lo, hi, fn):
        lax.fori_loop(lo, hi, lambda t, c: (fn(t), c)[1], 0)

    @pl.when(i == 0)
    def _():
        acc[pl.ds(0, ROW_TILE), :] = jnp.zeros((ROW_TILE, D_MODEL), F32)
        first_unused = row0_ref[row0_ref.shape[0] - 1] // ROW_TILE
        n_tiles = y_ref.shape[0] // ROW_TILE
        loop(first_unused, n_tiles, lambda t: y_copy(0, t).start())
        loop(first_unused, n_tiles, lambda t: y_copy(0, t).wait())

    def tile(t, width, f):
        first, last, s = f == 0, f == N_FF - 1, f % 2
        if first:
            for j in range(width):
                @pl.when(t + j < nt_prev)
                def _():
                    y_copy(t + j, 0).wait()
        rows = pl.ds(pl.multiple_of(t * ROW_TILE, ROW_TILE), width * ROW_TILE)
        x = x_bf[rows, :]
        b_gate = bgu_ref[0, :, f * FF_CHUNK:(f + 1) * FF_CHUNK]
        b_up = bgu_ref[0, :, D_FF + f * FF_CHUNK:D_FF + (f + 1) * FF_CHUNK]
        gate = jnp.dot(x, wg_buf[s].astype(BF16), preferred_element_type=F32) + b_gate
        up = jnp.dot(x, wu_buf[s].astype(BF16), preferred_element_type=F32) + b_up
        gate = jnp.minimum(gate, SWIGLU_LIMIT)
        up = jnp.clip(up, -SWIGLU_LIMIT, SWIGLU_LIMIT)
        act = (up + 1.0) * (gate * jax.nn.sigmoid(gate * SWIGLU_ALPHA))
        contrib = jnp.dot(act.astype(BF16), wd_buf[s].astype(BF16), preferred_element_type=F32)
        if first:
            acc[rows, :] = contrib + bd_ref[0]
        else:
            acc[rows, :] += contrib
        if last:
            for j in range(width):
                y_copy(t + j, row0 // ROW_TILE + t + j).start()

    def all_tiles(f):
        loop(0, nt // 2, lambda p: tile(2 * p, 2, f))

        @pl.when(nt % 2 == 1)
        def _():
            tile(nt - 1, 1, f)

    @pl.when(nt > 0)
    def _():
        @pl.when(i == 0)
        def _():
            for cp in w_copies(expert, 0):
                cp.start()
            loop(0, nt, lambda t: x_copy(row0, t).start())

        loop(0, nt, lambda t: x_copy(row0, t).wait())

        def convert(t):
            x_bf[rows_of(t), :] = x_stage[rows_of(t), :].astype(BF16)

        loop(0, nt, convert)

        for f in range(N_FF):
            if f + 1 < N_FF:
                for cp in w_copies(expert, f + 1):
                    cp.start()
            else:
                @pl.when(nt_next > 0)
                def _():
                    for cp in w_copies(expert_next, 0):
                        cp.start()

                loop(0, nt_next, lambda t: x_copy(row0_next, t).start())
            for cp in w_copies(expert, f):
                cp.wait()
            all_tiles(f)
            if f == 0:
                loop(nt, nt_prev, lambda t: y_copy(t, 0).wait())

        @pl.when(nt_next == 0)
        def _():
            loop(0, nt, lambda t: y_copy(t, 0).wait())


def _experts(item_e, item_row0, item_nt, xs, w_gu, b_gu, w_dn, b_dn):
    n_items = item_e.shape[0]
    n_rows = xs.shape[0]
    E = N_EXPERTS
    item_rows = TILES_PER_ITEM * ROW_TILE
    grid_spec = pltpu.PrefetchScalarGridSpec(
        num_scalar_prefetch=3,
        grid=(n_items,),
        in_specs=[pl.BlockSpec(memory_space=pl.ANY),
                  pl.BlockSpec(memory_space=pl.ANY),
                  pl.BlockSpec(memory_space=pl.ANY),
                  pl.BlockSpec((1, 1, 2 * D_FF), lambda i, ie, r0, nt: (ie[i], 0, 0)),
                  pl.BlockSpec((1, 1, D_MODEL), lambda i, ie, r0, nt: (ie[i], 0, 0))],
        out_specs=pl.BlockSpec(memory_space=pl.ANY),
        scratch_shapes=[pltpu.VMEM((item_rows, D_MODEL), F32),
                        pltpu.VMEM((item_rows, D_MODEL), BF16),
                        pltpu.VMEM((item_rows, D_MODEL), F32),
                        pltpu.VMEM((2, D_MODEL, FF_CHUNK), F32),
                        pltpu.VMEM((2, D_MODEL, FF_CHUNK), F32),
                        pltpu.VMEM((2, FF_CHUNK, D_MODEL), F32),
                        pltpu.SemaphoreType.DMA(()),
                        pltpu.SemaphoreType.DMA((TILES_PER_ITEM,)),
                        pltpu.SemaphoreType.DMA((2,))],
    )
    return pl.pallas_call(
        _experts_kernel,
        grid_spec=grid_spec,
        out_shape=jax.ShapeDtypeStruct((n_rows, D_MODEL), F32),
        compiler_params=pltpu.CompilerParams(dimension_semantics=("arbitrary",),
                                             vmem_limit_bytes=V7X_VMEM_LIMIT_BYTES),
        name="experts",
    )(item_e, item_row0, item_nt, xs, w_gu, w_dn, b_gu.reshape(E, 1, 2 * D_FF), b_dn.reshape(E, 1, D_MODEL))


def _combine_kernel(inv_ref, inv_next_ref, x2_ref, gate_ref, fn_ref, y_ref, o_ref, rows, sem, *, tm, last_layer):
    i = pl.program_id(0)
    slot = i % 2

    groups = tm // SUBLANE

    def row_copy(idx_ref, g, j, k, s):
        return pltpu.make_async_copy(y_ref.at[pl.ds(idx_ref[0, 0, (g * SUBLANE + j) * TOP_K + k], 1), :],
                                     rows.at[s, k, g, pl.ds(j, 1), :], sem.at[s])

    def gather(idx_ref, s):
        def issue(g, c):
            for j in range(SUBLANE):
                for k in range(TOP_K):
                    row_copy(idx_ref, g, j, k, s).start(priority=k % 2)
            return c

        lax.fori_loop(0, groups, issue, 0)

    @pl.when(i == 0)
    def _():
        gather(inv_ref, 0)

    @pl.when(i + 1 < pl.num_programs(0))
    def _():
        gather(inv_next_ref, 1 - slot)

    def drain(g, c):
        for j in range(SUBLANE):
            for k in range(TOP_K):
                row_copy(inv_ref, g, j, k, slot).wait()
        return c

    lax.fori_loop(0, groups, drain, 0)
    gate = gate_ref[...]
    out = x2_ref[...]
    for k in range(TOP_K):
        out = out + gate[:, k:k + 1] * rows[slot, k].reshape(tm, D_MODEL)
    o_ref[...] = _rms(out, fn_ref[...]) if last_layer else out


def _combine(inv, x2, gates, final_norm, y, last_layer, tm=128):
    T = x2.shape[0]
    inv3 = inv.reshape(T // tm, 1, tm * TOP_K)
    return pl.pallas_call(
        functools.partial(_combine_kernel, tm=tm, last_layer=last_layer),
        grid=(T // tm,),
        in_specs=[pl.BlockSpec((1, 1, tm * TOP_K), lambda i: (i, 0, 0), memory_space=pltpu.SMEM),
                  pl.BlockSpec((1, 1, tm * TOP_K), lambda i: (jnp.minimum(i + 1, T // tm - 1), 0, 0),
                               memory_space=pltpu.SMEM),
                  pl.BlockSpec((tm, D_MODEL), lambda i: (i, 0)),
                  pl.BlockSpec((tm, LANE), lambda i: (i, 0)),
                  pl.BlockSpec((1, D_MODEL), lambda i: (0, 0)),
                  pl.BlockSpec(memory_space=pl.ANY)],
        out_specs=pl.BlockSpec((tm, D_MODEL), lambda i: (i, 0)),
        out_shape=jax.ShapeDtypeStruct((T, D_MODEL), F32),
        scratch_shapes=[pltpu.VMEM((2, TOP_K, tm // SUBLANE, SUBLANE, D_MODEL), F32),
                        pltpu.SemaphoreType.DMA((2,))],
        compiler_params=pltpu.CompilerParams(dimension_semantics=("arbitrary",)),
        name="combine",
    )(inv3, inv3, x2, gates, final_norm, y)


def _reorder_w_in_kernel(w_ref, o_ref):
    w = w_ref[...]
    tk = w.shape[0]
    a = 2 * GLA_W + 2 * GLA_VW
    lat = a + GLA_RANK
    kr = lat + Q_LORA + KV_LORA
    half = QK_ROPE // 2
    zeros = lambda n: jnp.zeros((tk, n), F32)
    o_ref[:, 0:a] = w[:, 0:a].astype(BF16)
    o_ref[:, a:a + Q_LORA + KV_LORA] = w[:, lat:kr].astype(BF16)
    o_ref[:, 32 * LANE:33 * LANE] = jnp.concatenate([w[:, kr:kr + QK_ROPE], zeros(LANE - QK_ROPE)], axis=1).astype(BF16)
    o_ref[:, 33 * LANE:34 * LANE] = jnp.concatenate(
        [w[:, kr + half:kr + QK_ROPE], w[:, kr:kr + half], zeros(LANE - QK_ROPE)], axis=1).astype(BF16)
    o_ref[:, 34 * LANE:35 * LANE] = jnp.concatenate([w[:, a:lat], zeros(LANE - GLA_RANK)], axis=1).astype(BF16)


def _reorder_w_in(w_in, tk=256):
    in_cols = w_in.shape[1]
    return pl.pallas_call(
        _reorder_w_in_kernel,
        grid=(D_MODEL // tk,),
        in_specs=[pl.BlockSpec((tk, in_cols), lambda i: (i, 0))],
        out_specs=pl.BlockSpec((tk, PROJ_W_COLS), lambda i: (i, 0)),
        out_shape=jax.ShapeDtypeStruct((D_MODEL, PROJ_W_COLS), BF16),
        compiler_params=pltpu.CompilerParams(dimension_semantics=("parallel",)),
        name="reorder_w_in",
    )(w_in)


def _reorder_w_uq(w_uq):
    w = w_uq.reshape(Q_LORA, MLA_HEADS, QK_NOPE + QK_ROPE)
    nope = w[:, :, :QK_NOPE].reshape(Q_LORA, MLA_W)
    pe = w[:, :, QK_NOPE:]
    half = QK_ROPE // 2
    z = jnp.zeros((Q_LORA, MLA_HEADS, LANE - QK_ROPE), w_uq.dtype)
    pe_pad = jnp.concatenate([pe, z], axis=2).reshape(Q_LORA, MLA_W)
    pe_sw = jnp.concatenate([pe[:, :, half:], pe[:, :, :half], z], axis=2).reshape(Q_LORA, MLA_W)
    return jnp.concatenate([nope, pe_pad, pe_sw], axis=1).astype(BF16)


def _reorder_w_ukv(w_ukv):
    w = w_ukv.reshape(KV_LORA, MLA_HEADS, QK_NOPE + V_DIM)
    return jnp.concatenate([w[:, :, :QK_NOPE].reshape(KV_LORA, MLA_W),
                            w[:, :, QK_NOPE:].reshape(KV_LORA, MLA_W)], axis=1).astype(BF16)


def _rope_tables(positions):
    half = QK_ROPE // 2
    inv_freq = 1.0 / (ROPE_THETA ** (jnp.arange(0, QK_ROPE, 2, dtype=F32) / QK_ROPE))
    ang = positions.astype(F32).reshape(-1, 1) * inv_freq
    cos, sin = jnp.cos(ang), jnp.sin(ang)
    z = jnp.zeros((ang.shape[0], LANE - QK_ROPE), F32)
    del half
    return jnp.concatenate([cos, cos, z], axis=1), jnp.concatenate([-sin, sin, z], axis=1)


def _routing_plan(counts, topi, rank, n_items):
    padded = (counts + ROW_TILE - 1) // ROW_TILE * ROW_TILE
    pend = jnp.cumsum(padded)
    pstart = pend - padded
    inv = pstart[topi] + rank
    nt_e = padded // ROW_TILE
    ns_e = (nt_e + TILES_PER_ITEM - 1) // TILES_PER_ITEM
    cs = jnp.cumsum(ns_e)
    j = jnp.arange(n_items, dtype=I32)
    valid = j < cs[-1]
    jj = jnp.where(valid, j, cs[-1] - 1)
    e_j = jnp.sum((jj[:, None] >= cs[None, :]).astype(I32), axis=1)
    local = jj - (cs[e_j] - ns_e[e_j])
    row0 = pstart[e_j] + local * (TILES_PER_ITEM * ROW_TILE)
    nt_j = jnp.where(valid, jnp.clip(nt_e[e_j] - local * TILES_PER_ITEM, 0, TILES_PER_ITEM), 0)
    row0_and_used = jnp.concatenate([row0, pend[-1:]])
    return pend.astype(I32), inv.astype(I32), e_j.astype(I32), row0_and_used.astype(I32), nt_j.astype(I32)


def kernel(x, positions, attn_norm, w_in, gla_gate_up, gla_gate_bias, gla_out_norm, mla_q_norm, w_uq, mla_kv_norm, w_ukv, mla_out_norm, w_out, ffn_norm, router_w, router_b, w_gate_up, b_gate_up, w_down, b_down, final_norm):
    B, S, D = x.shape
    T = B * S
    A = T * TOP_K
    n_tiles = (A + N_EXPERTS * (ROW_TILE - 1)) // ROW_TILE
    n_rows = n_tiles * ROW_TILE
    n_items = (n_tiles + N_EXPERTS * (TILES_PER_ITEM - 1)) // TILES_PER_ITEM
    row2 = lambda v: v.reshape(1, -1)

    xf = x.reshape(T, D)
    cos_t, sin_t = _rope_tables(positions)
    for l in range(w_in.shape[0]):
        gup_pad = jnp.zeros((LANE, GLA_W), F32).at[:GLA_RANK].set(gla_gate_up[l]).astype(BF16)
        proj, loga = _in_proj(xf, row2(attn_norm[l]), _reorder_w_in(w_in[l]), gup_pad, row2(gla_gate_bias[l]))
        o_gla = _gla(proj, loga, row2(gla_out_norm[l]), B, S)
        q, k, v = _mla_proj(proj, cos_t, sin_t, row2(mla_q_norm[l]), row2(mla_kv_norm[l]),
                            _reorder_w_uq(w_uq[l]), _reorder_w_ukv(w_ukv[l]))
        o_mla = _mla_attn(q, k, v, B, S)
        rw_pad = jnp.zeros((D, LANE), F32).at[:, :N_EXPERTS].set(router_w[l])
        rw_hi = rw_pad.astype(BF16)
        rw_lo = (rw_pad - rw_hi.astype(F32)).astype(BF16)
        rb_pad = jnp.zeros((1, LANE), F32).at[0, :N_EXPERTS].set(router_b[l])
        x2, h2, topi, gates, rank, counts = _out_router(
            xf, o_gla, o_mla, w_out[l].astype(BF16), row2(mla_out_norm[l]), row2(ffn_norm[l]), rw_hi, rw_lo, rb_pad)
        pend, inv, item_e, item_row0, item_nt = _routing_plan(
            counts[0, :N_EXPERTS], topi[:, :TOP_K], rank[:, :TOP_K], n_items)
        xs = _dispatch(pend, inv, h2, n_rows)
        y = _experts(item_e, item_row0, item_nt, xs, w_gate_up[l], b_gate_up[l], w_down[l], b_down[l])
        xf = _combine(inv, x2, gates, row2(final_norm), y, last_layer=(l == w_in.shape[0] - 1))
    return xf.reshape(B, S, D)
```

```python
import functools

import jax
import jax.numpy as jnp
from jax import lax
from jax.experimental import pallas as pl
from jax.experimental.pallas import tpu as pltpu

F32 = jnp.float32
BF16 = jnp.bfloat16
I32 = jnp.int32
HIGHEST = lax.Precision.HIGHEST

LANE = 128
SUBLANE = 8
V7X_VMEM_LIMIT_BYTES = 60000 * 1024

D_MODEL = 2048
GLA_HEADS, GLA_DK, GLA_DV, GLA_RANK, GLA_TAU, GLA_CHUNK = 4, 128, 256, 16, 16.0, 64
MLA_HEADS, Q_LORA, KV_LORA, QK_NOPE, QK_ROPE, V_DIM = 8, 512, 512, 128, 64, 128
ROPE_THETA = 10000.0
N_EXPERTS, TOP_K, D_FF = 32, 4, 2048
SWIGLU_LIMIT, SWIGLU_ALPHA = 7.0, 1.702
EPS = 1e-6
LOG2_E = 1.4426950408889634
GLA_W = GLA_HEADS * GLA_DK
GLA_VW = GLA_HEADS * GLA_DV
MLA_W = MLA_HEADS * V_DIM
QK_PAD = 2 * LANE

PROJ_OUT_COLS = 34 * LANE
PROJ_W_COLS = 35 * LANE

ROW_TILE = 256
TILES_PER_ITEM = 5
FF_CHUNK = 512
N_FF = D_FF // FF_CHUNK


def _rms(x, g):
    return x * lax.rsqrt(jnp.mean(x * x, axis=-1, keepdims=True) + EPS) * g


def _inproj_kernel(x_ref, g_ref, w_ref, gup_ref, gb_ref, proj_ref, loga_ref):
    h = _rms(x_ref[...], g_ref[...]).astype(BF16)
    p = jnp.dot(h, w_ref[...], preferred_element_type=F32)
    proj_ref[...] = p[:, :PROJ_OUT_COLS].astype(BF16)
    z = jnp.dot(p[:, PROJ_OUT_COLS:].astype(BF16), gup_ref[...], preferred_element_type=F32) + gb_ref[...]
    log_sig = jnp.minimum(z, 0.0) - jnp.log(1.0 + jnp.exp(-jnp.abs(z)))
    loga_ref[...] = log_sig * (1.0 / GLA_TAU)


def _in_proj(xf, g, w_r, gup_pad, gbias, tm=256):
    T = xf.shape[0]
    return pl.pallas_call(
        _inproj_kernel,
        grid=(T // tm,),
        in_specs=[pl.BlockSpec((tm, D_MODEL), lambda i: (i, 0)),
                  pl.BlockSpec((1, D_MODEL), lambda i: (0, 0)),
                  pl.BlockSpec((D_MODEL, PROJ_W_COLS), lambda i: (0, 0)),
                  pl.BlockSpec((LANE, GLA_W), lambda i: (0, 0)),
                  pl.BlockSpec((1, GLA_W), lambda i: (0, 0))],
        out_specs=[pl.BlockSpec((tm, PROJ_OUT_COLS), lambda i: (i, 0)),
                   pl.BlockSpec((tm, GLA_W), lambda i: (i, 0))],
        out_shape=[jax.ShapeDtypeStruct((T, PROJ_OUT_COLS), BF16),
                   jax.ShapeDtypeStruct((T, GLA_W), F32)],
        compiler_params=pltpu.CompilerParams(dimension_semantics=("parallel",),
                                             vmem_limit_bytes=V7X_VMEM_LIMIT_BYTES),
        name="in_proj",
    )(xf, g, w_r, gup_pad, gbias)


def _gla_kernel(q_ref, k_ref, v_ref, r_ref, la_ref, gn_ref, o_ref, st_ref, *, n_chunks):
    @pl.when(pl.program_id(1) == 0)
    def _():
        st_ref[...] = jnp.zeros_like(st_ref)

    C = GLA_CHUNK
    row = lax.broadcasted_iota(I32, (C, C), 0)
    col = lax.broadcasted_iota(I32, (C, C), 1)
    causal = col <= row
    tri = causal.astype(F32)
    for c in range(n_chunks):
        sl = pl.ds(c * C, C)
        b = jnp.dot(tri, la_ref[sl, :], precision=HIGHEST, preferred_element_type=F32)
        b_last = b[C - 1:C, :]
        q_dec = (q_ref[sl, :].astype(F32) * (GLA_DK ** -0.5) * jnp.exp(b)).astype(BF16)
        kf = k_ref[sl, :].astype(F32)
        k_intra = (kf * jnp.exp(-b)).astype(BF16)
        k_state = (kf * jnp.exp(b_last - b)).astype(BF16)
        decay = jnp.exp(b_last)
        for h in range(GLA_HEADS):
            hs = slice(h * GLA_DK, (h + 1) * GLA_DK)
            vs = slice(h * GLA_DV, (h + 1) * GLA_DV)
            v = v_ref[sl, vs]
            attn = lax.dot_general(q_dec[:, hs], k_intra[:, hs], (((1,), (1,)), ((), ())),
                                   preferred_element_type=F32)
            attn = jnp.where(causal, attn, 0.0).astype(BF16)
            st = st_ref[h]
            o = jnp.dot(attn, v, preferred_element_type=F32)
            o = o + lax.dot_general(q_dec[:, hs], st.astype(BF16), (((1,), (1,)), ((), ())),
                                    preferred_element_type=F32)
            kv_t = lax.dot_general(v, k_state[:, hs], (((0,), (0,)), ((), ())),
                                   preferred_element_type=F32)
            st_ref[h] = st * decay[:, hs] + kv_t
            r = r_ref[sl, vs].astype(F32)
            o_ref[sl, vs] = (_rms(o, gn_ref[...]) * (r * jax.nn.sigmoid(r))).astype(BF16)


def _gla(proj, loga, gn, batch, seq, ct=256):
    nct = seq // ct
    T = batch * seq
    row_map = lambda blk: (lambda b, c: (b * nct + c, blk))
    return pl.pallas_call(
        functools.partial(_gla_kernel, n_chunks=ct // GLA_CHUNK),
        grid=(batch, nct),
        in_specs=[pl.BlockSpec((ct, GLA_W), row_map(0)),
                  pl.BlockSpec((ct, GLA_W), row_map(1)),
                  pl.BlockSpec((ct, GLA_VW), row_map(1)),
                  pl.BlockSpec((ct, GLA_VW), row_map(2)),
                  pl.BlockSpec((ct, GLA_W), row_map(0)),
                  pl.BlockSpec((1, GLA_DV), lambda b, c: (0, 0))],
        out_specs=pl.BlockSpec((ct, GLA_VW), row_map(0)),
        out_shape=jax.ShapeDtypeStruct((T, GLA_VW), BF16),
        scratch_shapes=[pltpu.VMEM((GLA_HEADS, GLA_DV, GLA_DK), F32)],
        compiler_params=pltpu.CompilerParams(dimension_semantics=("parallel", "arbitrary")),
        name="gla",
    )(proj, proj, proj, proj, loga, gn)


def _mla_proj_kernel(ql_ref, kvl_ref, kr_ref, krs_ref, cos_ref, sin_ref, qn_ref, kvn_ref,
                     wq_ref, wkv_ref, q_ref, k_ref, vt_ref):
    cos = cos_ref[...]
    sin = sin_ref[...]
    scale = (QK_NOPE + QK_ROPE) ** -0.5 * LOG2_E
    qlat = _rms(ql_ref[...].astype(F32), qn_ref[...]).astype(BF16)
    qall = jnp.dot(qlat, wq_ref[...], preferred_element_type=F32)
    kvlat = _rms(kvl_ref[...].astype(F32), kvn_ref[...]).astype(BF16)
    kv = jnp.dot(kvlat, wkv_ref[...], preferred_element_type=F32)
    k_pe = (kr_ref[...].astype(F32) * cos + krs_ref[...].astype(F32) * sin).astype(BF16)
    for h in range(MLA_HEADS):
        hl = slice(h * LANE, (h + 1) * LANE)
        pe = slice(MLA_W + h * LANE, MLA_W + (h + 1) * LANE)
        ps = slice(2 * MLA_W + h * LANE, 2 * MLA_W + (h + 1) * LANE)
        q_ref[h, :, 0:LANE] = (qall[:, hl] * scale).astype(BF16)
        q_ref[h, :, LANE:QK_PAD] = ((qall[:, pe] * cos + qall[:, ps] * sin) * scale).astype(BF16)
        k_ref[h, :, 0:LANE] = kv[:, hl].astype(BF16)
        k_ref[h, :, LANE:QK_PAD] = k_pe
        vt_ref[h] = kv[:, MLA_W + h * LANE:MLA_W + (h + 1) * LANE].T.astype(BF16)


def _mla_proj(proj, cos_t, sin_t, qn, kvn, wq, wkv, tm=512):
    T = proj.shape[0]
    H = MLA_HEADS
    return pl.pallas_call(
        _mla_proj_kernel,
        grid=(T // tm,),
        in_specs=[pl.BlockSpec((tm, Q_LORA), lambda i: (i, 6)),
                  pl.BlockSpec((tm, KV_LORA), lambda i: (i, 7)),
                  pl.BlockSpec((tm, LANE), lambda i: (i, 32)),
                  pl.BlockSpec((tm, LANE), lambda i: (i, 33)),
                  pl.BlockSpec((tm, LANE), lambda i: (i, 0)),
                  pl.BlockSpec((tm, LANE), lambda i: (i, 0)),
                  pl.BlockSpec((1, Q_LORA), lambda i: (0, 0)),
                  pl.BlockSpec((1, KV_LORA), lambda i: (0, 0)),
                  pl.BlockSpec((Q_LORA, 3 * MLA_W), lambda i: (0, 0)),
                  pl.BlockSpec((KV_LORA, 2 * MLA_W), lambda i: (0, 0))],
        out_specs=[pl.BlockSpec((H, tm, QK_PAD), lambda i: (0, i, 0)),
                   pl.BlockSpec((H, tm, QK_PAD), lambda i: (0, i, 0)),
                   pl.BlockSpec((H, V_DIM, tm), lambda i: (0, 0, i))],
        out_shape=[jax.ShapeDtypeStruct((H, T, QK_PAD), BF16),
                   jax.ShapeDtypeStruct((H, T, QK_PAD), BF16),
                   jax.ShapeDtypeStruct((H, V_DIM, T), BF16)],
        compiler_params=pltpu.CompilerParams(dimension_semantics=("parallel",)),
        name="mla_proj",
    )(proj, proj, proj, proj, cos_t, sin_t, qn, kvn, wq, wkv)


def _attn_kernel(q_ref, k_ref, vt_ref, o_ref, s0_ref, s1_ref, p0_ref, p1_ref, a0_ref, a1_ref,
                 m_ref, l_ref, acc_ref, *, tq):
    qi = pl.program_id(2)
    q = q_ref[0]
    s_buf, p_buf, a_buf = (s0_ref, s1_ref), (p0_ref, p1_ref), (a0_ref, a1_ref)
    m_ref[...] = jnp.full_like(m_ref, -jnp.inf)
    l_ref[...] = jnp.zeros_like(l_ref)
    acc_ref[...] = jnp.zeros_like(acc_ref)

    def chunk(c):
        return pl.ds(pl.multiple_of(c * tq, tq), tq)

    def score(c, slot):
        s_buf[slot][...] = lax.dot_general(k_ref[0, chunk(c), :], q, (((1,), (1,)), ((), ())),
                                           preferred_element_type=F32)

    def softmax(slot, diagonal):
        s_t = s_buf[slot][...]
        if diagonal:
            key = lax.broadcasted_iota(I32, (tq, tq), 0)
            qry = lax.broadcasted_iota(I32, (tq, tq), 1)
            s_t = jnp.where(key <= qry, s_t, -jnp.inf)
        m_old = m_ref[...]
        m_new = jnp.maximum(m_old, jnp.max(s_t, axis=0, keepdims=True))
        p_t = jnp.exp2(s_t - m_new)
        alpha = jnp.exp2(m_old - m_new)
        l_ref[...] = alpha * l_ref[...] + jnp.sum(p_t, axis=0, keepdims=True)
        m_ref[...] = m_new
        a_buf[slot][...] = alpha
        p_buf[slot][...] = p_t.astype(BF16)

    def value(c, slot):
        acc_ref[...] = a_buf[slot][...] * acc_ref[...] + jnp.dot(vt_ref[0, :, chunk(c)], p_buf[slot][...],
                                                                preferred_element_type=F32)

    def stages(c, slot):
        value(c, slot)
        softmax(1 - slot, False)
        score(c + 2, slot)

    score(0, 0)

    @pl.when(qi == 0)
    def _():
        softmax(0, True)
        value(0, 0)

    @pl.when(qi > 0)
    def _():
        score(1, 1)
        softmax(0, False)
        n_full = qi - 1

        def pair(j, carry):
            stages(2 * j, 0)
            stages(2 * j + 1, 1)
            return carry

        lax.fori_loop(0, n_full // 2, pair, 0)

        @pl.when(n_full % 2 == 1)
        def _():
            stages(qi - 2, 0)

        for parity in (0, 1):
            @pl.when(qi % 2 == parity)
            def _():
                value(qi - 1, 1 - parity)
                softmax(parity, True)
                value(qi, parity)

    o_ref[...] = (acc_ref[...] / l_ref[...]).T


def _mla_attn(q, k, vt, batch, seq, tq=512):
    H, T, _ = q.shape
    nq = seq // tq
    return pl.pallas_call(
        functools.partial(_attn_kernel, tq=tq),
        grid=(batch, H, nq),
        in_specs=[pl.BlockSpec((1, tq, QK_PAD), lambda b, h, i: (h, b * nq + i, 0)),
                  pl.BlockSpec((1, seq, QK_PAD), lambda b, h, i: (h, b, 0)),
                  pl.BlockSpec((1, V_DIM, seq), lambda b, h, i: (h, 0, b))],
        out_specs=pl.BlockSpec((tq, V_DIM), lambda b, h, i: (b * nq + i, h)),
        out_shape=jax.ShapeDtypeStruct((T, MLA_W), F32),
        scratch_shapes=[pltpu.VMEM((tq, tq), F32), pltpu.VMEM((tq, tq), F32),
                        pltpu.VMEM((tq, tq), BF16), pltpu.VMEM((tq, tq), BF16),
                        pltpu.VMEM((1, tq), F32), pltpu.VMEM((1, tq), F32),
                        pltpu.VMEM((1, tq), F32), pltpu.VMEM((1, tq), F32), pltpu.VMEM((V_DIM, tq), F32)],
        compiler_params=pltpu.CompilerParams(dimension_semantics=("parallel", "parallel", "arbitrary")),
        name="mla_attn",
    )(q, k, vt)


def _out_router_kernel(x_ref, og_ref, om_ref, wo_ref, mn_ref, fn_ref, rwh_ref, rwl_ref, rb_ref,
                       x2_ref, h2_ref, topi_ref, gate_ref, rank_ref, cnt_ref, carry_ref, *, tm):
    @pl.when(pl.program_id(0) == 0)
    def _():
        carry_ref[...] = jnp.zeros_like(carry_ref)

    om = _rms(om_ref[...], mn_ref[...]).astype(BF16)
    mix = jnp.dot(og_ref[...], wo_ref[0:GLA_VW, :], preferred_element_type=F32)
    mix = mix + jnp.dot(om, wo_ref[GLA_VW:GLA_VW + MLA_W, :], preferred_element_type=F32)
    x2 = x_ref[...] + mix
    x2_ref[...] = x2
    h2 = _rms(x2, fn_ref[...])
    h2_ref[...] = h2
    h2_hi = h2.astype(BF16)
    h2_lo = (h2 - h2_hi.astype(F32)).astype(BF16)
    logits = (jnp.dot(h2_hi, rwh_ref[...], preferred_element_type=F32)
              + jnp.dot(h2_lo, rwh_ref[...], preferred_element_type=F32)
              + jnp.dot(h2_hi, rwl_ref[...], preferred_element_type=F32)) + rb_ref[...]
    lane = lax.broadcasted_iota(I32, (tm, LANE), 1).astype(F32)
    cur = jnp.where(lane < N_EXPERTS, logits, -jnp.inf)
    vals, idxs = [], []
    for _ in range(TOP_K):
        m = jnp.max(cur, axis=-1, keepdims=True)
        idx = jnp.min(jnp.where(cur == m, lane, float(N_EXPERTS - 1)), axis=-1, keepdims=True)
        vals.append(m)
        idxs.append(idx)
        cur = jnp.where(lane == idx, -jnp.inf, cur)
    es = [jnp.exp(v - vals[0]) for v in vals]
    den = es[0] + es[1] + es[2] + es[3]
    gates = jnp.zeros((tm, LANE), F32)
    topi = jnp.zeros((tm, LANE), F32)
    multi = jnp.zeros((tm, LANE), F32)
    for k in range(TOP_K):
        gates = jnp.where(lane == k, es[k] / den, gates)
        topi = jnp.where(lane == k, idxs[k], topi)
        multi = multi + (lane == idxs[k]).astype(F32)
    gate_ref[...] = gates
    topi_ref[...] = topi.astype(I32)
    r = lax.broadcasted_iota(I32, (tm, tm), 0)
    c = lax.broadcasted_iota(I32, (tm, tm), 1)
    strict = (c < r).astype(BF16)
    pref = jnp.dot(strict, multi.astype(BF16), preferred_element_type=F32) + carry_ref[...]
    rank = jnp.zeros((tm, LANE), F32)
    for k in range(TOP_K):
        rk = jnp.sum(jnp.where(lane == idxs[k], pref, 0.0), axis=-1, keepdims=True)
        rank = jnp.where(lane == k, rk, rank)
    rank_ref[...] = rank.astype(I32)
    carry_ref[...] = carry_ref[...] + jnp.sum(multi, axis=0, keepdims=True)
    cnt_ref[...] = carry_ref[...].astype(I32)


def _out_router(xf, o_gla, o_mla, w_out, mn, fn, rw_hi, rw_lo, rb_pad, tm=256):
    T = xf.shape[0]
    row = lambda i: (i, 0)
    fixed = lambda i: (0, 0)
    return pl.pallas_call(
        functools.partial(_out_router_kernel, tm=tm),
        grid=(T // tm,),
        in_specs=[pl.BlockSpec((tm, D_MODEL), row),
                  pl.BlockSpec((tm, GLA_VW), row),
                  pl.BlockSpec((tm, MLA_W), row),
                  pl.BlockSpec((GLA_VW + MLA_W, D_MODEL), fixed),
                  pl.BlockSpec((1, MLA_W), fixed),
                  pl.BlockSpec((1, D_MODEL), fixed),
                  pl.BlockSpec((D_MODEL, LANE), fixed),
                  pl.BlockSpec((D_MODEL, LANE), fixed),
                  pl.BlockSpec((1, LANE), fixed)],
        out_specs=[pl.BlockSpec((tm, D_MODEL), row),
                   pl.BlockSpec((tm, D_MODEL), row),
                   pl.BlockSpec((tm, LANE), row),
                   pl.BlockSpec((tm, LANE), row),
                   pl.BlockSpec((tm, LANE), row),
                   pl.BlockSpec((1, LANE), fixed)],
        out_shape=[jax.ShapeDtypeStruct((T, D_MODEL), F32),
                   jax.ShapeDtypeStruct((T, D_MODEL), F32),
                   jax.ShapeDtypeStruct((T, LANE), I32),
                   jax.ShapeDtypeStruct((T, LANE), F32),
                   jax.ShapeDtypeStruct((T, LANE), I32),
                   jax.ShapeDtypeStruct((1, LANE), I32)],
        scratch_shapes=[pltpu.VMEM((1, LANE), F32)],
        compiler_params=pltpu.CompilerParams(dimension_semantics=("arbitrary",),
                                             vmem_limit_bytes=V7X_VMEM_LIMIT_BYTES),
        name="out_router",
    )(xf, o_gla, o_mla, w_out, mn, fn, rw_hi, rw_lo, rb_pad)


N_DISPATCH_SLOTS = 3


def _dispatch_kernel(pend_ref, inv_ref, h_hbm, xs_ref, hbuf, zero_ref, sem_in, sem_out, sem_fill, *, tm, n_tiles):
    i = pl.program_id(0)
    n_steps = pl.num_programs(0)

    groups = tm // SUBLANE

    def load(tile, s):
        return pltpu.make_async_copy(h_hbm.at[pl.ds(tile * groups, groups)], hbuf.at[s], sem_in.at[s])

    def row_copy(g, j, dst_row, s):
        return pltpu.make_async_copy(hbuf.at[s, g, pl.ds(j, 1), :], xs_ref.at[pl.ds(dst_row, 1), :], sem_out.at[s])

    def drain(s):
        def body(g, c):
            for j in range(SUBLANE):
                for k in range(TOP_K):
                    row_copy(g, j, 0, s).wait()
            return c

        lax.fori_loop(0, groups, body, 0)

    @pl.when(i == 0)
    def _():
        load(0, 0).start()
        zero_ref[...] = jnp.zeros_like(zero_ref)

        def fill_copy(t):
            return pltpu.make_async_copy(zero_ref, xs_ref.at[pl.ds(pl.multiple_of(t * ROW_TILE, ROW_TILE), ROW_TILE), :],
                                         sem_fill)

        def fill(e, wait):
            prev = jnp.where(e == 0, 0, pend_ref[jnp.maximum(e - 1, 0)])

            @pl.when(pend_ref[e] > prev)
            def _():
                cp = fill_copy(pend_ref[e] // ROW_TILE - 1)
                if wait:
                    cp.wait()
                else:
                    cp.start()

        def tail(t, wait):
            if wait:
                fill_copy(t).wait()
            else:
                fill_copy(t).start()

        first_unused = pend_ref[N_EXPERTS - 1] // ROW_TILE
        lax.fori_loop(0, N_EXPERTS, lambda e, c: (fill(e, False), c)[1], 0)
        lax.fori_loop(first_unused, n_tiles, lambda t, c: (tail(t, False), c)[1], 0)
        lax.fori_loop(0, N_EXPERTS, lambda e, c: (fill(e, True), c)[1], 0)
        lax.fori_loop(first_unused, n_tiles, lambda t, c: (tail(t, True), c)[1], 0)

    slot = i % N_DISPATCH_SLOTS
    next_slot = (i + 1) % N_DISPATCH_SLOTS

    @pl.when(i >= 2)
    def _():
        drain(next_slot)

    @pl.when(i + 1 < n_steps)
    def _():
        load(i + 1, next_slot).start()

    load(i, slot).wait()

    def issue(g, c):
        for j in range(SUBLANE):
            for k in range(TOP_K):
                dst_row = inv_ref[0, 0, (g * SUBLANE + j) * TOP_K + k]
                row_copy(g, j, dst_row, slot).start(priority=k % 2)
        return c

    lax.fori_loop(0, groups, issue, 0)

    @pl.when(i == n_steps - 1)
    def _():
        @pl.when(i >= 1)
        def _():
            drain((i + 2) % N_DISPATCH_SLOTS)

        drain(slot)


def _dispatch(pend, inv, h2, n_rows, tm=256):
    T = h2.shape[0]
    inv3 = inv.reshape(T // tm, 1, tm * TOP_K)
    return pl.pallas_call(
        functools.partial(_dispatch_kernel, tm=tm, n_tiles=n_rows // ROW_TILE),
        grid=(T // tm,),
        in_specs=[pl.BlockSpec(memory_space=pltpu.SMEM),
                  pl.BlockSpec((1, 1, tm * TOP_K), lambda i: (i, 0, 0), memory_space=pltpu.SMEM),
                  pl.BlockSpec(memory_space=pl.ANY)],
        out_specs=pl.BlockSpec(memory_space=pl.ANY),
        out_shape=jax.ShapeDtypeStruct((n_rows, D_MODEL), F32),
        scratch_shapes=[pltpu.VMEM((N_DISPATCH_SLOTS, tm // SUBLANE, SUBLANE, D_MODEL), F32),
                        pltpu.VMEM((ROW_TILE, D_MODEL), F32),
                        pltpu.SemaphoreType.DMA((N_DISPATCH_SLOTS,)),
                        pltpu.SemaphoreType.DMA((N_DISPATCH_SLOTS,)),
                        pltpu.SemaphoreType.DMA(())],
        compiler_params=pltpu.CompilerParams(dimension_semantics=("arbitrary",)),
        name="dispatch",
    )(pend, inv3, h2.reshape(T // SUBLANE, SUBLANE, D_MODEL))


def _experts_kernel(ie_ref, row0_ref, nt_ref, xs_ref, wgu_hbm, wdn_hbm, bgu_ref, bd_ref,
                    y_ref, x_stage, x_bf, acc, wg_buf, wu_buf, wd_buf, sem_x, sem_y, sem_w):
    i = pl.program_id(0)
    n_items = nt_ref.shape[0]
    nt = nt_ref[i]
    row0 = row0_ref[i]
    expert = ie_ref[i]
    nt_prev = jnp.where(i > 0, nt_ref[jnp.maximum(i - 1, 0)], 0)
    nxt = jnp.minimum(i + 1, n_items - 1)
    nt_next = jnp.where(i + 1 < n_items, nt_ref[nxt], 0)
    row0_next = row0_ref[nxt]
    expert_next = ie_ref[nxt]

    def rows_of(t):
        return pl.ds(pl.multiple_of(t * ROW_TILE, ROW_TILE), ROW_TILE)

    def x_copy(first_row, t):
        return pltpu.make_async_copy(xs_ref.at[pl.ds(pl.multiple_of(first_row + t * ROW_TILE, ROW_TILE), ROW_TILE), :],
                                     x_stage.at[rows_of(t), :], sem_x)

    def y_copy(t, dst_tile):
        return pltpu.make_async_copy(acc.at[rows_of(t), :], y_ref.at[rows_of(dst_tile), :], sem_y.at[t])

    def w_copies(e, f):
        s = f % 2
        cols = pl.ds(f * FF_CHUNK, FF_CHUNK)
        return (pltpu.make_async_copy(wgu_hbm.at[e, :, cols], wg_buf.at[s], sem_w.at[s]),
                pltpu.make_async_copy(wgu_hbm.at[e, :, pl.ds(D_FF + f * FF_CHUNK, FF_CHUNK)], wu_buf.at[s], sem_w.at[s]),
                pltpu.make_async_copy(wdn_hbm.at[e, cols, :], wd_buf.at[s], sem_w.at[s]))

    def loop(lo, hi, fn):
        lax.fori_loop(lo, hi, lambda t, c: (fn(t), c)[1], 0)

    @pl.when(i == 0)
    def _():
        acc[pl.ds(0, ROW_TILE), :] = jnp.zeros((ROW_TILE, D_MODEL), F32)
        first_unused = row0_ref[row0_ref.shape[0] - 1] // ROW_TILE
        n_tiles = y_ref.shape[0] // ROW_TILE
        loop(first_unused, n_tiles, lambda t: y_copy(0, t).start())
        loop(first_unused, n_tiles, lambda t: y_copy(0, t).wait())

    def tile(t, width, f):
        first, last, s = f == 0, f == N_FF - 1, f % 2
        if first:
            for j in range(width):
                @pl.when(t + j < nt_prev)
                def _():
                    y_copy(t + j, 0).wait()
        rows = pl.ds(pl.multiple_of(t * ROW_TILE, ROW_TILE), width * ROW_TILE)
        x = x_bf[rows, :]
        b_gate = bgu_ref[0, :, f * FF_CHUNK:(f + 1) * FF_CHUNK]
        b_up = bgu_ref[0, :, D_FF + f * FF_CHUNK:D_FF + (f + 1) * FF_CHUNK]
        gate = jnp.dot(x, wg_buf[s].astype(BF16), preferred_element_type=F32) + b_gate
        up = jnp.dot(x, wu_buf[s].astype(BF16), preferred_element_type=F32) + b_up
        gate = jnp.minimum(gate, SWIGLU_LIMIT)
        up = jnp.clip(up, -SWIGLU_LIMIT, SWIGLU_LIMIT)
        act = (up + 1.0) * (gate * jax.nn.sigmoid(gate * SWIGLU_ALPHA))
        contrib = jnp.dot(act.astype(BF16), wd_buf[s].astype(BF16), preferred_element_type=F32)
        if first:
            acc[rows, :] = contrib + bd_ref[0]
        else:
            acc[rows, :] += contrib
        if last:
            for j in range(width):
                y_copy(t + j, row0 // ROW_TILE + t + j).start()

    def all_tiles(f):
        loop(0, nt // 2, lambda p: tile(2 * p, 2, f))

        @pl.when(nt % 2 == 1)
        def _():
            tile(nt - 1, 1, f)

    @pl.when(nt > 0)
    def _():
        @pl.when(i == 0)
        def _():
            for n, cp in enumerate(w_copies(expert, 0)):
                cp.start(priority=n % 2)
            loop(0, nt, lambda t: x_copy(row0, t).start())

        loop(0, nt, lambda t: x_copy(row0, t).wait())

        def convert(t):
            x_bf[rows_of(t), :] = x_stage[rows_of(t), :].astype(BF16)

        loop(0, nt, convert)

        def start_weights(e, f):
            for n, cp in enumerate(w_copies(e, f)):
                cp.start(priority=n % 2)

        for f in range(N_FF):
            if f + 1 < N_FF:
                start_weights(expert, f + 1)
            else:
                @pl.when(nt_next > 0)
                def _():
                    start_weights(expert_next, 0)

                loop(0, nt_next, lambda t: x_copy(row0_next, t).start())
            for cp in w_copies(expert, f):
                cp.wait()
            all_tiles(f)
            if f == 0:
                loop(nt, nt_prev, lambda t: y_copy(t, 0).wait())

        @pl.when(nt_next == 0)
        def _():
            loop(0, nt, lambda t: y_copy(t, 0).wait())


def _experts(item_e, item_row0, item_nt, xs, w_gu, b_gu, w_dn, b_dn):
    n_items = item_e.shape[0]
    n_rows = xs.shape[0]
    E = N_EXPERTS
    item_rows = TILES_PER_ITEM * ROW_TILE
    grid_spec = pltpu.PrefetchScalarGridSpec(
        num_scalar_prefetch=3,
        grid=(n_items,),
        in_specs=[pl.BlockSpec(memory_space=pl.ANY),
                  pl.BlockSpec(memory_space=pl.ANY),
                  pl.BlockSpec(memory_space=pl.ANY),
                  pl.BlockSpec((1, 1, 2 * D_FF), lambda i, ie, r0, nt: (ie[i], 0, 0)),
                  pl.BlockSpec((1, 1, D_MODEL), lambda i, ie, r0, nt: (ie[i], 0, 0))],
        out_specs=pl.BlockSpec(memory_space=pl.ANY),
        scratch_shapes=[pltpu.VMEM((item_rows, D_MODEL), F32),
                        pltpu.VMEM((item_rows, D_MODEL), BF16),
                        pltpu.VMEM((item_rows, D_MODEL), F32),
                        pltpu.VMEM((2, D_MODEL, FF_CHUNK), F32),
                        pltpu.VMEM((2, D_MODEL, FF_CHUNK), F32),
                        pltpu.VMEM((2, FF_CHUNK, D_MODEL), F32),
                        pltpu.SemaphoreType.DMA(()),
                        pltpu.SemaphoreType.DMA((TILES_PER_ITEM,)),
                        pltpu.SemaphoreType.DMA((2,))],
    )
    return pl.pallas_call(
        _experts_kernel,
        grid_spec=grid_spec,
        out_shape=jax.ShapeDtypeStruct((n_rows, D_MODEL), F32),
        compiler_params=pltpu.CompilerParams(dimension_semantics=("arbitrary",),
                                             vmem_limit_bytes=V7X_VMEM_LIMIT_BYTES),
        name="experts",
    )(item_e, item_row0, item_nt, xs, w_gu, w_dn, b_gu.reshape(E, 1, 2 * D_FF), b_dn.reshape(E, 1, D_MODEL))


def _combine_kernel(inv_ref, inv_next_ref, x2_ref, gate_ref, fn_ref, y_ref, o_ref, rows, sem, *, tm, last_layer):
    i = pl.program_id(0)
    slot = i % 2

    groups = tm // SUBLANE

    def row_copy(idx_ref, g, j, k, s):
        return pltpu.make_async_copy(y_ref.at[pl.ds(idx_ref[0, 0, (g * SUBLANE + j) * TOP_K + k], 1), :],
                                     rows.at[s, k, g, pl.ds(j, 1), :], sem.at[s])

    def gather(idx_ref, s):
        def issue(g, c):
            for j in range(SUBLANE):
                for k in range(TOP_K):
                    row_copy(idx_ref, g, j, k, s).start(priority=k % 2)
            return c

        lax.fori_loop(0, groups, issue, 0)

    @pl.when(i == 0)
    def _():
        gather(inv_ref, 0)

    @pl.when(i + 1 < pl.num_programs(0))
    def _():
        gather(inv_next_ref, 1 - slot)

    def drain(g, c):
        for j in range(SUBLANE):
            for k in range(TOP_K):
                row_copy(inv_ref, g, j, k, slot).wait()
        return c

    lax.fori_loop(0, groups, drain, 0)
    gate = gate_ref[...]
    out = x2_ref[...]
    for k in range(TOP_K):
        out = out + gate[:, k:k + 1] * rows[slot, k].reshape(tm, D_MODEL)
    o_ref[...] = _rms(out, fn_ref[...]) if last_layer else out


def _combine(inv, x2, gates, final_norm, y, last_layer, tm=128):
    T = x2.shape[0]
    inv3 = inv.reshape(T // tm, 1, tm * TOP_K)
    return pl.pallas_call(
        functools.partial(_combine_kernel, tm=tm, last_layer=last_layer),
        grid=(T // tm,),
        in_specs=[pl.BlockSpec((1, 1, tm * TOP_K), lambda i: (i, 0, 0), memory_space=pltpu.SMEM),
                  pl.BlockSpec((1, 1, tm * TOP_K), lambda i: (jnp.minimum(i + 1, T // tm - 1), 0, 0),
                               memory_space=pltpu.SMEM),
                  pl.BlockSpec((tm, D_MODEL), lambda i: (i, 0)),
                  pl.BlockSpec((tm, LANE), lambda i: (i, 0)),
                  pl.BlockSpec((1, D_MODEL), lambda i: (0, 0)),
                  pl.BlockSpec(memory_space=pl.ANY)],
        out_specs=pl.BlockSpec((tm, D_MODEL), lambda i: (i, 0)),
        out_shape=jax.ShapeDtypeStruct((T, D_MODEL), F32),
        scratch_shapes=[pltpu.VMEM((2, TOP_K, tm // SUBLANE, SUBLANE, D_MODEL), F32),
                        pltpu.SemaphoreType.DMA((2,))],
        compiler_params=pltpu.CompilerParams(dimension_semantics=("arbitrary",)),
        name="combine",
    )(inv3, inv3, x2, gates, final_norm, y)


def _reorder_w_in_kernel(w_ref, o_ref):
    w = w_ref[...]
    tk = w.shape[0]
    a = 2 * GLA_W + 2 * GLA_VW
    lat = a + GLA_RANK
    kr = lat + Q_LORA + KV_LORA
    half = QK_ROPE // 2
    zeros = lambda n: jnp.zeros((tk, n), F32)
    o_ref[:, 0:a] = w[:, 0:a].astype(BF16)
    o_ref[:, a:a + Q_LORA + KV_LORA] = w[:, lat:kr].astype(BF16)
    o_ref[:, 32 * LANE:33 * LANE] = jnp.concatenate([w[:, kr:kr + QK_ROPE], zeros(LANE - QK_ROPE)], axis=1).astype(BF16)
    o_ref[:, 33 * LANE:34 * LANE] = jnp.concatenate(
        [w[:, kr + half:kr + QK_ROPE], w[:, kr:kr + half], zeros(LANE - QK_ROPE)], axis=1).astype(BF16)
    o_ref[:, 34 * LANE:35 * LANE] = jnp.concatenate([w[:, a:lat], zeros(LANE - GLA_RANK)], axis=1).astype(BF16)


def _reorder_w_in(w_in, tk=256):
    in_cols = w_in.shape[1]
    return pl.pallas_call(
        _reorder_w_in_kernel,
        grid=(D_MODEL // tk,),
        in_specs=[pl.BlockSpec((tk, in_cols), lambda i: (i, 0))],
        out_specs=pl.BlockSpec((tk, PROJ_W_COLS), lambda i: (i, 0)),
        out_shape=jax.ShapeDtypeStruct((D_MODEL, PROJ_W_COLS), BF16),
        compiler_params=pltpu.CompilerParams(dimension_semantics=("parallel",)),
        name="reorder_w_in",
    )(w_in)


def _reorder_w_uq(w_uq):
    w = w_uq.reshape(Q_LORA, MLA_HEADS, QK_NOPE + QK_ROPE)
    nope = w[:, :, :QK_NOPE].reshape(Q_LORA, MLA_W)
    pe = w[:, :, QK_NOPE:]
    half = QK_ROPE // 2
    z = jnp.zeros((Q_LORA, MLA_HEADS, LANE - QK_ROPE), w_uq.dtype)
    pe_pad = jnp.concatenate([pe, z], axis=2).reshape(Q_LORA, MLA_W)
    pe_sw = jnp.concatenate([pe[:, :, half:], pe[:, :, :half], z], axis=2).reshape(Q_LORA, MLA_W)
    return jnp.concatenate([nope, pe_pad, pe_sw], axis=1).astype(BF16)


def _reorder_w_ukv(w_ukv):
    w = w_ukv.reshape(KV_LORA, MLA_HEADS, QK_NOPE + V_DIM)
    return jnp.concatenate([w[:, :, :QK_NOPE].reshape(KV_LORA, MLA_W),
                            w[:, :, QK_NOPE:].reshape(KV_LORA, MLA_W)], axis=1).astype(BF16)


def _rope_tables(positions):
    half = QK_ROPE // 2
    inv_freq = 1.0 / (ROPE_THETA ** (jnp.arange(0, QK_ROPE, 2, dtype=F32) / QK_ROPE))
    ang = positions.astype(F32).reshape(-1, 1) * inv_freq
    cos, sin = jnp.cos(ang), jnp.sin(ang)
    z = jnp.zeros((ang.shape[0], LANE - QK_ROPE), F32)
    del half
    return jnp.concatenate([cos, cos, z], axis=1), jnp.concatenate([-sin, sin, z], axis=1)


def _routing_plan(counts, topi, rank, n_items):
    padded = (counts + ROW_TILE - 1) // ROW_TILE * ROW_TILE
    pend = jnp.cumsum(padded)
    pstart = pend - padded
    inv = pstart[topi] + rank
    nt_e = padded // ROW_TILE
    ns_e = (nt_e + TILES_PER_ITEM - 1) // TILES_PER_ITEM
    cs = jnp.cumsum(ns_e)
    j = jnp.arange(n_items, dtype=I32)
    valid = j < cs[-1]
    jj = jnp.where(valid, j, cs[-1] - 1)
    e_j = jnp.sum((jj[:, None] >= cs[None, :]).astype(I32), axis=1)
    local = jj - (cs[e_j] - ns_e[e_j])
    row0 = pstart[e_j] + local * (TILES_PER_ITEM * ROW_TILE)
    nt_j = jnp.where(valid, jnp.clip(nt_e[e_j] - local * TILES_PER_ITEM, 0, TILES_PER_ITEM), 0)
    row0_and_used = jnp.concatenate([row0, pend[-1:]])
    return pend.astype(I32), inv.astype(I32), e_j.astype(I32), row0_and_used.astype(I32), nt_j.astype(I32)


def kernel(x, positions, attn_norm, w_in, gla_gate_up, gla_gate_bias, gla_out_norm, mla_q_norm, w_uq, mla_kv_norm, w_ukv, mla_out_norm, w_out, ffn_norm, router_w, router_b, w_gate_up, b_gate_up, w_down, b_down, final_norm):
    B, S, D = x.shape
    T = B * S
    A = T * TOP_K
    n_tiles = (A + N_EXPERTS * (ROW_TILE - 1)) // ROW_TILE
    n_rows = n_tiles * ROW_TILE
    n_items = (n_tiles + N_EXPERTS * (TILES_PER_ITEM - 1)) // TILES_PER_ITEM
    row2 = lambda v: v.reshape(1, -1)

    xf = x.reshape(T, D)
    cos_t, sin_t = _rope_tables(positions)
    for l in range(w_in.shape[0]):
        gup_pad = jnp.zeros((LANE, GLA_W), F32).at[:GLA_RANK].set(gla_gate_up[l]).astype(BF16)
        proj, loga = _in_proj(xf, row2(attn_norm[l]), _reorder_w_in(w_in[l]), gup_pad, row2(gla_gate_bias[l]))
        o_gla = _gla(proj, loga, row2(gla_out_norm[l]), B, S)
        q, k, v = _mla_proj(proj, cos_t, sin_t, row2(mla_q_norm[l]), row2(mla_kv_norm[l]),
                            _reorder_w_uq(w_uq[l]), _reorder_w_ukv(w_ukv[l]))
        o_mla = _mla_attn(q, k, v, B, S)
        rw_pad = jnp.zeros((D, LANE), F32).at[:, :N_EXPERTS].set(router_w[l])
        rw_hi = rw_pad.astype(BF16)
        rw_lo = (rw_pad - rw_hi.astype(F32)).astype(BF16)
        rb_pad = jnp.zeros((1, LANE), F32).at[0, :N_EXPERTS].set(router_b[l])
        x2, h2, topi, gates, rank, counts = _out_router(
            xf, o_gla, o_mla, w_out[l].astype(BF16), row2(mla_out_norm[l]), row2(ffn_norm[l]), rw_hi, rw_lo, rb_pad)
        pend, inv, item_e, item_row0, item_nt = _routing_plan(
            counts[0, :N_EXPERTS], topi[:, :TOP_K], rank[:, :TOP_K], n_items)
        xs = _dispatch(pend, inv, h2, n_rows)
        y = _experts(item_e, item_row0, item_nt, xs, w_gate_up[l], b_gate_up[l], w_down[l], b_down[l])
        xf = _combine(inv, x2, gates, row2(final_norm), y, last_layer=(l == w_in.shape[0] - 1))
    return xf.reshape(B, S, D)
```

```python
import functools

import jax
import jax.numpy as jnp
from jax import lax
from jax.experimental import pallas as pl
from jax.experimental.pallas import tpu as pltpu

F32 = jnp.float32
BF16 = jnp.bfloat16
I32 = jnp.int32
HIGHEST = lax.Precision.HIGHEST

LANE = 128
SUBLANE = 8
V7X_VMEM_LIMIT_BYTES = 60000 * 1024

D_MODEL = 2048
GLA_HEADS, GLA_DK, GLA_DV, GLA_RANK, GLA_TAU, GLA_CHUNK = 4, 128, 256, 16, 16.0, 64
MLA_HEADS, Q_LORA, KV_LORA, QK_NOPE, QK_ROPE, V_DIM = 8, 512, 512, 128, 64, 128
ROPE_THETA = 10000.0
N_EXPERTS, TOP_K, D_FF = 32, 4, 2048
SWIGLU_LIMIT, SWIGLU_ALPHA = 7.0, 1.702
EPS = 1e-6
LOG2_E = 1.4426950408889634
GLA_W = GLA_HEADS * GLA_DK
GLA_VW = GLA_HEADS * GLA_DV
MLA_W = MLA_HEADS * V_DIM
QK_PAD = 2 * LANE

PROJ_OUT_COLS = 34 * LANE
PROJ_W_COLS = 35 * LANE

ROW_TILE = 256
TILES_PER_ITEM = 5
FF_CHUNK = 512
N_FF = D_FF // FF_CHUNK


def _rms(x, g):
    return x * lax.rsqrt(jnp.mean(x * x, axis=-1, keepdims=True) + EPS) * g


def _inproj_kernel(x_ref, g_ref, w_ref, gup_ref, gb_ref, proj_ref, loga_ref):
    h = _rms(x_ref[...], g_ref[...]).astype(BF16)
    p = jnp.dot(h, w_ref[...], preferred_element_type=F32)
    proj_ref[...] = p[:, :PROJ_OUT_COLS].astype(BF16)
    z = jnp.dot(p[:, PROJ_OUT_COLS:].astype(BF16), gup_ref[...], preferred_element_type=F32) + gb_ref[...]
    log_sig = jnp.minimum(z, 0.0) - jnp.log(1.0 + jnp.exp(-jnp.abs(z)))
    loga_ref[...] = log_sig * (1.0 / GLA_TAU)


def _in_proj(xf, g, w_r, gup_pad, gbias, tm=256):
    T = xf.shape[0]
    return pl.pallas_call(
        _inproj_kernel,
        grid=(T // tm,),
        in_specs=[pl.BlockSpec((tm, D_MODEL), lambda i: (i, 0)),
                  pl.BlockSpec((1, D_MODEL), lambda i: (0, 0)),
                  pl.BlockSpec((D_MODEL, PROJ_W_COLS), lambda i: (0, 0)),
                  pl.BlockSpec((LANE, GLA_W), lambda i: (0, 0)),
                  pl.BlockSpec((1, GLA_W), lambda i: (0, 0))],
        out_specs=[pl.BlockSpec((tm, PROJ_OUT_COLS), lambda i: (i, 0)),
                   pl.BlockSpec((tm, GLA_W), lambda i: (i, 0))],
        out_shape=[jax.ShapeDtypeStruct((T, PROJ_OUT_COLS), BF16),
                   jax.ShapeDtypeStruct((T, GLA_W), F32)],
        compiler_params=pltpu.CompilerParams(dimension_semantics=("parallel",),
                                             vmem_limit_bytes=V7X_VMEM_LIMIT_BYTES),
        name="in_proj",
    )(xf, g, w_r, gup_pad, gbias)


def _gla_kernel(q_ref, k_ref, v_ref, r_ref, la_ref, gn_ref, o_ref, st_ref, *, n_chunks):
    @pl.when(pl.program_id(1) == 0)
    def _():
        st_ref[...] = jnp.zeros_like(st_ref)

    C = GLA_CHUNK
    row = lax.broadcasted_iota(I32, (C, C), 0)
    col = lax.broadcasted_iota(I32, (C, C), 1)
    causal = col <= row
    tri = causal.astype(F32)
    for c in range(n_chunks):
        sl = pl.ds(c * C, C)
        b = jnp.dot(tri, la_ref[sl, :], precision=HIGHEST, preferred_element_type=F32)
        b_last = b[C - 1:C, :]
        q_dec = (q_ref[sl, :].astype(F32) * (GLA_DK ** -0.5) * jnp.exp(b)).astype(BF16)
        kf = k_ref[sl, :].astype(F32)
        k_intra = (kf * jnp.exp(-b)).astype(BF16)
        k_state = (kf * jnp.exp(b_last - b)).astype(BF16)
        decay = jnp.exp(b_last)
        for h in range(GLA_HEADS):
            hs = slice(h * GLA_DK, (h + 1) * GLA_DK)
            vs = slice(h * GLA_DV, (h + 1) * GLA_DV)
            v = v_ref[sl, vs]
            attn = lax.dot_general(q_dec[:, hs], k_intra[:, hs], (((1,), (1,)), ((), ())),
                                   preferred_element_type=F32)
            attn = jnp.where(causal, attn, 0.0).astype(BF16)
            st = st_ref[h]
            o = jnp.dot(attn, v, preferred_element_type=F32)
            o = o + lax.dot_general(q_dec[:, hs], st.astype(BF16), (((1,), (1,)), ((), ())),
                                    preferred_element_type=F32)
            kv_t = lax.dot_general(v, k_state[:, hs], (((0,), (0,)), ((), ())),
                                   preferred_element_type=F32)
            st_ref[h] = st * decay[:, hs] + kv_t
            r = r_ref[sl, vs].astype(F32)
            o_ref[sl, vs] = (_rms(o, gn_ref[...]) * (r * jax.nn.sigmoid(r))).astype(BF16)


def _gla(proj, loga, gn, batch, seq, ct=256):
    nct = seq // ct
    T = batch * seq
    row_map = lambda blk: (lambda b, c: (b * nct + c, blk))
    return pl.pallas_call(
        functools.partial(_gla_kernel, n_chunks=ct // GLA_CHUNK),
        grid=(batch, nct),
        in_specs=[pl.BlockSpec((ct, GLA_W), row_map(0)),
                  pl.BlockSpec((ct, GLA_W), row_map(1)),
                  pl.BlockSpec((ct, GLA_VW), row_map(1)),
                  pl.BlockSpec((ct, GLA_VW), row_map(2)),
                  pl.BlockSpec((ct, GLA_W), row_map(0)),
                  pl.BlockSpec((1, GLA_DV), lambda b, c: (0, 0))],
        out_specs=pl.BlockSpec((ct, GLA_VW), row_map(0)),
        out_shape=jax.ShapeDtypeStruct((T, GLA_VW), BF16),
        scratch_shapes=[pltpu.VMEM((GLA_HEADS, GLA_DV, GLA_DK), F32)],
        compiler_params=pltpu.CompilerParams(dimension_semantics=("parallel", "arbitrary")),
        name="gla",
    )(proj, proj, proj, proj, loga, gn)


def _mla_proj_kernel(ql_ref, kvl_ref, kr_ref, krs_ref, cos_ref, sin_ref, qn_ref, kvn_ref,
                     wq_ref, wkv_ref, q_ref, k_ref, vt_ref):
    cos = cos_ref[...]
    sin = sin_ref[...]
    scale = (QK_NOPE + QK_ROPE) ** -0.5 * LOG2_E
    qlat = _rms(ql_ref[...].astype(F32), qn_ref[...]).astype(BF16)
    qall = jnp.dot(qlat, wq_ref[...], preferred_element_type=F32)
    kvlat = _rms(kvl_ref[...].astype(F32), kvn_ref[...]).astype(BF16)
    kv = jnp.dot(kvlat, wkv_ref[...], preferred_element_type=F32)
    k_pe = (kr_ref[...].astype(F32) * cos + krs_ref[...].astype(F32) * sin).astype(BF16)
    for h in range(MLA_HEADS):
        hl = slice(h * LANE, (h + 1) * LANE)
        pe = slice(MLA_W + h * LANE, MLA_W + (h + 1) * LANE)
        ps = slice(2 * MLA_W + h * LANE, 2 * MLA_W + (h + 1) * LANE)
        q_ref[h, :, 0:LANE] = (qall[:, hl] * scale).astype(BF16)
        q_ref[h, :, LANE:QK_PAD] = ((qall[:, pe] * cos + qall[:, ps] * sin) * scale).astype(BF16)
        k_ref[h, :, 0:LANE] = kv[:, hl].astype(BF16)
        k_ref[h, :, LANE:QK_PAD] = k_pe
        vt_ref[h] = kv[:, MLA_W + h * LANE:MLA_W + (h + 1) * LANE].T.astype(BF16)


def _mla_proj(proj, cos_t, sin_t, qn, kvn, wq, wkv, tm=512):
    T = proj.shape[0]
    H = MLA_HEADS
    return pl.pallas_call(
        _mla_proj_kernel,
        grid=(T // tm,),
        in_specs=[pl.BlockSpec((tm, Q_LORA), lambda i: (i, 6)),
                  pl.BlockSpec((tm, KV_LORA), lambda i: (i, 7)),
                  pl.BlockSpec((tm, LANE), lambda i: (i, 32)),
                  pl.BlockSpec((tm, LANE), lambda i: (i, 33)),
                  pl.BlockSpec((tm, LANE), lambda i: (i, 0)),
                  pl.BlockSpec((tm, LANE), lambda i: (i, 0)),
                  pl.BlockSpec((1, Q_LORA), lambda i: (0, 0)),
                  pl.BlockSpec((1, KV_LORA), lambda i: (0, 0)),
                  pl.BlockSpec((Q_LORA, 3 * MLA_W), lambda i: (0, 0)),
                  pl.BlockSpec((KV_LORA, 2 * MLA_W), lambda i: (0, 0))],
        out_specs=[pl.BlockSpec((H, tm, QK_PAD), lambda i: (0, i, 0)),
                   pl.BlockSpec((H, tm, QK_PAD), lambda i: (0, i, 0)),
                   pl.BlockSpec((H, V_DIM, tm), lambda i: (0, 0, i))],
        out_shape=[jax.ShapeDtypeStruct((H, T, QK_PAD), BF16),
                   jax.ShapeDtypeStruct((H, T, QK_PAD), BF16),
                   jax.ShapeDtypeStruct((H, V_DIM, T), BF16)],
        compiler_params=pltpu.CompilerParams(dimension_semantics=("parallel",)),
        name="mla_proj",
    )(proj, proj, proj, proj, cos_t, sin_t, qn, kvn, wq, wkv)


def _attn_kernel(q_ref, k_ref, vt_ref, o_ref, s0_ref, s1_ref, p0_ref, p1_ref, a0_ref, a1_ref,
                 m_ref, l_ref, acc_ref, *, tq):
    qi = pl.program_id(2)
    q = q_ref[0]
    s_buf, p_buf, a_buf = (s0_ref, s1_ref), (p0_ref, p1_ref), (a0_ref, a1_ref)
    m_ref[...] = jnp.full_like(m_ref, -jnp.inf)
    l_ref[...] = jnp.zeros_like(l_ref)
    acc_ref[...] = jnp.zeros_like(acc_ref)

    def chunk(c):
        return pl.ds(pl.multiple_of(c * tq, tq), tq)

    def score(c, slot):
        s_buf[slot][...] = lax.dot_general(k_ref[0, chunk(c), :], q, (((1,), (1,)), ((), ())),
                                           preferred_element_type=F32)

    def softmax(slot, diagonal):
        s_t = s_buf[slot][...]
        if diagonal:
            key = lax.broadcasted_iota(I32, (tq, tq), 0)
            qry = lax.broadcasted_iota(I32, (tq, tq), 1)
            s_t = jnp.where(key <= qry, s_t, -jnp.inf)
        m_old = m_ref[...]
        m_new = jnp.maximum(m_old, jnp.max(s_t, axis=0, keepdims=True))
        p_t = jnp.exp2(s_t - m_new)
        alpha = jnp.exp2(m_old - m_new)
        l_ref[...] = alpha * l_ref[...] + jnp.sum(p_t, axis=0, keepdims=True)
        m_ref[...] = m_new
        a_buf[slot][...] = alpha
        p_buf[slot][...] = p_t.astype(BF16)

    def value(c, slot):
        acc_ref[...] = a_buf[slot][...] * acc_ref[...] + jnp.dot(vt_ref[0, :, chunk(c)], p_buf[slot][...],
                                                                preferred_element_type=F32)

    def stages(c, slot):
        value(c, slot)
        softmax(1 - slot, False)
        score(c + 2, slot)

    score(0, 0)

    @pl.when(qi == 0)
    def _():
        softmax(0, True)
        value(0, 0)

    @pl.when(qi > 0)
    def _():
        score(1, 1)
        softmax(0, False)
        n_full = qi - 1

        def pair(j, carry):
            stages(2 * j, 0)
            stages(2 * j + 1, 1)
            return carry

        lax.fori_loop(0, n_full // 2, pair, 0)

        @pl.when(n_full % 2 == 1)
        def _():
            stages(qi - 2, 0)

        for parity in (0, 1):
            @pl.when(qi % 2 == parity)
            def _():
                value(qi - 1, 1 - parity)
                softmax(parity, True)
                value(qi, parity)

    o_ref[...] = (acc_ref[...] / l_ref[...]).T


def _mla_attn(q, k, vt, batch, seq, tq=512):
    H, T, _ = q.shape
    nq = seq // tq
    return pl.pallas_call(
        functools.partial(_attn_kernel, tq=tq),
        grid=(batch, H, nq),
        in_specs=[pl.BlockSpec((1, tq, QK_PAD), lambda b, h, i: (h, b * nq + i, 0)),
                  pl.BlockSpec((1, seq, QK_PAD), lambda b, h, i: (h, b, 0)),
                  pl.BlockSpec((1, V_DIM, seq), lambda b, h, i: (h, 0, b))],
        out_specs=pl.BlockSpec((tq, V_DIM), lambda b, h, i: (b * nq + i, h)),
        out_shape=jax.ShapeDtypeStruct((T, MLA_W), F32),
        scratch_shapes=[pltpu.VMEM((tq, tq), F32), pltpu.VMEM((tq, tq), F32),
                        pltpu.VMEM((tq, tq), BF16), pltpu.VMEM((tq, tq), BF16),
                        pltpu.VMEM((1, tq), F32), pltpu.VMEM((1, tq), F32),
                        pltpu.VMEM((1, tq), F32), pltpu.VMEM((1, tq), F32), pltpu.VMEM((V_DIM, tq), F32)],
        compiler_params=pltpu.CompilerParams(dimension_semantics=("parallel", "parallel", "arbitrary")),
        name="mla_attn",
    )(q, k, vt)


def _out_router_kernel(x_ref, og_ref, om_ref, wo_ref, mn_ref, fn_ref, rwh_ref, rwl_ref, rb_ref,
                       x2_ref, h2_ref, topi_ref, gate_ref, rank_ref, cnt_ref, carry_ref, *, tm):
    @pl.when(pl.program_id(0) == 0)
    def _():
        carry_ref[...] = jnp.zeros_like(carry_ref)

    om = _rms(om_ref[...], mn_ref[...]).astype(BF16)
    mix = jnp.dot(og_ref[...], wo_ref[0:GLA_VW, :], preferred_element_type=F32)
    mix = mix + jnp.dot(om, wo_ref[GLA_VW:GLA_VW + MLA_W, :], preferred_element_type=F32)
    x2 = x_ref[...] + mix
    x2_ref[...] = x2
    h2 = _rms(x2, fn_ref[...])
    h2_ref[...] = h2
    h2_hi = h2.astype(BF16)
    h2_lo = (h2 - h2_hi.astype(F32)).astype(BF16)
    logits = (jnp.dot(h2_hi, rwh_ref[...], preferred_element_type=F32)
              + jnp.dot(h2_lo, rwh_ref[...], preferred_element_type=F32)
              + jnp.dot(h2_hi, rwl_ref[...], preferred_element_type=F32)) + rb_ref[...]
    lane = lax.broadcasted_iota(I32, (tm, LANE), 1).astype(F32)
    cur = jnp.where(lane < N_EXPERTS, logits, -jnp.inf)
    vals, idxs = [], []
    for _ in range(TOP_K):
        m = jnp.max(cur, axis=-1, keepdims=True)
        idx = jnp.min(jnp.where(cur == m, lane, float(N_EXPERTS - 1)), axis=-1, keepdims=True)
        vals.append(m)
        idxs.append(idx)
        cur = jnp.where(lane == idx, -jnp.inf, cur)
    es = [jnp.exp(v - vals[0]) for v in vals]
    den = es[0] + es[1] + es[2] + es[3]
    gates = jnp.zeros((tm, LANE), F32)
    topi = jnp.zeros((tm, LANE), F32)
    multi = jnp.zeros((tm, LANE), F32)
    for k in range(TOP_K):
        gates = jnp.where(lane == k, es[k] / den, gates)
        topi = jnp.where(lane == k, idxs[k], topi)
        multi = multi + (lane == idxs[k]).astype(F32)
    gate_ref[...] = gates
    topi_ref[0] = topi.T[0:SUBLANE, :].astype(I32)
    r = lax.broadcasted_iota(I32, (tm, tm), 0)
    c = lax.broadcasted_iota(I32, (tm, tm), 1)
    strict = (c < r).astype(BF16)
    pref = jnp.dot(strict, multi.astype(BF16), preferred_element_type=F32) + carry_ref[...]
    rank = jnp.zeros((tm, LANE), F32)
    for k in range(TOP_K):
        rk = jnp.sum(jnp.where(lane == idxs[k], pref, 0.0), axis=-1, keepdims=True)
        rank = jnp.where(lane == k, rk, rank)
    rank_ref[0] = rank.T[0:SUBLANE, :].astype(I32)
    carry_ref[...] = carry_ref[...] + jnp.sum(multi, axis=0, keepdims=True)
    cnt_ref[...] = carry_ref[...].astype(I32)


def _out_router(xf, o_gla, o_mla, w_out, mn, fn, rw_hi, rw_lo, rb_pad, tm=256):
    T = xf.shape[0]
    row = lambda i: (i, 0)
    fixed = lambda i: (0, 0)
    return pl.pallas_call(
        functools.partial(_out_router_kernel, tm=tm),
        grid=(T // tm,),
        in_specs=[pl.BlockSpec((tm, D_MODEL), row),
                  pl.BlockSpec((tm, GLA_VW), row),
                  pl.BlockSpec((tm, MLA_W), row),
                  pl.BlockSpec((GLA_VW + MLA_W, D_MODEL), fixed),
                  pl.BlockSpec((1, MLA_W), fixed),
                  pl.BlockSpec((1, D_MODEL), fixed),
                  pl.BlockSpec((D_MODEL, LANE), fixed),
                  pl.BlockSpec((D_MODEL, LANE), fixed),
                  pl.BlockSpec((1, LANE), fixed)],
        out_specs=[pl.BlockSpec((tm, D_MODEL), row),
                   pl.BlockSpec((tm, D_MODEL), row),
                   pl.BlockSpec((1, SUBLANE, tm), lambda i: (i, 0, 0)),
                   pl.BlockSpec((tm, LANE), row),
                   pl.BlockSpec((1, SUBLANE, tm), lambda i: (i, 0, 0)),
                   pl.BlockSpec((1, LANE), fixed)],
        out_shape=[jax.ShapeDtypeStruct((T, D_MODEL), F32),
                   jax.ShapeDtypeStruct((T, D_MODEL), F32),
                   jax.ShapeDtypeStruct((T // tm, SUBLANE, tm), I32),
                   jax.ShapeDtypeStruct((T, LANE), F32),
                   jax.ShapeDtypeStruct((T // tm, SUBLANE, tm), I32),
                   jax.ShapeDtypeStruct((1, LANE), I32)],
        scratch_shapes=[pltpu.VMEM((1, LANE), F32)],
        compiler_params=pltpu.CompilerParams(dimension_semantics=("arbitrary",),
                                             vmem_limit_bytes=V7X_VMEM_LIMIT_BYTES),
        name="out_router",
    )(xf, o_gla, o_mla, w_out, mn, fn, rw_hi, rw_lo, rb_pad)


N_DISPATCH_SLOTS = 3


def _dispatch_kernel(pend_ref, inv_ref, h_hbm, xs_ref, hbuf, zero_ref, sem_in, sem_out, sem_fill, *, tm, n_tiles):
    i = pl.program_id(0)
    n_steps = pl.num_programs(0)

    groups = tm // SUBLANE

    def load(tile, s):
        return pltpu.make_async_copy(h_hbm.at[pl.ds(tile * groups, groups)], hbuf.at[s], sem_in.at[s])

    def row_copy(g, j, dst_row, s):
        return pltpu.make_async_copy(hbuf.at[s, g, pl.ds(j, 1), :], xs_ref.at[pl.ds(dst_row, 1), :], sem_out.at[s])

    def drain(s):
        def body(g, c):
            for j in range(SUBLANE):
                for k in range(TOP_K):
                    row_copy(g, j, 0, s).wait()
            return c

        lax.fori_loop(0, groups, body, 0)

    @pl.when(i == 0)
    def _():
        load(0, 0).start()
        zero_ref[...] = jnp.zeros_like(zero_ref)

        def fill_copy(t):
            return pltpu.make_async_copy(zero_ref, xs_ref.at[pl.ds(pl.multiple_of(t * ROW_TILE, ROW_TILE), ROW_TILE), :],
                                         sem_fill)

        def fill(e, wait):
            prev = jnp.where(e == 0, 0, pend_ref[jnp.maximum(e - 1, 0)])

            @pl.when(pend_ref[e] > prev)
            def _():
                cp = fill_copy(pend_ref[e] // ROW_TILE - 1)
                if wait:
                    cp.wait()
                else:
                    cp.start()

        def tail(t, wait):
            if wait:
                fill_copy(t).wait()
            else:
                fill_copy(t).start()

        first_unused = pend_ref[N_EXPERTS - 1] // ROW_TILE
        lax.fori_loop(0, N_EXPERTS, lambda e, c: (fill(e, False), c)[1], 0)
        lax.fori_loop(first_unused, n_tiles, lambda t, c: (tail(t, False), c)[1], 0)
        lax.fori_loop(0, N_EXPERTS, lambda e, c: (fill(e, True), c)[1], 0)
        lax.fori_loop(first_unused, n_tiles, lambda t, c: (tail(t, True), c)[1], 0)

    slot = i % N_DISPATCH_SLOTS
    next_slot = (i + 1) % N_DISPATCH_SLOTS

    @pl.when(i >= 2)
    def _():
        drain(next_slot)

    @pl.when(i + 1 < n_steps)
    def _():
        load(i + 1, next_slot).start()

    load(i, slot).wait()

    def issue(g, c):
        for j in range(SUBLANE):
            for k in range(TOP_K):
                dst_row = inv_ref[0, 0, k * tm + g * SUBLANE + j]
                row_copy(g, j, dst_row, slot).start(priority=k % 2)
        return c

    lax.fori_loop(0, groups, issue, 0)

    @pl.when(i == n_steps - 1)
    def _():
        @pl.when(i >= 1)
        def _():
            drain((i + 2) % N_DISPATCH_SLOTS)

        drain(slot)


def _dispatch(pend, inv3, h2, n_rows):
    T = h2.shape[0]
    tm = inv3.shape[2] // TOP_K
    return pl.pallas_call(
        functools.partial(_dispatch_kernel, tm=tm, n_tiles=n_rows // ROW_TILE),
        grid=(T // tm,),
        in_specs=[pl.BlockSpec(memory_space=pltpu.SMEM),
                  pl.BlockSpec((1, 1, tm * TOP_K), lambda i: (i, 0, 0), memory_space=pltpu.SMEM),
                  pl.BlockSpec(memory_space=pl.ANY)],
        out_specs=pl.BlockSpec(memory_space=pl.ANY),
        out_shape=jax.ShapeDtypeStruct((n_rows, D_MODEL), F32),
        scratch_shapes=[pltpu.VMEM((N_DISPATCH_SLOTS, tm // SUBLANE, SUBLANE, D_MODEL), F32),
                        pltpu.VMEM((ROW_TILE, D_MODEL), F32),
                        pltpu.SemaphoreType.DMA((N_DISPATCH_SLOTS,)),
                        pltpu.SemaphoreType.DMA((N_DISPATCH_SLOTS,)),
                        pltpu.SemaphoreType.DMA(())],
        compiler_params=pltpu.CompilerParams(dimension_semantics=("arbitrary",)),
        name="dispatch",
    )(pend, inv3, h2.reshape(T // SUBLANE, SUBLANE, D_MODEL))


def _experts_kernel(ie_ref, row0_ref, nt_ref, xs_ref, wgu_hbm, wdn_hbm, bgu_ref, bd_ref,
                    y_ref, x_stage, x_bf, acc, wg_buf, wu_buf, wd_buf, sem_x, sem_y, sem_w):
    i = pl.program_id(0)
    n_items = nt_ref.shape[0]
    nt = nt_ref[i]
    row0 = row0_ref[i]
    expert = ie_ref[i]
    nt_prev = jnp.where(i > 0, nt_ref[jnp.maximum(i - 1, 0)], 0)
    nxt = jnp.minimum(i + 1, n_items - 1)
    nt_next = jnp.where(i + 1 < n_items, nt_ref[nxt], 0)
    row0_next = row0_ref[nxt]
    expert_next = ie_ref[nxt]

    def rows_of(t):
        return pl.ds(pl.multiple_of(t * ROW_TILE, ROW_TILE), ROW_TILE)

    def x_copy(first_row, t):
        return pltpu.make_async_copy(xs_ref.at[pl.ds(pl.multiple_of(first_row + t * ROW_TILE, ROW_TILE), ROW_TILE), :],
                                     x_stage.at[rows_of(t), :], sem_x)

    def y_copy(t, dst_tile):
        return pltpu.make_async_copy(acc.at[rows_of(t), :], y_ref.at[rows_of(dst_tile), :], sem_y.at[t])

    def w_copies(e, f):
        s = f % 2
        cols = pl.ds(f * FF_CHUNK, FF_CHUNK)
        return (pltpu.make_async_copy(wgu_hbm.at[e, :, cols], wg_buf.at[s], sem_w.at[s]),
                pltpu.make_async_copy(wgu_hbm.at[e, :, pl.ds(D_FF + f * FF_CHUNK, FF_CHUNK)], wu_buf.at[s], sem_w.at[s]),
                pltpu.make_async_copy(wdn_hbm.at[e, cols, :], wd_buf.at[s], sem_w.at[s]))

    def loop(lo, hi, fn):
        lax.fori_loop(lo, hi, lambda t, c: (fn(t), c)[1], 0)

    @pl.when(i == 0)
    def _():
        acc[pl.ds(0, ROW_TILE), :] = jnp.zeros((ROW_TILE, D_MODEL), F32)
        first_unused = row0_ref[row0_ref.shape[0] - 1] // ROW_TILE
        n_tiles = y_ref.shape[0] // ROW_TILE
        loop(first_unused, n_tiles, lambda t: y_copy(0, t).start())
        loop(first_unused, n_tiles, lambda t: y_copy(0, t).wait())

    def tile(t, width, f):
        first, last, s = f == 0, f == N_FF - 1, f % 2
        if first:
            for j in range(width):
                @pl.when(t + j < nt_prev)
                def _():
                    y_copy(t + j, 0).wait()
        rows = pl.ds(pl.multiple_of(t * ROW_TILE, ROW_TILE), width * ROW_TILE)
        x = x_bf[rows, :]
        b_gate = bgu_ref[0, :, f * FF_CHUNK:(f + 1) * FF_CHUNK]
        b_up = bgu_ref[0, :, D_FF + f * FF_CHUNK:D_FF + (f + 1) * FF_CHUNK]
        gate = jnp.dot(x, wg_buf[s].astype(BF16), preferred_element_type=F32) + b_gate
        up = jnp.dot(x, wu_buf[s].astype(BF16), preferred_element_type=F32) + b_up
        gate = jnp.minimum(gate, SWIGLU_LIMIT)
        up = jnp.clip(up, -SWIGLU_LIMIT, SWIGLU_LIMIT)
        act = (up + 1.0) * (gate * jax.nn.sigmoid(gate * SWIGLU_ALPHA))
        contrib = jnp.dot(act.astype(BF16), wd_buf[s].astype(BF16), preferred_element_type=F32)
        if first:
            acc[rows, :] = contrib + bd_ref[0]
        else:
            acc[rows, :] += contrib
        if last:
            for j in range(width):
                y_copy(t + j, row0 // ROW_TILE + t + j).start()

    def all_tiles(f):
        loop(0, nt // 2, lambda p: tile(2 * p, 2, f))

        @pl.when(nt % 2 == 1)
        def _():
            tile(nt - 1, 1, f)

    @pl.when(nt > 0)
    def _():
        @pl.when(i == 0)
        def _():
            for n, cp in enumerate(w_copies(expert, 0)):
                cp.start(priority=n % 2)
            loop(0, nt, lambda t: x_copy(row0, t).start())

        loop(0, nt, lambda t: x_copy(row0, t).wait())

        def convert(t):
            x_bf[rows_of(t), :] = x_stage[rows_of(t), :].astype(BF16)

        loop(0, nt, convert)

        def start_weights(e, f):
            for n, cp in enumerate(w_copies(e, f)):
                cp.start(priority=n % 2)

        for f in range(N_FF):
            if f + 1 < N_FF:
                start_weights(expert, f + 1)
            else:
                @pl.when(nt_next > 0)
                def _():
                    start_weights(expert_next, 0)
            if f == 0:
                loop(0, nt_next, lambda t: x_copy(row0_next, t).start())
            for cp in w_copies(expert, f):
                cp.wait()
            all_tiles(f)
            if f == 0:
                loop(nt, nt_prev, lambda t: y_copy(t, 0).wait())

        @pl.when(nt_next == 0)
        def _():
            loop(0, nt, lambda t: y_copy(t, 0).wait())


def _experts(item_e, item_row0, item_nt, xs, w_gu, b_gu, w_dn, b_dn):
    n_items = item_e.shape[0]
    n_rows = xs.shape[0]
    E = N_EXPERTS
    item_rows = TILES_PER_ITEM * ROW_TILE
    grid_spec = pltpu.PrefetchScalarGridSpec(
        num_scalar_prefetch=3,
        grid=(n_items,),
        in_specs=[pl.BlockSpec(memory_space=pl.ANY),
                  pl.BlockSpec(memory_space=pl.ANY),
                  pl.BlockSpec(memory_space=pl.ANY),
                  pl.BlockSpec((1, 1, 2 * D_FF), lambda i, ie, r0, nt: (ie[i], 0, 0)),
                  pl.BlockSpec((1, 1, D_MODEL), lambda i, ie, r0, nt: (ie[i], 0, 0))],
        out_specs=pl.BlockSpec(memory_space=pl.ANY),
        scratch_shapes=[pltpu.VMEM((item_rows, D_MODEL), F32),
                        pltpu.VMEM((item_rows, D_MODEL), BF16),
                        pltpu.VMEM((item_rows, D_MODEL), F32),
                        pltpu.VMEM((2, D_MODEL, FF_CHUNK), F32),
                        pltpu.VMEM((2, D_MODEL, FF_CHUNK), F32),
                        pltpu.VMEM((2, FF_CHUNK, D_MODEL), F32),
                        pltpu.SemaphoreType.DMA(()),
                        pltpu.SemaphoreType.DMA((TILES_PER_ITEM,)),
                        pltpu.SemaphoreType.DMA((2,))],
    )
    return pl.pallas_call(
        _experts_kernel,
        grid_spec=grid_spec,
        out_shape=jax.ShapeDtypeStruct((n_rows, D_MODEL), F32),
        compiler_params=pltpu.CompilerParams(dimension_semantics=("arbitrary",),
                                             vmem_limit_bytes=V7X_VMEM_LIMIT_BYTES),
        name="experts",
    )(item_e, item_row0, item_nt, xs, w_gu, w_dn, b_gu.reshape(E, 1, 2 * D_FF), b_dn.reshape(E, 1, D_MODEL))


def _combine_kernel(inv_ref, inv_next_ref, x2_ref, gate_ref, fn_ref, y_ref, o_ref, rows, sem, *,
                    tm, route_tile, last_layer):
    i = pl.program_id(0)
    slot = i % 2
    per_block = route_tile // tm

    groups = tm // SUBLANE

    def row_copy(idx_ref, first, g, j, k, s):
        src_row = idx_ref[0, 0, k * route_tile + first + g * SUBLANE + j]
        return pltpu.make_async_copy(y_ref.at[pl.ds(src_row, 1), :], rows.at[s, k, g, pl.ds(j, 1), :], sem.at[s])

    def gather(idx_ref, step, s):
        first = (step % per_block) * tm

        def issue(g, c):
            for j in range(SUBLANE):
                for k in range(TOP_K):
                    row_copy(idx_ref, first, g, j, k, s).start(priority=k % 2)
            return c

        lax.fori_loop(0, groups, issue, 0)

    @pl.when(i == 0)
    def _():
        gather(inv_ref, 0, 0)

    @pl.when(i + 1 < pl.num_programs(0))
    def _():
        gather(inv_next_ref, i + 1, 1 - slot)

    def drain(g, c):
        for j in range(SUBLANE):
            for k in range(TOP_K):
                row_copy(inv_ref, 0, g, j, k, slot).wait()
        return c

    lax.fori_loop(0, groups, drain, 0)
    gate = gate_ref[...]
    out = x2_ref[...]
    for k in range(TOP_K):
        out = out + gate[:, k:k + 1] * rows[slot, k].reshape(tm, D_MODEL)
    o_ref[...] = _rms(out, fn_ref[...]) if last_layer else out


def _combine(inv3, x2, gates, final_norm, y, last_layer, tm=128):
    T = x2.shape[0]
    route_tile = inv3.shape[2] // TOP_K
    per_block = route_tile // tm
    idx_block = (1, 1, TOP_K * route_tile)
    return pl.pallas_call(
        functools.partial(_combine_kernel, tm=tm, route_tile=route_tile, last_layer=last_layer),
        grid=(T // tm,),
        in_specs=[pl.BlockSpec(idx_block, lambda i: (i // per_block, 0, 0), memory_space=pltpu.SMEM),
                  pl.BlockSpec(idx_block, lambda i: (jnp.minimum(i + 1, T // tm - 1) // per_block, 0, 0),
                               memory_space=pltpu.SMEM),
                  pl.BlockSpec((tm, D_MODEL), lambda i: (i, 0)),
                  pl.BlockSpec((tm, LANE), lambda i: (i, 0)),
                  pl.BlockSpec((1, D_MODEL), lambda i: (0, 0)),
                  pl.BlockSpec(memory_space=pl.ANY)],
        out_specs=pl.BlockSpec((tm, D_MODEL), lambda i: (i, 0)),
        out_shape=jax.ShapeDtypeStruct((T, D_MODEL), F32),
        scratch_shapes=[pltpu.VMEM((2, TOP_K, tm // SUBLANE, SUBLANE, D_MODEL), F32),
                        pltpu.SemaphoreType.DMA((2,))],
        compiler_params=pltpu.CompilerParams(dimension_semantics=("arbitrary",)),
        name="combine",
    )(inv3, inv3, x2, gates, final_norm, y)


def _reorder_w_in_kernel(wt_ref, o_ref):
    tk = wt_ref.shape[1]
    a = 2 * GLA_W + 2 * GLA_VW
    lat = a + GLA_RANK
    kr = lat + Q_LORA + KV_LORA
    half = QK_ROPE // 2

    def put(dst_block, pieces):
        rows = [wt_ref[start:start + n, :] for start, n in pieces]
        used = sum(n for _, n in pieces)
        if used < LANE:
            rows.append(jnp.zeros((LANE - used, tk), F32))
        block = rows[0] if len(rows) == 1 else jnp.concatenate(rows, axis=0)
        o_ref[:, dst_block * LANE:(dst_block + 1) * LANE] = block.T.astype(BF16)

    for j in range(a // LANE):
        put(j, [(j * LANE, LANE)])
    for j in range((Q_LORA + KV_LORA) // LANE):
        put(a // LANE + j, [(lat + j * LANE, LANE)])
    put(32, [(kr, QK_ROPE)])
    put(33, [(kr + half, half), (kr, half)])
    put(34, [(a, GLA_RANK)])


def _reorder_w_in(w_in, tk=256):
    in_cols = w_in.shape[1]
    return pl.pallas_call(
        _reorder_w_in_kernel,
        grid=(D_MODEL // tk,),
        in_specs=[pl.BlockSpec((in_cols, tk), lambda i: (0, i))],
        out_specs=pl.BlockSpec((tk, PROJ_W_COLS), lambda i: (i, 0)),
        out_shape=jax.ShapeDtypeStruct((D_MODEL, PROJ_W_COLS), BF16),
        compiler_params=pltpu.CompilerParams(dimension_semantics=("parallel",)),
        name="reorder_w_in",
    )(w_in.T)


def _reorder_w_uq(w_uq):
    w = w_uq.reshape(Q_LORA, MLA_HEADS, QK_NOPE + QK_ROPE)
    nope = w[:, :, :QK_NOPE].reshape(Q_LORA, MLA_W)
    pe = w[:, :, QK_NOPE:]
    half = QK_ROPE // 2
    z = jnp.zeros((Q_LORA, MLA_HEADS, LANE - QK_ROPE), w_uq.dtype)
    pe_pad = jnp.concatenate([pe, z], axis=2).reshape(Q_LORA, MLA_W)
    pe_sw = jnp.concatenate([pe[:, :, half:], pe[:, :, :half], z], axis=2).reshape(Q_LORA, MLA_W)
    return jnp.concatenate([nope, pe_pad, pe_sw], axis=1).astype(BF16)


def _reorder_w_ukv(w_ukv):
    w = w_ukv.reshape(KV_LORA, MLA_HEADS, QK_NOPE + V_DIM)
    return jnp.concatenate([w[:, :, :QK_NOPE].reshape(KV_LORA, MLA_W),
                            w[:, :, QK_NOPE:].reshape(KV_LORA, MLA_W)], axis=1).astype(BF16)


def _rope_tables(positions):
    half = QK_ROPE // 2
    inv_freq = 1.0 / (ROPE_THETA ** (jnp.arange(0, QK_ROPE, 2, dtype=F32) / QK_ROPE))
    lane = jnp.arange(LANE)
    freq = jnp.where(lane < QK_ROPE, jnp.tile(inv_freq, LANE // half), 0.0)
    sign = jnp.where(lane < half, -1.0, jnp.where(lane < QK_ROPE, 1.0, 0.0))
    ang = positions.astype(F32).reshape(-1, 1) * freq[None, :]
    return jnp.where(lane < QK_ROPE, jnp.cos(ang), 0.0), jnp.sin(ang) * sign


def _routing_plan(counts, topi, rank, n_items):
    padded = (counts + ROW_TILE - 1) // ROW_TILE * ROW_TILE
    pend = jnp.cumsum(padded)
    pstart = pend - padded
    inv = (pstart[topi] + rank).reshape(topi.shape[0], 1, TOP_K * topi.shape[2])
    nt_e = padded // ROW_TILE
    ns_e = (nt_e + TILES_PER_ITEM - 1) // TILES_PER_ITEM
    cs = jnp.cumsum(ns_e)
    j = jnp.arange(n_items, dtype=I32)
    valid = j < cs[-1]
    jj = jnp.where(valid, j, cs[-1] - 1)
    e_j = jnp.sum((jj[:, None] >= cs[None, :]).astype(I32), axis=1)
    local = jj - (cs[e_j] - ns_e[e_j])
    row0 = pstart[e_j] + local * (TILES_PER_ITEM * ROW_TILE)
    nt_j = jnp.where(valid, jnp.clip(nt_e[e_j] - local * TILES_PER_ITEM, 0, TILES_PER_ITEM), 0)
    row0_and_used = jnp.concatenate([row0, pend[-1:]])
    return pend.astype(I32), inv.astype(I32), e_j.astype(I32), row0_and_used.astype(I32), nt_j.astype(I32)


def kernel(x, positions, attn_norm, w_in, gla_gate_up, gla_gate_bias, gla_out_norm, mla_q_norm, w_uq, mla_kv_norm, w_ukv, mla_out_norm, w_out, ffn_norm, router_w, router_b, w_gate_up, b_gate_up, w_down, b_down, final_norm):
    B, S, D = x.shape
    T = B * S
    A = T * TOP_K
    n_tiles = (A + N_EXPERTS * (ROW_TILE - 1)) // ROW_TILE
    n_rows = n_tiles * ROW_TILE
    n_items = (n_tiles + N_EXPERTS * (TILES_PER_ITEM - 1)) // TILES_PER_ITEM
    row2 = lambda v: v.reshape(1, -1)

    xf = x.reshape(T, D)
    cos_t, sin_t = _rope_tables(positions)
    for l in range(w_in.shape[0]):
        gup_pad = jnp.zeros((LANE, GLA_W), F32).at[:GLA_RANK].set(gla_gate_up[l]).astype(BF16)
        proj, loga = _in_proj(xf, row2(attn_norm[l]), _reorder_w_in(w_in[l]), gup_pad, row2(gla_gate_bias[l]))
        o_gla = _gla(proj, loga, row2(gla_out_norm[l]), B, S)
        q, k, v = _mla_proj(proj, cos_t, sin_t, row2(mla_q_norm[l]), row2(mla_kv_norm[l]),
                            _reorder_w_uq(w_uq[l]), _reorder_w_ukv(w_ukv[l]))
        o_mla = _mla_attn(q, k, v, B, S)
        rw_pad = jnp.zeros((D, LANE), F32).at[:, :N_EXPERTS].set(router_w[l])
        rw_hi = rw_pad.astype(BF16)
        rw_lo = (rw_pad - rw_hi.astype(F32)).astype(BF16)
        rb_pad = jnp.zeros((1, LANE), F32).at[0, :N_EXPERTS].set(router_b[l])
        x2, h2, topi, gates, rank, counts = _out_router(
            xf, o_gla, o_mla, w_out[l].astype(BF16), row2(mla_out_norm[l]), row2(ffn_norm[l]), rw_hi, rw_lo, rb_pad)
        pend, inv, item_e, item_row0, item_nt = _routing_plan(
            counts[0, :N_EXPERTS], topi[:, :TOP_K, :], rank[:, :TOP_K, :], n_items)
        xs = _dispatch(pend, inv, h2, n_rows)
        y = _experts(item_e, item_row0, item_nt, xs, w_gate_up[l], b_gate_up[l], w_down[l], b_down[l])
        xf = _combine(inv, x2, gates, row2(final_norm), y, last_layer=(l == w_in.shape[0] - 1))
    return xf.reshape(B, S, D)
```

```python
import functools

import jax
import jax.numpy as jnp
from jax import lax
from jax.experimental import pallas as pl
from jax.experimental.pallas import tpu as pltpu

F32 = jnp.float32
BF16 = jnp.bfloat16
I32 = jnp.int32
HIGHEST = lax.Precision.HIGHEST

LANE = 128
SUBLANE = 8
V7X_VMEM_LIMIT_BYTES = 60000 * 1024

D_MODEL = 2048
GLA_HEADS, GLA_DK, GLA_DV, GLA_RANK, GLA_TAU, GLA_CHUNK = 4, 128, 256, 16, 16.0, 64
MLA_HEADS, Q_LORA, KV_LORA, QK_NOPE, QK_ROPE, V_DIM = 8, 512, 512, 128, 64, 128
ROPE_THETA = 10000.0
N_EXPERTS, TOP_K, D_FF = 32, 4, 2048
SWIGLU_LIMIT, SWIGLU_ALPHA = 7.0, 1.702
EPS = 1e-6
LOG2_E = 1.4426950408889634
GLA_W = GLA_HEADS * GLA_DK
GLA_VW = GLA_HEADS * GLA_DV
MLA_W = MLA_HEADS * V_DIM
QK_PAD = 2 * LANE

PROJ_OUT_COLS = 34 * LANE
PROJ_W_COLS = 35 * LANE

ROW_TILE = 256
TILES_PER_ITEM = 5
FF_CHUNK = 512
N_FF = D_FF // FF_CHUNK


def _rms(x, g):
    return x * lax.rsqrt(jnp.mean(x * x, axis=-1, keepdims=True) + EPS) * g


def _inproj_kernel(x_ref, g_ref, w_ref, gup_ref, gb_ref, proj_ref, loga_ref):
    h = _rms(x_ref[...], g_ref[...]).astype(BF16)
    p = jnp.dot(h, w_ref[...], preferred_element_type=F32)
    proj_ref[...] = p[:, :PROJ_OUT_COLS].astype(BF16)
    z = jnp.dot(p[:, PROJ_OUT_COLS:].astype(BF16), gup_ref[...], preferred_element_type=F32) + gb_ref[...]
    log_sig = jnp.minimum(z, 0.0) - jnp.log(1.0 + jnp.exp(-jnp.abs(z)))
    loga_ref[...] = log_sig * (1.0 / GLA_TAU)


def _in_proj(xf, g, w_r, gup_pad, gbias, tm=256):
    T = xf.shape[0]
    return pl.pallas_call(
        _inproj_kernel,
        grid=(T // tm,),
        in_specs=[pl.BlockSpec((tm, D_MODEL), lambda i: (i, 0)),
                  pl.BlockSpec((1, D_MODEL), lambda i: (0, 0)),
                  pl.BlockSpec((D_MODEL, PROJ_W_COLS), lambda i: (0, 0)),
                  pl.BlockSpec((LANE, GLA_W), lambda i: (0, 0)),
                  pl.BlockSpec((1, GLA_W), lambda i: (0, 0))],
        out_specs=[pl.BlockSpec((tm, PROJ_OUT_COLS), lambda i: (i, 0)),
                   pl.BlockSpec((tm, GLA_W), lambda i: (i, 0))],
        out_shape=[jax.ShapeDtypeStruct((T, PROJ_OUT_COLS), BF16),
                   jax.ShapeDtypeStruct((T, GLA_W), F32)],
        compiler_params=pltpu.CompilerParams(dimension_semantics=("parallel",),
                                             vmem_limit_bytes=V7X_VMEM_LIMIT_BYTES),
        name="in_proj",
    )(xf, g, w_r, gup_pad, gbias)


def _gla_kernel(q_ref, k_ref, v_ref, r_ref, la_ref, gn_ref, o_ref, st_ref, *, n_chunks):
    @pl.when(pl.program_id(1) == 0)
    def _():
        st_ref[...] = jnp.zeros_like(st_ref)

    C = GLA_CHUNK
    row = lax.broadcasted_iota(I32, (C, C), 0)
    col = lax.broadcasted_iota(I32, (C, C), 1)
    causal = col <= row
    tri = causal.astype(F32)
    for c in range(n_chunks):
        sl = pl.ds(c * C, C)
        b = jnp.dot(tri, la_ref[sl, :], precision=HIGHEST, preferred_element_type=F32)
        b_last = b[C - 1:C, :]
        q_dec = (q_ref[sl, :].astype(F32) * (GLA_DK ** -0.5) * jnp.exp(b)).astype(BF16)
        kf = k_ref[sl, :].astype(F32)
        k_intra = (kf * jnp.exp(-b)).astype(BF16)
        k_state = (kf * jnp.exp(b_last - b)).astype(BF16)
        decay = jnp.exp(b_last)
        for h in range(GLA_HEADS):
            hs = slice(h * GLA_DK, (h + 1) * GLA_DK)
            vs = slice(h * GLA_DV, (h + 1) * GLA_DV)
            v = v_ref[sl, vs]
            attn = lax.dot_general(q_dec[:, hs], k_intra[:, hs], (((1,), (1,)), ((), ())),
                                   preferred_element_type=F32)
            attn = jnp.where(causal, attn, 0.0).astype(BF16)
            st = st_ref[h]
            o = jnp.dot(attn, v, preferred_element_type=F32)
            o = o + lax.dot_general(q_dec[:, hs], st.astype(BF16), (((1,), (1,)), ((), ())),
                                    preferred_element_type=F32)
            kv_t = lax.dot_general(v, k_state[:, hs], (((0,), (0,)), ((), ())),
                                   preferred_element_type=F32)
            st_ref[h] = st * decay[:, hs] + kv_t
            r = r_ref[sl, vs].astype(F32)
            o_ref[sl, vs] = (_rms(o, gn_ref[...]) * (r * jax.nn.sigmoid(r))).astype(BF16)


def _gla(proj, loga, gn, batch, seq, ct=256):
    nct = seq // ct
    T = batch * seq
    row_map = lambda blk: (lambda b, c: (b * nct + c, blk))
    return pl.pallas_call(
        functools.partial(_gla_kernel, n_chunks=ct // GLA_CHUNK),
        grid=(batch, nct),
        in_specs=[pl.BlockSpec((ct, GLA_W), row_map(0)),
                  pl.BlockSpec((ct, GLA_W), row_map(1)),
                  pl.BlockSpec((ct, GLA_VW), row_map(1)),
                  pl.BlockSpec((ct, GLA_VW), row_map(2)),
                  pl.BlockSpec((ct, GLA_W), row_map(0)),
                  pl.BlockSpec((1, GLA_DV), lambda b, c: (0, 0))],
        out_specs=pl.BlockSpec((ct, GLA_VW), row_map(0)),
        out_shape=jax.ShapeDtypeStruct((T, GLA_VW), BF16),
        scratch_shapes=[pltpu.VMEM((GLA_HEADS, GLA_DV, GLA_DK), F32)],
        compiler_params=pltpu.CompilerParams(dimension_semantics=("parallel", "arbitrary")),
        name="gla",
    )(proj, proj, proj, proj, loga, gn)


def _mla_proj_kernel(ql_ref, kvl_ref, kr_ref, krs_ref, cos_ref, sin_ref, qn_ref, kvn_ref,
                     wq_ref, wkv_ref, q_ref, k_ref, vt_ref):
    cos = cos_ref[...]
    sin = sin_ref[...]
    scale = (QK_NOPE + QK_ROPE) ** -0.5 * LOG2_E
    qlat = _rms(ql_ref[...].astype(F32), qn_ref[...]).astype(BF16)
    qall = jnp.dot(qlat, wq_ref[...], preferred_element_type=F32)
    kvlat = _rms(kvl_ref[...].astype(F32), kvn_ref[...]).astype(BF16)
    kv = jnp.dot(kvlat, wkv_ref[...], preferred_element_type=F32)
    k_pe = (kr_ref[...].astype(F32) * cos + krs_ref[...].astype(F32) * sin).astype(BF16)
    for h in range(MLA_HEADS):
        hl = slice(h * LANE, (h + 1) * LANE)
        pe = slice(MLA_W + h * LANE, MLA_W + (h + 1) * LANE)
        ps = slice(2 * MLA_W + h * LANE, 2 * MLA_W + (h + 1) * LANE)
        q_ref[h, :, 0:LANE] = (qall[:, hl] * scale).astype(BF16)
        q_ref[h, :, LANE:QK_PAD] = ((qall[:, pe] * cos + qall[:, ps] * sin) * scale).astype(BF16)
        k_ref[h, :, 0:LANE] = kv[:, hl].astype(BF16)
        k_ref[h, :, LANE:QK_PAD] = k_pe
        vt_ref[h] = kv[:, MLA_W + h * LANE:MLA_W + (h + 1) * LANE].T.astype(BF16)


def _mla_proj(proj, cos_t, sin_t, qn, kvn, wq, wkv, tm=512):
    T = proj.shape[0]
    H = MLA_HEADS
    return pl.pallas_call(
        _mla_proj_kernel,
        grid=(T // tm,),
        in_specs=[pl.BlockSpec((tm, Q_LORA), lambda i: (i, 6)),
                  pl.BlockSpec((tm, KV_LORA), lambda i: (i, 7)),
                  pl.BlockSpec((tm, LANE), lambda i: (i, 32)),
                  pl.BlockSpec((tm, LANE), lambda i: (i, 33)),
                  pl.BlockSpec((tm, LANE), lambda i: (i, 0)),
                  pl.BlockSpec((tm, LANE), lambda i: (i, 0)),
                  pl.BlockSpec((1, Q_LORA), lambda i: (0, 0)),
                  pl.BlockSpec((1, KV_LORA), lambda i: (0, 0)),
                  pl.BlockSpec((Q_LORA, 3 * MLA_W), lambda i: (0, 0)),
                  pl.BlockSpec((KV_LORA, 2 * MLA_W), lambda i: (0, 0))],
        out_specs=[pl.BlockSpec((H, tm, QK_PAD), lambda i: (0, i, 0)),
                   pl.BlockSpec((H, tm, QK_PAD), lambda i: (0, i, 0)),
                   pl.BlockSpec((H, V_DIM, tm), lambda i: (0, 0, i))],
        out_shape=[jax.ShapeDtypeStruct((H, T, QK_PAD), BF16),
                   jax.ShapeDtypeStruct((H, T, QK_PAD), BF16),
                   jax.ShapeDtypeStruct((H, V_DIM, T), BF16)],
        compiler_params=pltpu.CompilerParams(dimension_semantics=("parallel",)),
        name="mla_proj",
    )(proj, proj, proj, proj, cos_t, sin_t, qn, kvn, wq, wkv)


def _attn_kernel(q_ref, k_ref, vt_ref, o_ref, s0_ref, s1_ref, p0_ref, p1_ref, a0_ref, a1_ref,
                 m_ref, l_ref, acc_ref, *, tq):
    s_buf, p_buf, a_buf = (s0_ref, s1_ref), (p0_ref, p1_ref), (a0_ref, a1_ref)

    def chunk(c):
        return pl.ds(pl.multiple_of(c * tq, tq), tq)

    def query_tile(qi, carry):
        q = q_ref[0, chunk(qi), :]
        m_ref[...] = jnp.full_like(m_ref, -jnp.inf)
        l_ref[...] = jnp.zeros_like(l_ref)
        acc_ref[...] = jnp.zeros_like(acc_ref)

        def score(c, slot):
            s_buf[slot][...] = lax.dot_general(k_ref[0, chunk(c), :], q, (((1,), (1,)), ((), ())),
                                               preferred_element_type=F32)

        def softmax(slot, diagonal):
            s_t = s_buf[slot][...]
            if diagonal:
                key = lax.broadcasted_iota(I32, (tq, tq), 0)
                qry = lax.broadcasted_iota(I32, (tq, tq), 1)
                s_t = jnp.where(key <= qry, s_t, -jnp.inf)
            m_old = m_ref[...]
            m_new = jnp.maximum(m_old, jnp.max(s_t, axis=0, keepdims=True))
            p_t = jnp.exp2(s_t - m_new)
            alpha = jnp.exp2(m_old - m_new)
            l_ref[...] = alpha * l_ref[...] + jnp.sum(p_t, axis=0, keepdims=True)
            m_ref[...] = m_new
            a_buf[slot][...] = alpha
            p_buf[slot][...] = p_t.astype(BF16)

        def value(c, slot):
            acc_ref[...] = a_buf[slot][...] * acc_ref[...] + jnp.dot(vt_ref[0, :, chunk(c)], p_buf[slot][...],
                                                                    preferred_element_type=F32)

        def stages(c, slot):
            value(c, slot)
            softmax(1 - slot, False)
            score(c + 2, slot)

        score(0, 0)

        @pl.when(qi == 0)
        def _():
            softmax(0, True)
            value(0, 0)

        @pl.when(qi > 0)
        def _():
            score(1, 1)
            softmax(0, False)
            n_full = qi - 1

            def pair(j, c):
                stages(2 * j, 0)
                stages(2 * j + 1, 1)
                return c

            lax.fori_loop(0, n_full // 2, pair, 0)

            @pl.when(n_full % 2 == 1)
            def _():
                stages(qi - 2, 0)

            for parity in (0, 1):
                @pl.when(qi % 2 == parity)
                def _():
                    value(qi - 1, 1 - parity)
                    softmax(parity, True)
                    value(qi, parity)

        o_ref[chunk(qi), :] = (acc_ref[...] / l_ref[...]).T
        return carry

    lax.fori_loop(0, q_ref.shape[1] // tq, query_tile, 0)


def _mla_attn(q, k, vt, batch, seq, tq=512):
    H, T, _ = q.shape
    return pl.pallas_call(
        functools.partial(_attn_kernel, tq=tq),
        grid=(batch, H),
        in_specs=[pl.BlockSpec((1, seq, QK_PAD), lambda b, h: (h, b, 0)),
                  pl.BlockSpec((1, seq, QK_PAD), lambda b, h: (h, b, 0)),
                  pl.BlockSpec((1, V_DIM, seq), lambda b, h: (h, 0, b))],
        out_specs=pl.BlockSpec((seq, V_DIM), lambda b, h: (b, h)),
        out_shape=jax.ShapeDtypeStruct((T, MLA_W), F32),
        scratch_shapes=[pltpu.VMEM((tq, tq), F32), pltpu.VMEM((tq, tq), F32),
                        pltpu.VMEM((tq, tq), BF16), pltpu.VMEM((tq, tq), BF16),
                        pltpu.VMEM((1, tq), F32), pltpu.VMEM((1, tq), F32),
                        pltpu.VMEM((1, tq), F32), pltpu.VMEM((1, tq), F32), pltpu.VMEM((V_DIM, tq), F32)],
        compiler_params=pltpu.CompilerParams(dimension_semantics=("parallel", "parallel")),
        name="mla_attn",
    )(q, k, vt)


def _out_router_kernel(x_ref, og_ref, om_ref, wo_ref, mn_ref, fn_ref, rwh_ref, rwl_ref, rb_ref,
                       x2_ref, h2_ref, topi_ref, gate_ref, rank_ref, cnt_ref, carry_ref, *, tm):
    @pl.when(pl.program_id(0) == 0)
    def _():
        carry_ref[...] = jnp.zeros_like(carry_ref)

    om = _rms(om_ref[...], mn_ref[...]).astype(BF16)
    mix = jnp.dot(og_ref[...], wo_ref[0:GLA_VW, :], preferred_element_type=F32)
    mix = mix + jnp.dot(om, wo_ref[GLA_VW:GLA_VW + MLA_W, :], preferred_element_type=F32)
    x2 = x_ref[...] + mix
    x2_ref[...] = x2
    h2 = _rms(x2, fn_ref[...])
    h2_ref[...] = h2
    h2_hi = h2.astype(BF16)
    h2_lo = (h2 - h2_hi.astype(F32)).astype(BF16)
    logits = (jnp.dot(h2_hi, rwh_ref[...], preferred_element_type=F32)
              + jnp.dot(h2_lo, rwh_ref[...], preferred_element_type=F32)
              + jnp.dot(h2_hi, rwl_ref[...], preferred_element_type=F32)) + rb_ref[...]
    lane = lax.broadcasted_iota(I32, (tm, LANE), 1).astype(F32)
    cur = jnp.where(lane < N_EXPERTS, logits, -jnp.inf)
    vals, idxs = [], []
    for _ in range(TOP_K):
        m = jnp.max(cur, axis=-1, keepdims=True)
        idx = jnp.min(jnp.where(cur == m, lane, float(N_EXPERTS - 1)), axis=-1, keepdims=True)
        vals.append(m)
        idxs.append(idx)
        cur = jnp.where(lane == idx, -jnp.inf, cur)
    es = [jnp.exp(v - vals[0]) for v in vals]
    den = es[0] + es[1] + es[2] + es[3]
    gates = jnp.zeros((tm, LANE), F32)
    topi = jnp.zeros((tm, LANE), F32)
    multi = jnp.zeros((tm, LANE), F32)
    for k in range(TOP_K):
        gates = jnp.where(lane == k, es[k] / den, gates)
        topi = jnp.where(lane == k, idxs[k], topi)
        multi = multi + (lane == idxs[k]).astype(F32)
    gate_ref[...] = gates
    topi_ref[0] = topi.T[0:SUBLANE, :].astype(I32)
    r = lax.broadcasted_iota(I32, (tm, tm), 0)
    c = lax.broadcasted_iota(I32, (tm, tm), 1)
    strict = (c < r).astype(BF16)
    pref = jnp.dot(strict, multi.astype(BF16), preferred_element_type=F32) + carry_ref[...]
    rank = jnp.zeros((tm, LANE), F32)
    for k in range(TOP_K):
        rk = jnp.sum(jnp.where(lane == idxs[k], pref, 0.0), axis=-1, keepdims=True)
        rank = jnp.where(lane == k, rk, rank)
    rank_ref[0] = rank.T[0:SUBLANE, :].astype(I32)
    carry_ref[...] = carry_ref[...] + jnp.sum(multi, axis=0, keepdims=True)
    cnt_ref[...] = carry_ref[...].astype(I32)


def _out_router(xf, o_gla, o_mla, w_out, mn, fn, rw_hi, rw_lo, rb_pad, tm=256):
    T = xf.shape[0]
    row = lambda i: (i, 0)
    fixed = lambda i: (0, 0)
    return pl.pallas_call(
        functools.partial(_out_router_kernel, tm=tm),
        grid=(T // tm,),
        in_specs=[pl.BlockSpec((tm, D_MODEL), row),
                  pl.BlockSpec((tm, GLA_VW), row),
                  pl.BlockSpec((tm, MLA_W), row),
                  pl.BlockSpec((GLA_VW + MLA_W, D_MODEL), fixed),
                  pl.BlockSpec((1, MLA_W), fixed),
                  pl.BlockSpec((1, D_MODEL), fixed),
                  pl.BlockSpec((D_MODEL, LANE), fixed),
                  pl.BlockSpec((D_MODEL, LANE), fixed),
                  pl.BlockSpec((1, LANE), fixed)],
        out_specs=[pl.BlockSpec((tm, D_MODEL), row),
                   pl.BlockSpec((tm, D_MODEL), row),
                   pl.BlockSpec((1, SUBLANE, tm), lambda i: (i, 0, 0)),
                   pl.BlockSpec((tm, LANE), row),
                   pl.BlockSpec((1, SUBLANE, tm), lambda i: (i, 0, 0)),
                   pl.BlockSpec((1, LANE), fixed)],
        out_shape=[jax.ShapeDtypeStruct((T, D_MODEL), F32),
                   jax.ShapeDtypeStruct((T, D_MODEL), F32),
                   jax.ShapeDtypeStruct((T // tm, SUBLANE, tm), I32),
                   jax.ShapeDtypeStruct((T, LANE), F32),
                   jax.ShapeDtypeStruct((T // tm, SUBLANE, tm), I32),
                   jax.ShapeDtypeStruct((1, LANE), I32)],
        scratch_shapes=[pltpu.VMEM((1, LANE), F32)],
        compiler_params=pltpu.CompilerParams(dimension_semantics=("arbitrary",),
                                             vmem_limit_bytes=V7X_VMEM_LIMIT_BYTES),
        name="out_router",
    )(xf, o_gla, o_mla, w_out, mn, fn, rw_hi, rw_lo, rb_pad)


N_DISPATCH_SLOTS = 3


def _dispatch_kernel(pend_ref, inv_ref, h_hbm, xs_ref, hbuf, zero_ref, sem_in, sem_out, sem_fill, *, tm, n_tiles):
    i = pl.program_id(0)
    n_steps = pl.num_programs(0)

    groups = tm // SUBLANE

    def load(tile, s):
        return pltpu.make_async_copy(h_hbm.at[pl.ds(tile * groups, groups)], hbuf.at[s], sem_in.at[s])

    def row_copy(g, j, dst_row, s):
        return pltpu.make_async_copy(hbuf.at[s, g, pl.ds(j, 1), :], xs_ref.at[pl.ds(dst_row, 1), :], sem_out.at[s])

    def drain(s):
        def body(g, c):
            for j in range(SUBLANE):
                for k in range(TOP_K):
                    row_copy(g, j, 0, s).wait()
            return c

        lax.fori_loop(0, groups, body, 0)

    @pl.when(i == 0)
    def _():
        load(0, 0).start()
        zero_ref[...] = jnp.zeros_like(zero_ref)

        def fill_copy(t):
            return pltpu.make_async_copy(zero_ref, xs_ref.at[pl.ds(pl.multiple_of(t * ROW_TILE, ROW_TILE), ROW_TILE), :],
                                         sem_fill)

        def fill(e, wait):
            prev = jnp.where(e == 0, 0, pend_ref[jnp.maximum(e - 1, 0)])

            @pl.when(pend_ref[e] > prev)
            def _():
                cp = fill_copy(pend_ref[e] // ROW_TILE - 1)
                if wait:
                    cp.wait()
                else:
                    cp.start()

        def tail(t, wait):
            if wait:
                fill_copy(t).wait()
            else:
                fill_copy(t).start()

        first_unused = pend_ref[N_EXPERTS - 1] // ROW_TILE
        lax.fori_loop(0, N_EXPERTS, lambda e, c: (fill(e, False), c)[1], 0)
        lax.fori_loop(first_unused, n_tiles, lambda t, c: (tail(t, False), c)[1], 0)
        lax.fori_loop(0, N_EXPERTS, lambda e, c: (fill(e, True), c)[1], 0)
        lax.fori_loop(first_unused, n_tiles, lambda t, c: (tail(t, True), c)[1], 0)

    slot = i % N_DISPATCH_SLOTS
    next_slot = (i + 1) % N_DISPATCH_SLOTS

    @pl.when(i >= 2)
    def _():
        drain(next_slot)

    @pl.when(i + 1 < n_steps)
    def _():
        load(i + 1, next_slot).start()

    load(i, slot).wait()

    def issue(g, c):
        for j in range(SUBLANE):
            for k in range(TOP_K):
                dst_row = inv_ref[0, 0, k * tm + g * SUBLANE + j]
                row_copy(g, j, dst_row, slot).start(priority=k % 2)
        return c

    lax.fori_loop(0, groups, issue, 0)

    @pl.when(i == n_steps - 1)
    def _():
        @pl.when(i >= 1)
        def _():
            drain((i + 2) % N_DISPATCH_SLOTS)

        drain(slot)


def _dispatch(pend, inv3, h2, n_rows):
    T = h2.shape[0]
    tm = inv3.shape[2] // TOP_K
    return pl.pallas_call(
        functools.partial(_dispatch_kernel, tm=tm, n_tiles=n_rows // ROW_TILE),
        grid=(T // tm,),
        in_specs=[pl.BlockSpec(memory_space=pltpu.SMEM),
                  pl.BlockSpec((1, 1, tm * TOP_K), lambda i: (i, 0, 0), memory_space=pltpu.SMEM),
                  pl.BlockSpec(memory_space=pl.ANY)],
        out_specs=pl.BlockSpec(memory_space=pl.ANY),
        out_shape=jax.ShapeDtypeStruct((n_rows, D_MODEL), F32),
        scratch_shapes=[pltpu.VMEM((N_DISPATCH_SLOTS, tm // SUBLANE, SUBLANE, D_MODEL), F32),
                        pltpu.VMEM((ROW_TILE, D_MODEL), F32),
                        pltpu.SemaphoreType.DMA((N_DISPATCH_SLOTS,)),
                        pltpu.SemaphoreType.DMA((N_DISPATCH_SLOTS,)),
                        pltpu.SemaphoreType.DMA(())],
        compiler_params=pltpu.CompilerParams(dimension_semantics=("arbitrary",)),
        name="dispatch",
    )(pend, inv3, h2.reshape(T // SUBLANE, SUBLANE, D_MODEL))


def _experts_kernel(ie_ref, row0_ref, nt_ref, xs_ref, wgu_hbm, wdn_hbm, bgu_ref, bd_ref,
                    y_ref, x_stage, x_bf, acc, wg_buf, wu_buf, wd_buf, sem_x, sem_y, sem_w):
    i = pl.program_id(0)
    n_items = nt_ref.shape[0]
    nt = nt_ref[i]
    row0 = row0_ref[i]
    expert = ie_ref[i]
    nt_prev = jnp.where(i > 0, nt_ref[jnp.maximum(i - 1, 0)], 0)
    nxt = jnp.minimum(i + 1, n_items - 1)
    nt_next = jnp.where(i + 1 < n_items, nt_ref[nxt], 0)
    row0_next = row0_ref[nxt]
    expert_next = ie_ref[nxt]

    def rows_of(t):
        return pl.ds(pl.multiple_of(t * ROW_TILE, ROW_TILE), ROW_TILE)

    def x_copy(first_row, t):
        return pltpu.make_async_copy(xs_ref.at[pl.ds(pl.multiple_of(first_row + t * ROW_TILE, ROW_TILE), ROW_TILE), :],
                                     x_stage.at[rows_of(t), :], sem_x)

    def y_copy(t, dst_tile):
        return pltpu.make_async_copy(acc.at[rows_of(t), :], y_ref.at[rows_of(dst_tile), :], sem_y.at[t])

    def w_copies(e, f):
        s = f % 2
        cols = pl.ds(f * FF_CHUNK, FF_CHUNK)
        return (pltpu.make_async_copy(wgu_hbm.at[e, :, cols], wg_buf.at[s], sem_w.at[s]),
                pltpu.make_async_copy(wgu_hbm.at[e, :, pl.ds(D_FF + f * FF_CHUNK, FF_CHUNK)], wu_buf.at[s], sem_w.at[s]),
                pltpu.make_async_copy(wdn_hbm.at[e, cols, :], wd_buf.at[s], sem_w.at[s]))

    def loop(lo, hi, fn):
        lax.fori_loop(lo, hi, lambda t, c: (fn(t), c)[1], 0)

    @pl.when(i == 0)
    def _():
        acc[pl.ds(0, ROW_TILE), :] = jnp.zeros((ROW_TILE, D_MODEL), F32)
        first_unused = row0_ref[row0_ref.shape[0] - 1] // ROW_TILE
        n_tiles = y_ref.shape[0] // ROW_TILE
        loop(first_unused, n_tiles, lambda t: y_copy(0, t).start())
        loop(first_unused, n_tiles, lambda t: y_copy(0, t).wait())

    def tile(t, width, f):
        first, last, s = f == 0, f == N_FF - 1, f % 2
        if first:
            for j in range(width):
                @pl.when(t + j < nt_prev)
                def _():
                    y_copy(t + j, 0).wait()
        rows = pl.ds(pl.multiple_of(t * ROW_TILE, ROW_TILE), width * ROW_TILE)
        x = x_bf[rows, :]
        b_gate = bgu_ref[0, :, f * FF_CHUNK:(f + 1) * FF_CHUNK]
        b_up = bgu_ref[0, :, D_FF + f * FF_CHUNK:D_FF + (f + 1) * FF_CHUNK]
        gate = jnp.dot(x, wg_buf[s].astype(BF16), preferred_element_type=F32) + b_gate
        up = jnp.dot(x, wu_buf[s].astype(BF16), preferred_element_type=F32) + b_up
        gate = jnp.minimum(gate, SWIGLU_LIMIT)
        up = jnp.clip(up, -SWIGLU_LIMIT, SWIGLU_LIMIT)
        act = (up + 1.0) * (gate * jax.nn.sigmoid(gate * SWIGLU_ALPHA))
        contrib = jnp.dot(act.astype(BF16), wd_buf[s].astype(BF16), preferred_element_type=F32)
        if first:
            acc[rows, :] = contrib + bd_ref[0]
        else:
            acc[rows, :] += contrib
        if last:
            for j in range(width):
                y_copy(t + j, row0 // ROW_TILE + t + j).start()

    def all_tiles(f):
        loop(0, nt // 2, lambda p: tile(2 * p, 2, f))

        @pl.when(nt % 2 == 1)
        def _():
            tile(nt - 1, 1, f)

    @pl.when(nt > 0)
    def _():
        @pl.when(i == 0)
        def _():
            for n, cp in enumerate(w_copies(expert, 0)):
                cp.start(priority=n % 2)
            loop(0, nt, lambda t: x_copy(row0, t).start())

        loop(0, nt, lambda t: x_copy(row0, t).wait())

        def convert(t):
            x_bf[rows_of(t), :] = x_stage[rows_of(t), :].astype(BF16)

        loop(0, nt, convert)

        def start_weights(e, f):
            for n, cp in enumerate(w_copies(e, f)):
                cp.start(priority=n % 2)

        for f in range(N_FF):
            if f + 1 < N_FF:
                start_weights(expert, f + 1)
            else:
                @pl.when(nt_next > 0)
                def _():
                    start_weights(expert_next, 0)
            if f == 0:
                loop(0, nt_next, lambda t: x_copy(row0_next, t).start())
            for cp in w_copies(expert, f):
                cp.wait()
            all_tiles(f)
            if f == 0:
                loop(nt, nt_prev, lambda t: y_copy(t, 0).wait())

        @pl.when(nt_next == 0)
        def _():
            loop(0, nt, lambda t: y_copy(t, 0).wait())


def _experts(item_e, item_row0, item_nt, xs, w_gu, b_gu, w_dn, b_dn):
    n_items = item_e.shape[0]
    n_rows = xs.shape[0]
    E = N_EXPERTS
    item_rows = TILES_PER_ITEM * ROW_TILE
    grid_spec = pltpu.PrefetchScalarGridSpec(
        num_scalar_prefetch=3,
        grid=(n_items,),
        in_specs=[pl.BlockSpec(memory_space=pl.ANY),
                  pl.BlockSpec(memory_space=pl.ANY),
                  pl.BlockSpec(memory_space=pl.ANY),
                  pl.BlockSpec((1, 1, 2 * D_FF), lambda i, ie, r0, nt: (ie[i], 0, 0)),
                  pl.BlockSpec((1, 1, D_MODEL), lambda i, ie, r0, nt: (ie[i], 0, 0))],
        out_specs=pl.BlockSpec(memory_space=pl.ANY),
        scratch_shapes=[pltpu.VMEM((item_rows, D_MODEL), F32),
                        pltpu.VMEM((item_rows, D_MODEL), BF16),
                        pltpu.VMEM((item_rows, D_MODEL), F32),
                        pltpu.VMEM((2, D_MODEL, FF_CHUNK), F32),
                        pltpu.VMEM((2, D_MODEL, FF_CHUNK), F32),
                        pltpu.VMEM((2, FF_CHUNK, D_MODEL), F32),
                        pltpu.SemaphoreType.DMA(()),
                        pltpu.SemaphoreType.DMA((TILES_PER_ITEM,)),
                        pltpu.SemaphoreType.DMA((2,))],
    )
    return pl.pallas_call(
        _experts_kernel,
        grid_spec=grid_spec,
        out_shape=jax.ShapeDtypeStruct((n_rows, D_MODEL), F32),
        compiler_params=pltpu.CompilerParams(dimension_semantics=("arbitrary",),
                                             vmem_limit_bytes=V7X_VMEM_LIMIT_BYTES),
        name="experts",
    )(item_e, item_row0, item_nt, xs, w_gu, w_dn, b_gu.reshape(E, 1, 2 * D_FF), b_dn.reshape(E, 1, D_MODEL))


def _combine_kernel(inv_ref, inv_next_ref, x2_ref, gate_ref, fn_ref, y_ref, o_ref, rows, sem, *,
                    tm, route_tile, last_layer):
    i = pl.program_id(0)
    slot = i % 2
    per_block = route_tile // tm

    groups = tm // SUBLANE

    def row_copy(idx_ref, first, g, j, k, s):
        src_row = idx_ref[0, 0, k * route_tile + first + g * SUBLANE + j]
        return pltpu.make_async_copy(y_ref.at[pl.ds(src_row, 1), :], rows.at[s, k, g, pl.ds(j, 1), :], sem.at[s])

    def gather(idx_ref, step, s):
        first = (step % per_block) * tm

        def issue(g, c):
            for j in range(SUBLANE):
                for k in range(TOP_K):
                    row_copy(idx_ref, first, g, j, k, s).start(priority=k % 2)
            return c

        lax.fori_loop(0, groups, issue, 0)

    @pl.when(i == 0)
    def _():
        gather(inv_ref, 0, 0)

    @pl.when(i + 1 < pl.num_programs(0))
    def _():
        gather(inv_next_ref, i + 1, 1 - slot)

    def drain(g, c):
        for j in range(SUBLANE):
            for k in range(TOP_K):
                row_copy(inv_ref, 0, g, j, k, slot).wait()
        return c

    lax.fori_loop(0, groups, drain, 0)
    gate = gate_ref[...]
    out = x2_ref[...]
    for k in range(TOP_K):
        out = out + gate[:, k:k + 1] * rows[slot, k].reshape(tm, D_MODEL)
    o_ref[...] = _rms(out, fn_ref[...]) if last_layer else out


def _combine(inv3, x2, gates, final_norm, y, last_layer, tm=128):
    T = x2.shape[0]
    route_tile = inv3.shape[2] // TOP_K
    per_block = route_tile // tm
    idx_block = (1, 1, TOP_K * route_tile)
    return pl.pallas_call(
        functools.partial(_combine_kernel, tm=tm, route_tile=route_tile, last_layer=last_layer),
        grid=(T // tm,),
        in_specs=[pl.BlockSpec(idx_block, lambda i: (i // per_block, 0, 0), memory_space=pltpu.SMEM),
                  pl.BlockSpec(idx_block, lambda i: (jnp.minimum(i + 1, T // tm - 1) // per_block, 0, 0),
                               memory_space=pltpu.SMEM),
                  pl.BlockSpec((tm, D_MODEL), lambda i: (i, 0)),
                  pl.BlockSpec((tm, LANE), lambda i: (i, 0)),
                  pl.BlockSpec((1, D_MODEL), lambda i: (0, 0)),
                  pl.BlockSpec(memory_space=pl.ANY)],
        out_specs=pl.BlockSpec((tm, D_MODEL), lambda i: (i, 0)),
        out_shape=jax.ShapeDtypeStruct((T, D_MODEL), F32),
        scratch_shapes=[pltpu.VMEM((2, TOP_K, tm // SUBLANE, SUBLANE, D_MODEL), F32),
                        pltpu.SemaphoreType.DMA((2,))],
        compiler_params=pltpu.CompilerParams(dimension_semantics=("arbitrary",)),
        name="combine",
    )(inv3, inv3, x2, gates, final_norm, y)


def _reorder_w_in_kernel(wt_ref, o_ref):
    tk = wt_ref.shape[1]
    a = 2 * GLA_W + 2 * GLA_VW
    lat = a + GLA_RANK
    kr = lat + Q_LORA + KV_LORA
    half = QK_ROPE // 2

    def put(dst_block, pieces):
        rows = [wt_ref[start:start + n, :] for start, n in pieces]
        used = sum(n for _, n in pieces)
        if used < LANE:
            rows.append(jnp.zeros((LANE - used, tk), F32))
        block = rows[0] if len(rows) == 1 else jnp.concatenate(rows, axis=0)
        o_ref[:, dst_block * LANE:(dst_block + 1) * LANE] = block.T.astype(BF16)

    for j in range(a // LANE):
        put(j, [(j * LANE, LANE)])
    for j in range((Q_LORA + KV_LORA) // LANE):
        put(a // LANE + j, [(lat + j * LANE, LANE)])
    put(32, [(kr, QK_ROPE)])
    put(33, [(kr + half, half), (kr, half)])
    put(34, [(a, GLA_RANK)])


def _reorder_w_in(w_in, tk=256):
    in_cols = w_in.shape[1]
    return pl.pallas_call(
        _reorder_w_in_kernel,
        grid=(D_MODEL // tk,),
        in_specs=[pl.BlockSpec((in_cols, tk), lambda i: (0, i))],
        out_specs=pl.BlockSpec((tk, PROJ_W_COLS), lambda i: (i, 0)),
        out_shape=jax.ShapeDtypeStruct((D_MODEL, PROJ_W_COLS), BF16),
        compiler_params=pltpu.CompilerParams(dimension_semantics=("parallel",)),
        name="reorder_w_in",
    )(w_in.T)


def _reorder_w_uq(w_uq):
    w = w_uq.reshape(Q_LORA, MLA_HEADS, QK_NOPE + QK_ROPE)
    nope = w[:, :, :QK_NOPE].reshape(Q_LORA, MLA_W)
    pe = w[:, :, QK_NOPE:]
    half = QK_ROPE // 2
    z = jnp.zeros((Q_LORA, MLA_HEADS, LANE - QK_ROPE), w_uq.dtype)
    pe_pad = jnp.concatenate([pe, z], axis=2).reshape(Q_LORA, MLA_W)
    pe_sw = jnp.concatenate([pe[:, :, half:], pe[:, :, :half], z], axis=2).reshape(Q_LORA, MLA_W)
    return jnp.concatenate([nope, pe_pad, pe_sw], axis=1).astype(BF16)


def _reorder_w_ukv(w_ukv):
    w = w_ukv.reshape(KV_LORA, MLA_HEADS, QK_NOPE + V_DIM)
    return jnp.concatenate([w[:, :, :QK_NOPE].reshape(KV_LORA, MLA_W),
                            w[:, :, QK_NOPE:].reshape(KV_LORA, MLA_W)], axis=1).astype(BF16)


def _rope_tables(positions):
    half = QK_ROPE // 2
    inv_freq = 1.0 / (ROPE_THETA ** (jnp.arange(0, QK_ROPE, 2, dtype=F32) / QK_ROPE))
    lane = jnp.arange(LANE)
    freq = jnp.where(lane < QK_ROPE, jnp.tile(inv_freq, LANE // half), 0.0)
    sign = jnp.where(lane < half, -1.0, jnp.where(lane < QK_ROPE, 1.0, 0.0))
    ang = positions.astype(F32).reshape(-1, 1) * freq[None, :]
    return jnp.where(lane < QK_ROPE, jnp.cos(ang), 0.0), jnp.sin(ang) * sign


def _routing_plan(counts, topi, rank, n_items):
    padded = (counts + ROW_TILE - 1) // ROW_TILE * ROW_TILE
    pend = jnp.cumsum(padded)
    pstart = pend - padded
    experts = jnp.arange(N_EXPERTS, dtype=topi.dtype)
    seg_start = jnp.sum(jnp.where(topi[..., None] == experts, pstart.astype(topi.dtype), 0), axis=-1)
    inv = (seg_start + rank).reshape(topi.shape[0], 1, TOP_K * topi.shape[2])
    nt_e = padded // ROW_TILE
    ns_e = (nt_e + TILES_PER_ITEM - 1) // TILES_PER_ITEM
    cs = jnp.cumsum(ns_e)
    j = jnp.arange(n_items, dtype=I32)
    valid = j < cs[-1]
    jj = jnp.where(valid, j, cs[-1] - 1)
    e_j = jnp.sum((jj[:, None] >= cs[None, :]).astype(I32), axis=1)
    local = jj - (cs[e_j] - ns_e[e_j])
    row0 = pstart[e_j] + local * (TILES_PER_ITEM * ROW_TILE)
    nt_j = jnp.where(valid, jnp.clip(nt_e[e_j] - local * TILES_PER_ITEM, 0, TILES_PER_ITEM), 0)
    row0_and_used = jnp.concatenate([row0, pend[-1:]])
    return pend.astype(I32), inv.astype(I32), e_j.astype(I32), row0_and_used.astype(I32), nt_j.astype(I32)


def kernel(x, positions, attn_norm, w_in, gla_gate_up, gla_gate_bias, gla_out_norm, mla_q_norm, w_uq, mla_kv_norm, w_ukv, mla_out_norm, w_out, ffn_norm, router_w, router_b, w_gate_up, b_gate_up, w_down, b_down, final_norm):
    B, S, D = x.shape
    T = B * S
    A = T * TOP_K
    n_tiles = (A + N_EXPERTS * (ROW_TILE - 1)) // ROW_TILE
    n_rows = n_tiles * ROW_TILE
    n_items = (n_tiles + N_EXPERTS * (TILES_PER_ITEM - 1)) // TILES_PER_ITEM
    row2 = lambda v: v.reshape(1, -1)

    xf = x.reshape(T, D)
    cos_t, sin_t = _rope_tables(positions)
    for l in range(w_in.shape[0]):
        gup_pad = jnp.zeros((LANE, GLA_W), F32).at[:GLA_RANK].set(gla_gate_up[l]).astype(BF16)
        proj, loga = _in_proj(xf, row2(attn_norm[l]), _reorder_w_in(w_in[l]), gup_pad, row2(gla_gate_bias[l]))
        o_gla = _gla(proj, loga, row2(gla_out_norm[l]), B, S)
        q, k, v = _mla_proj(proj, cos_t, sin_t, row2(mla_q_norm[l]), row2(mla_kv_norm[l]),
                            _reorder_w_uq(w_uq[l]), _reorder_w_ukv(w_ukv[l]))
        o_mla = _mla_attn(q, k, v, B, S)
        rw_pad = jnp.zeros((D, LANE), F32).at[:, :N_EXPERTS].set(router_w[l])
        rw_hi = rw_pad.astype(BF16)
        rw_lo = (rw_pad - rw_hi.astype(F32)).astype(BF16)
        rb_pad = jnp.zeros((1, LANE), F32).at[0, :N_EXPERTS].set(router_b[l])
        x2, h2, topi, gates, rank, counts = _out_router(
            xf, o_gla, o_mla, w_out[l].astype(BF16), row2(mla_out_norm[l]), row2(ffn_norm[l]), rw_hi, rw_lo, rb_pad)
        pend, inv, item_e, item_row0, item_nt = _routing_plan(
            counts[0, :N_EXPERTS], topi[:, :TOP_K, :], rank[:, :TOP_K, :], n_items)
        xs = _dispatch(pend, inv, h2, n_rows)
        y = _experts(item_e, item_row0, item_nt, xs, w_gate_up[l], b_gate_up[l], w_down[l], b_down[l])
        xf = _combine(inv, x2, gates, row2(final_norm), y, last_layer=(l == w_in.shape[0] - 1))
    return xf.reshape(B, S, D)
```

```python
import functools

import jax
import jax.numpy as jnp
from jax import lax
from jax.experimental import pallas as pl
from jax.experimental.pallas import tpu as pltpu

F32 = jnp.float32
BF16 = jnp.bfloat16
I32 = jnp.int32
HIGHEST = lax.Precision.HIGHEST

LANE = 128
SUBLANE = 8
V7X_VMEM_LIMIT_BYTES = 60000 * 1024

D_MODEL = 2048
GLA_HEADS, GLA_DK, GLA_DV, GLA_RANK, GLA_TAU, GLA_CHUNK = 4, 128, 256, 16, 16.0, 64
MLA_HEADS, Q_LORA, KV_LORA, QK_NOPE, QK_ROPE, V_DIM = 8, 512, 512, 128, 64, 128
ROPE_THETA = 10000.0
N_EXPERTS, TOP_K, D_FF = 32, 4, 2048
SWIGLU_LIMIT, SWIGLU_ALPHA = 7.0, 1.702
EPS = 1e-6
LOG2_E = 1.4426950408889634
GLA_W = GLA_HEADS * GLA_DK
GLA_VW = GLA_HEADS * GLA_DV
MLA_W = MLA_HEADS * V_DIM
QK_PAD = 2 * LANE

PROJ_OUT_COLS = 34 * LANE
PROJ_W_COLS = 35 * LANE

ROW_TILE = 128
TILES_PER_ITEM = 10
MATMUL_TILES = 4
FF_CHUNK = 512
N_FF = D_FF // FF_CHUNK


def _rms(x, g):
    return x * lax.rsqrt(jnp.mean(x * x, axis=-1, keepdims=True) + EPS) * g


def _inproj_kernel(x_ref, g_ref, w_ref, gup_ref, gb_ref, proj_ref, loga_ref):
    h = _rms(x_ref[...], g_ref[...]).astype(BF16)
    p = jnp.dot(h, w_ref[...], preferred_element_type=F32)
    proj_ref[...] = p[:, :PROJ_OUT_COLS].astype(BF16)
    z = jnp.dot(p[:, PROJ_OUT_COLS:].astype(BF16), gup_ref[...], preferred_element_type=F32) + gb_ref[...]
    log_sig = jnp.minimum(z, 0.0) - jnp.log(1.0 + jnp.exp(-jnp.abs(z)))
    loga_ref[...] = log_sig * (1.0 / GLA_TAU)


def _in_proj(xf, g, w_r, gup_pad, gbias, tm=256):
    T = xf.shape[0]
    return pl.pallas_call(
        _inproj_kernel,
        grid=(T // tm,),
        in_specs=[pl.BlockSpec((tm, D_MODEL), lambda i: (i, 0)),
                  pl.BlockSpec((1, D_MODEL), lambda i: (0, 0)),
                  pl.BlockSpec((D_MODEL, PROJ_W_COLS), lambda i: (0, 0)),
                  pl.BlockSpec((LANE, GLA_W), lambda i: (0, 0)),
                  pl.BlockSpec((1, GLA_W), lambda i: (0, 0))],
        out_specs=[pl.BlockSpec((tm, PROJ_OUT_COLS), lambda i: (i, 0)),
                   pl.BlockSpec((tm, GLA_W), lambda i: (i, 0))],
        out_shape=[jax.ShapeDtypeStruct((T, PROJ_OUT_COLS), BF16),
                   jax.ShapeDtypeStruct((T, GLA_W), F32)],
        compiler_params=pltpu.CompilerParams(dimension_semantics=("parallel",),
                                             vmem_limit_bytes=V7X_VMEM_LIMIT_BYTES),
        name="in_proj",
    )(xf, g, w_r, gup_pad, gbias)


def _gla_kernel(q_ref, k_ref, v_ref, r_ref, la_ref, gn_ref, o_ref, st_ref, *, n_chunks):
    @pl.when(pl.program_id(1) == 0)
    def _():
        st_ref[...] = jnp.zeros_like(st_ref)

    C = GLA_CHUNK
    row = lax.broadcasted_iota(I32, (C, C), 0)
    col = lax.broadcasted_iota(I32, (C, C), 1)
    causal = col <= row
    tri = causal.astype(F32)
    for c in range(n_chunks):
        sl = pl.ds(c * C, C)
        b = jnp.dot(tri, la_ref[sl, :], precision=HIGHEST, preferred_element_type=F32)
        b_last = b[C - 1:C, :]
        q_dec = (q_ref[sl, :].astype(F32) * (GLA_DK ** -0.5) * jnp.exp(b)).astype(BF16)
        kf = k_ref[sl, :].astype(F32)
        k_intra = (kf * jnp.exp(-b)).astype(BF16)
        k_state = (kf * jnp.exp(b_last - b)).astype(BF16)
        decay = jnp.exp(b_last)
        for h in range(GLA_HEADS):
            hs = slice(h * GLA_DK, (h + 1) * GLA_DK)
            vs = slice(h * GLA_DV, (h + 1) * GLA_DV)
            v = v_ref[sl, vs]
            attn = lax.dot_general(q_dec[:, hs], k_intra[:, hs], (((1,), (1,)), ((), ())),
                                   preferred_element_type=F32)
            attn = jnp.where(causal, attn, 0.0).astype(BF16)
            st = st_ref[h]
            o = jnp.dot(attn, v, preferred_element_type=F32)
            o = o + lax.dot_general(q_dec[:, hs], st.astype(BF16), (((1,), (1,)), ((), ())),
                                    preferred_element_type=F32)
            kv_t = lax.dot_general(v, k_state[:, hs], (((0,), (0,)), ((), ())),
                                   preferred_element_type=F32)
            st_ref[h] = st * decay[:, hs] + kv_t
            r = r_ref[sl, vs].astype(F32)
            o_ref[sl, vs] = (_rms(o, gn_ref[...]) * (r * jax.nn.sigmoid(r))).astype(BF16)


def _gla(proj, loga, gn, batch, seq, ct=256):
    nct = seq // ct
    T = batch * seq
    row_map = lambda blk: (lambda b, c: (b * nct + c, blk))
    return pl.pallas_call(
        functools.partial(_gla_kernel, n_chunks=ct // GLA_CHUNK),
        grid=(batch, nct),
        in_specs=[pl.BlockSpec((ct, GLA_W), row_map(0)),
                  pl.BlockSpec((ct, GLA_W), row_map(1)),
                  pl.BlockSpec((ct, GLA_VW), row_map(1)),
                  pl.BlockSpec((ct, GLA_VW), row_map(2)),
                  pl.BlockSpec((ct, GLA_W), row_map(0)),
                  pl.BlockSpec((1, GLA_DV), lambda b, c: (0, 0))],
        out_specs=pl.BlockSpec((ct, GLA_VW), row_map(0)),
        out_shape=jax.ShapeDtypeStruct((T, GLA_VW), BF16),
        scratch_shapes=[pltpu.VMEM((GLA_HEADS, GLA_DV, GLA_DK), F32)],
        compiler_params=pltpu.CompilerParams(dimension_semantics=("parallel", "arbitrary")),
        name="gla",
    )(proj, proj, proj, proj, loga, gn)


def _mla_proj_kernel(ql_ref, kvl_ref, kr_ref, krs_ref, cos_ref, sin_ref, qn_ref, kvn_ref,
                     wq_ref, wkv_ref, q_ref, k_ref, vt_ref):
    cos = cos_ref[...]
    sin = sin_ref[...]
    scale = (QK_NOPE + QK_ROPE) ** -0.5 * LOG2_E
    qlat = _rms(ql_ref[...].astype(F32), qn_ref[...]).astype(BF16)
    qall = jnp.dot(qlat, wq_ref[...], preferred_element_type=F32)
    kvlat = _rms(kvl_ref[...].astype(F32), kvn_ref[...]).astype(BF16)
    kv = jnp.dot(kvlat, wkv_ref[...], preferred_element_type=F32)
    k_pe = (kr_ref[...].astype(F32) * cos + krs_ref[...].astype(F32) * sin).astype(BF16)
    for h in range(MLA_HEADS):
        hl = slice(h * LANE, (h + 1) * LANE)
        pe = slice(MLA_W + h * LANE, MLA_W + (h + 1) * LANE)
        ps = slice(2 * MLA_W + h * LANE, 2 * MLA_W + (h + 1) * LANE)
        q_ref[h, :, 0:LANE] = (qall[:, hl] * scale).astype(BF16)
        q_ref[h, :, LANE:QK_PAD] = ((qall[:, pe] * cos + qall[:, ps] * sin) * scale).astype(BF16)
        k_ref[h, :, 0:LANE] = kv[:, hl].astype(BF16)
        k_ref[h, :, LANE:QK_PAD] = k_pe
        vt_ref[h] = kv[:, MLA_W + h * LANE:MLA_W + (h + 1) * LANE].T.astype(BF16)


def _mla_proj(proj, cos_t, sin_t, qn, kvn, wq, wkv, tm=512):
    T = proj.shape[0]
    H = MLA_HEADS
    return pl.pallas_call(
        _mla_proj_kernel,
        grid=(T // tm,),
        in_specs=[pl.BlockSpec((tm, Q_LORA), lambda i: (i, 6)),
                  pl.BlockSpec((tm, KV_LORA), lambda i: (i, 7)),
                  pl.BlockSpec((tm, LANE), lambda i: (i, 32)),
                  pl.BlockSpec((tm, LANE), lambda i: (i, 33)),
                  pl.BlockSpec((tm, LANE), lambda i: (i, 0)),
                  pl.BlockSpec((tm, LANE), lambda i: (i, 0)),
                  pl.BlockSpec((1, Q_LORA), lambda i: (0, 0)),
                  pl.BlockSpec((1, KV_LORA), lambda i: (0, 0)),
                  pl.BlockSpec((Q_LORA, 3 * MLA_W), lambda i: (0, 0)),
                  pl.BlockSpec((KV_LORA, 2 * MLA_W), lambda i: (0, 0))],
        out_specs=[pl.BlockSpec((H, tm, QK_PAD), lambda i: (0, i, 0)),
                   pl.BlockSpec((H, tm, QK_PAD), lambda i: (0, i, 0)),
                   pl.BlockSpec((H, V_DIM, tm), lambda i: (0, 0, i))],
        out_shape=[jax.ShapeDtypeStruct((H, T, QK_PAD), BF16),
                   jax.ShapeDtypeStruct((H, T, QK_PAD), BF16),
                   jax.ShapeDtypeStruct((H, V_DIM, T), BF16)],
        compiler_params=pltpu.CompilerParams(dimension_semantics=("parallel",)),
        name="mla_proj",
    )(proj, proj, proj, proj, cos_t, sin_t, qn, kvn, wq, wkv)


def _attn_kernel(q_ref, k_ref, vt_ref, o_ref, s0_ref, s1_ref, p0_ref, p1_ref, a0_ref, a1_ref,
                 m_ref, l_ref, acc_ref, *, tq):
    s_buf, p_buf, a_buf = (s0_ref, s1_ref), (p0_ref, p1_ref), (a0_ref, a1_ref)

    def chunk(c):
        return pl.ds(pl.multiple_of(c * tq, tq), tq)

    def query_tile(qi, carry):
        q = q_ref[0, chunk(qi), :]
        m_ref[...] = jnp.full_like(m_ref, -jnp.inf)
        l_ref[...] = jnp.zeros_like(l_ref)
        acc_ref[...] = jnp.zeros_like(acc_ref)

        def score(c, slot):
            s_buf[slot][...] = lax.dot_general(k_ref[0, chunk(c), :], q, (((1,), (1,)), ((), ())),
                                               preferred_element_type=F32)

        def softmax(slot, diagonal):
            s_t = s_buf[slot][...]
            if diagonal:
                key = lax.broadcasted_iota(I32, (tq, tq), 0)
                qry = lax.broadcasted_iota(I32, (tq, tq), 1)
                s_t = jnp.where(key <= qry, s_t, -jnp.inf)
            m_old = m_ref[...]
            m_new = jnp.maximum(m_old, jnp.max(s_t, axis=0, keepdims=True))
            p_t = jnp.exp2(s_t - m_new)
            alpha = jnp.exp2(m_old - m_new)
            l_ref[...] = alpha * l_ref[...] + jnp.sum(p_t, axis=0, keepdims=True)
            m_ref[...] = m_new
            a_buf[slot][...] = alpha
            p_buf[slot][...] = p_t.astype(BF16)

        def value(c, slot):
            acc_ref[...] = a_buf[slot][...] * acc_ref[...] + jnp.dot(vt_ref[0, :, chunk(c)], p_buf[slot][...],
                                                                    preferred_element_type=F32)

        def stages(c, slot):
            value(c, slot)
            softmax(1 - slot, False)
            score(c + 2, slot)

        score(0, 0)

        @pl.when(qi == 0)
        def _():
            softmax(0, True)
            value(0, 0)

        @pl.when(qi > 0)
        def _():
            score(1, 1)
            softmax(0, False)
            n_full = qi - 1

            def pair(j, c):
                stages(2 * j, 0)
                stages(2 * j + 1, 1)
                return c

            lax.fori_loop(0, n_full // 2, pair, 0)

            @pl.when(n_full % 2 == 1)
            def _():
                stages(qi - 2, 0)

            for parity in (0, 1):
                @pl.when(qi % 2 == parity)
                def _():
                    value(qi - 1, 1 - parity)
                    softmax(parity, True)
                    value(qi, parity)

        o_ref[chunk(qi), :] = (acc_ref[...] / l_ref[...]).T
        return carry

    lax.fori_loop(0, q_ref.shape[1] // tq, query_tile, 0)


def _mla_attn(q, k, vt, batch, seq, tq=512):
    H, T, _ = q.shape
    return pl.pallas_call(
        functools.partial(_attn_kernel, tq=tq),
        grid=(batch, H),
        in_specs=[pl.BlockSpec((1, seq, QK_PAD), lambda b, h: (h, b, 0)),
                  pl.BlockSpec((1, seq, QK_PAD), lambda b, h: (h, b, 0)),
                  pl.BlockSpec((1, V_DIM, seq), lambda b, h: (h, 0, b))],
        out_specs=pl.BlockSpec((seq, V_DIM), lambda b, h: (b, h)),
        out_shape=jax.ShapeDtypeStruct((T, MLA_W), F32),
        scratch_shapes=[pltpu.VMEM((tq, tq), F32), pltpu.VMEM((tq, tq), F32),
                        pltpu.VMEM((tq, tq), BF16), pltpu.VMEM((tq, tq), BF16),
                        pltpu.VMEM((1, tq), F32), pltpu.VMEM((1, tq), F32),
                        pltpu.VMEM((1, tq), F32), pltpu.VMEM((1, tq), F32), pltpu.VMEM((V_DIM, tq), F32)],
        compiler_params=pltpu.CompilerParams(dimension_semantics=("parallel", "parallel")),
        name="mla_attn",
    )(q, k, vt)


def _out_router_kernel(x_ref, og_ref, om_ref, wo_ref, mn_ref, fn_ref, rwh_ref, rwl_ref, rb_ref,
                       x2_ref, h2_ref, topi_ref, gate_ref, rank_ref, cnt_ref, carry_ref, *, tm):
    @pl.when(pl.program_id(0) == 0)
    def _():
        carry_ref[...] = jnp.zeros_like(carry_ref)

    om = _rms(om_ref[...], mn_ref[...]).astype(BF16)
    mix = jnp.dot(og_ref[...], wo_ref[0:GLA_VW, :], preferred_element_type=F32)
    mix = mix + jnp.dot(om, wo_ref[GLA_VW:GLA_VW + MLA_W, :], preferred_element_type=F32)
    x2 = x_ref[...] + mix
    x2_ref[...] = x2
    h2 = _rms(x2, fn_ref[...])
    h2_ref[...] = h2
    h2_hi = h2.astype(BF16)
    h2_lo = (h2 - h2_hi.astype(F32)).astype(BF16)
    logits = (jnp.dot(h2_hi, rwh_ref[...], preferred_element_type=F32)
              + jnp.dot(h2_lo, rwh_ref[...], preferred_element_type=F32)
              + jnp.dot(h2_hi, rwl_ref[...], preferred_element_type=F32)) + rb_ref[...]
    lane = lax.broadcasted_iota(I32, (tm, LANE), 1).astype(F32)
    cur = jnp.where(lane < N_EXPERTS, logits, -jnp.inf)
    vals, idxs = [], []
    for _ in range(TOP_K):
        m = jnp.max(cur, axis=-1, keepdims=True)
        idx = jnp.min(jnp.where(cur == m, lane, float(N_EXPERTS - 1)), axis=-1, keepdims=True)
        vals.append(m)
        idxs.append(idx)
        cur = jnp.where(lane == idx, -jnp.inf, cur)
    es = [jnp.exp(v - vals[0]) for v in vals]
    den = es[0] + es[1] + es[2] + es[3]
    gates = jnp.zeros((tm, LANE), F32)
    topi = jnp.zeros((tm, LANE), F32)
    multi = jnp.zeros((tm, LANE), F32)
    for k in range(TOP_K):
        gates = jnp.where(lane == k, es[k] / den, gates)
        topi = jnp.where(lane == k, idxs[k], topi)
        multi = multi + (lane == idxs[k]).astype(F32)
    gate_ref[...] = gates
    topi_ref[0] = topi.T[0:SUBLANE, :].astype(I32)
    r = lax.broadcasted_iota(I32, (tm, tm), 0)
    c = lax.broadcasted_iota(I32, (tm, tm), 1)
    strict = (c < r).astype(BF16)
    pref = jnp.dot(strict, multi.astype(BF16), preferred_element_type=F32) + carry_ref[...]
    rank = jnp.zeros((tm, LANE), F32)
    for k in range(TOP_K):
        rk = jnp.sum(jnp.where(lane == idxs[k], pref, 0.0), axis=-1, keepdims=True)
        rank = jnp.where(lane == k, rk, rank)
    rank_ref[0] = rank.T[0:SUBLANE, :].astype(I32)
    carry_ref[...] = carry_ref[...] + jnp.sum(multi, axis=0, keepdims=True)
    cnt_ref[...] = carry_ref[...].astype(I32)


def _out_router(xf, o_gla, o_mla, w_out, mn, fn, rw_hi, rw_lo, rb_pad, tm=256):
    T = xf.shape[0]
    row = lambda i: (i, 0)
    fixed = lambda i: (0, 0)
    return pl.pallas_call(
        functools.partial(_out_router_kernel, tm=tm),
        grid=(T // tm,),
        in_specs=[pl.BlockSpec((tm, D_MODEL), row),
                  pl.BlockSpec((tm, GLA_VW), row),
                  pl.BlockSpec((tm, MLA_W), row),
                  pl.BlockSpec((GLA_VW + MLA_W, D_MODEL), fixed),
                  pl.BlockSpec((1, MLA_W), fixed),
                  pl.BlockSpec((1, D_MODEL), fixed),
                  pl.BlockSpec((D_MODEL, LANE), fixed),
                  pl.BlockSpec((D_MODEL, LANE), fixed),
                  pl.BlockSpec((1, LANE), fixed)],
        out_specs=[pl.BlockSpec((tm, D_MODEL), row),
                   pl.BlockSpec((tm, D_MODEL), row),
                   pl.BlockSpec((1, SUBLANE, tm), lambda i: (i, 0, 0)),
                   pl.BlockSpec((tm, LANE), row),
                   pl.BlockSpec((1, SUBLANE, tm), lambda i: (i, 0, 0)),
                   pl.BlockSpec((1, LANE), fixed)],
        out_shape=[jax.ShapeDtypeStruct((T, D_MODEL), F32),
                   jax.ShapeDtypeStruct((T, D_MODEL), F32),
                   jax.ShapeDtypeStruct((T // tm, SUBLANE, tm), I32),
                   jax.ShapeDtypeStruct((T, LANE), F32),
                   jax.ShapeDtypeStruct((T // tm, SUBLANE, tm), I32),
                   jax.ShapeDtypeStruct((1, LANE), I32)],
        scratch_shapes=[pltpu.VMEM((1, LANE), F32)],
        compiler_params=pltpu.CompilerParams(dimension_semantics=("arbitrary",),
                                             vmem_limit_bytes=V7X_VMEM_LIMIT_BYTES),
        name="out_router",
    )(xf, o_gla, o_mla, w_out, mn, fn, rw_hi, rw_lo, rb_pad)


N_DISPATCH_SLOTS = 3


def _dispatch_kernel(pend_ref, inv_ref, h_hbm, xs_ref, hbuf, zero_ref, sem_in, sem_out, sem_fill, *, tm, n_tiles):
    i = pl.program_id(0)
    n_steps = pl.num_programs(0)

    groups = tm // SUBLANE

    def load(tile, s):
        return pltpu.make_async_copy(h_hbm.at[pl.ds(tile * groups, groups)], hbuf.at[s], sem_in.at[s])

    def row_copy(g, j, dst_row, s):
        return pltpu.make_async_copy(hbuf.at[s, g, pl.ds(j, 1), :], xs_ref.at[pl.ds(dst_row, 1), :], sem_out.at[s])

    def drain(s):
        def body(g, c):
            for j in range(SUBLANE):
                for k in range(TOP_K):
                    row_copy(g, j, 0, s).wait()
            return c

        lax.fori_loop(0, groups, body, 0)

    @pl.when(i == 0)
    def _():
        load(0, 0).start()
        zero_ref[...] = jnp.zeros_like(zero_ref)

        def fill_copy(t):
            return pltpu.make_async_copy(zero_ref, xs_ref.at[pl.ds(pl.multiple_of(t * ROW_TILE, ROW_TILE), ROW_TILE), :],
                                         sem_fill)

        def fill(e, wait):
            prev = jnp.where(e == 0, 0, pend_ref[jnp.maximum(e - 1, 0)])

            @pl.when(pend_ref[e] > prev)
            def _():
                cp = fill_copy(pend_ref[e] // ROW_TILE - 1)
                if wait:
                    cp.wait()
                else:
                    cp.start()

        def tail(t, wait):
            if wait:
                fill_copy(t).wait()
            else:
                fill_copy(t).start()

        first_unused = pend_ref[N_EXPERTS - 1] // ROW_TILE
        lax.fori_loop(0, N_EXPERTS, lambda e, c: (fill(e, False), c)[1], 0)
        lax.fori_loop(first_unused, n_tiles, lambda t, c: (tail(t, False), c)[1], 0)
        lax.fori_loop(0, N_EXPERTS, lambda e, c: (fill(e, True), c)[1], 0)
        lax.fori_loop(first_unused, n_tiles, lambda t, c: (tail(t, True), c)[1], 0)

    slot = i % N_DISPATCH_SLOTS
    next_slot = (i + 1) % N_DISPATCH_SLOTS

    @pl.when(i >= 2)
    def _():
        drain(next_slot)

    @pl.when(i + 1 < n_steps)
    def _():
        load(i + 1, next_slot).start()

    load(i, slot).wait()

    def issue(g, c):
        for j in range(SUBLANE):
            for k in range(TOP_K):
                dst_row = inv_ref[0, 0, k * tm + g * SUBLANE + j]
                row_copy(g, j, dst_row, slot).start(priority=k % 2)
        return c

    lax.fori_loop(0, groups, issue, 0)

    @pl.when(i == n_steps - 1)
    def _():
        @pl.when(i >= 1)
        def _():
            drain((i + 2) % N_DISPATCH_SLOTS)

        drain(slot)


def _dispatch(pend, inv3, h2, n_rows):
    T = h2.shape[0]
    tm = inv3.shape[2] // TOP_K
    return pl.pallas_call(
        functools.partial(_dispatch_kernel, tm=tm, n_tiles=n_rows // ROW_TILE),
        grid=(T // tm,),
        in_specs=[pl.BlockSpec(memory_space=pltpu.SMEM),
                  pl.BlockSpec((1, 1, tm * TOP_K), lambda i: (i, 0, 0), memory_space=pltpu.SMEM),
                  pl.BlockSpec(memory_space=pl.ANY)],
        out_specs=pl.BlockSpec(memory_space=pl.ANY),
        out_shape=jax.ShapeDtypeStruct((n_rows, D_MODEL), F32),
        scratch_shapes=[pltpu.VMEM((N_DISPATCH_SLOTS, tm // SUBLANE, SUBLANE, D_MODEL), F32),
                        pltpu.VMEM((ROW_TILE, D_MODEL), F32),
                        pltpu.SemaphoreType.DMA((N_DISPATCH_SLOTS,)),
                        pltpu.SemaphoreType.DMA((N_DISPATCH_SLOTS,)),
                        pltpu.SemaphoreType.DMA(())],
        compiler_params=pltpu.CompilerParams(dimension_semantics=("arbitrary",)),
        name="dispatch",
    )(pend, inv3, h2.reshape(T // SUBLANE, SUBLANE, D_MODEL))


def _experts_kernel(ie_ref, row0_ref, nt_ref, xs_ref, wgu_hbm, wdn_hbm, bgu_ref, bd_ref,
                    y_ref, x_stage, x_bf, acc, wg_buf, wu_buf, wd_buf, sem_x, sem_y, sem_w):
    i = pl.program_id(0)
    n_items = nt_ref.shape[0]
    nt = nt_ref[i]
    row0 = row0_ref[i]
    expert = ie_ref[i]
    nt_prev = jnp.where(i > 0, nt_ref[jnp.maximum(i - 1, 0)], 0)
    nxt = jnp.minimum(i + 1, n_items - 1)
    nt_next = jnp.where(i + 1 < n_items, nt_ref[nxt], 0)
    row0_next = row0_ref[nxt]
    expert_next = ie_ref[nxt]

    def rows_of(t):
        return pl.ds(pl.multiple_of(t * ROW_TILE, ROW_TILE), ROW_TILE)

    def x_copy(first_row, t):
        return pltpu.make_async_copy(xs_ref.at[pl.ds(pl.multiple_of(first_row + t * ROW_TILE, ROW_TILE), ROW_TILE), :],
                                     x_stage.at[rows_of(t), :], sem_x)

    def y_copy(t, dst_tile):
        return pltpu.make_async_copy(acc.at[rows_of(t), :], y_ref.at[rows_of(dst_tile), :], sem_y.at[t])

    def w_copies(e, f):
        s = f % 2
        cols = pl.ds(f * FF_CHUNK, FF_CHUNK)
        return (pltpu.make_async_copy(wgu_hbm.at[e, :, cols], wg_buf.at[s], sem_w.at[s]),
                pltpu.make_async_copy(wgu_hbm.at[e, :, pl.ds(D_FF + f * FF_CHUNK, FF_CHUNK)], wu_buf.at[s], sem_w.at[s]),
                pltpu.make_async_copy(wdn_hbm.at[e, cols, :], wd_buf.at[s], sem_w.at[s]))

    def loop(lo, hi, fn):
        lax.fori_loop(lo, hi, lambda t, c: (fn(t), c)[1], 0)

    @pl.when(i == 0)
    def _():
        acc[pl.ds(0, ROW_TILE), :] = jnp.zeros((ROW_TILE, D_MODEL), F32)
        first_unused = row0_ref[row0_ref.shape[0] - 1] // ROW_TILE
        n_tiles = y_ref.shape[0] // ROW_TILE
        loop(first_unused, n_tiles, lambda t: y_copy(0, t).start())
        loop(first_unused, n_tiles, lambda t: y_copy(0, t).wait())

    def tile(t, width, f):
        first, last, s = f == 0, f == N_FF - 1, f % 2
        if first:
            for j in range(width):
                @pl.when(t + j < nt_prev)
                def _():
                    y_copy(t + j, 0).wait()
        rows = pl.ds(pl.multiple_of(t * ROW_TILE, ROW_TILE), width * ROW_TILE)
        x = x_bf[rows, :]
        b_gate = bgu_ref[0, :, f * FF_CHUNK:(f + 1) * FF_CHUNK]
        b_up = bgu_ref[0, :, D_FF + f * FF_CHUNK:D_FF + (f + 1) * FF_CHUNK]
        gate = jnp.dot(x, wg_buf[s].astype(BF16), preferred_element_type=F32) + b_gate
        up = jnp.dot(x, wu_buf[s].astype(BF16), preferred_element_type=F32) + b_up
        gate = jnp.minimum(gate, SWIGLU_LIMIT)
        up = jnp.clip(up, -SWIGLU_LIMIT, SWIGLU_LIMIT)
        act = (up + 1.0) * (gate * jax.nn.sigmoid(gate * SWIGLU_ALPHA))
        contrib = jnp.dot(act.astype(BF16), wd_buf[s].astype(BF16), preferred_element_type=F32)
        if first:
            acc[rows, :] = contrib + bd_ref[0]
        else:
            acc[rows, :] += contrib
        if last:
            for j in range(width):
                y_copy(t + j, row0 // ROW_TILE + t + j).start()

    def all_tiles(f):
        loop(0, nt // MATMUL_TILES, lambda p: tile(MATMUL_TILES * p, MATMUL_TILES, f))
        done = nt // MATMUL_TILES * MATMUL_TILES
        left = nt - done

        @pl.when(left >= 2)
        def _():
            tile(done, 2, f)

        @pl.when(left % 2 == 1)
        def _():
            tile(nt - 1, 1, f)

    @pl.when(nt > 0)
    def _():
        @pl.when(i == 0)
        def _():
            for n, cp in enumerate(w_copies(expert, 0)):
                cp.start(priority=n % 2)
            loop(0, nt, lambda t: x_copy(row0, t).start())

        loop(0, nt, lambda t: x_copy(row0, t).wait())

        def convert(t):
            x_bf[rows_of(t), :] = x_stage[rows_of(t), :].astype(BF16)

        loop(0, nt, convert)

        def start_weights(e, f):
            for n, cp in enumerate(w_copies(e, f)):
                cp.start(priority=n % 2)

        for f in range(N_FF):
            if f + 1 < N_FF:
                start_weights(expert, f + 1)
            else:
                @pl.when(nt_next > 0)
                def _():
                    start_weights(expert_next, 0)
            if f == 0:
                loop(0, nt_next, lambda t: x_copy(row0_next, t).start())
            for cp in w_copies(expert, f):
                cp.wait()
            all_tiles(f)
            if f == 0:
                loop(nt, nt_prev, lambda t: y_copy(t, 0).wait())

        @pl.when(nt_next == 0)
        def _():
            loop(0, nt, lambda t: y_copy(t, 0).wait())


def _experts(item_e, item_row0, item_nt, xs, w_gu, b_gu, w_dn, b_dn):
    n_items = item_e.shape[0]
    n_rows = xs.shape[0]
    E = N_EXPERTS
    item_rows = TILES_PER_ITEM * ROW_TILE
    grid_spec = pltpu.PrefetchScalarGridSpec(
        num_scalar_prefetch=3,
        grid=(n_items,),
        in_specs=[pl.BlockSpec(memory_space=pl.ANY),
                  pl.BlockSpec(memory_space=pl.ANY),
                  pl.BlockSpec(memory_space=pl.ANY),
                  pl.BlockSpec((1, 1, 2 * D_FF), lambda i, ie, r0, nt: (ie[i], 0, 0)),
                  pl.BlockSpec((1, 1, D_MODEL), lambda i, ie, r0, nt: (ie[i], 0, 0))],
        out_specs=pl.BlockSpec(memory_space=pl.ANY),
        scratch_shapes=[pltpu.VMEM((item_rows, D_MODEL), F32),
                        pltpu.VMEM((item_rows, D_MODEL), BF16),
                        pltpu.VMEM((item_rows, D_MODEL), F32),
                        pltpu.VMEM((2, D_MODEL, FF_CHUNK), F32),
                        pltpu.VMEM((2, D_MODEL, FF_CHUNK), F32),
                        pltpu.VMEM((2, FF_CHUNK, D_MODEL), F32),
                        pltpu.SemaphoreType.DMA(()),
                        pltpu.SemaphoreType.DMA((TILES_PER_ITEM,)),
                        pltpu.SemaphoreType.DMA((2,))],
    )
    return pl.pallas_call(
        _experts_kernel,
        grid_spec=grid_spec,
        out_shape=jax.ShapeDtypeStruct((n_rows, D_MODEL), F32),
        compiler_params=pltpu.CompilerParams(dimension_semantics=("arbitrary",),
                                             vmem_limit_bytes=V7X_VMEM_LIMIT_BYTES),
        name="experts",
    )(item_e, item_row0, item_nt, xs, w_gu, w_dn, b_gu.reshape(E, 1, 2 * D_FF), b_dn.reshape(E, 1, D_MODEL))


def _combine_kernel(inv_ref, inv_next_ref, x2_ref, gate_ref, fn_ref, y_ref, o_ref, rows, sem, *,
                    tm, route_tile, last_layer):
    i = pl.program_id(0)
    slot = i % 2
    per_block = route_tile // tm

    groups = tm // SUBLANE

    def row_copy(idx_ref, first, g, j, k, s):
        src_row = idx_ref[0, 0, k * route_tile + first + g * SUBLANE + j]
        return pltpu.make_async_copy(y_ref.at[pl.ds(src_row, 1), :], rows.at[s, k, g, pl.ds(j, 1), :], sem.at[s])

    def gather(idx_ref, step, s):
        first = (step % per_block) * tm if per_block > 1 else 0

        def issue(g, c):
            for j in range(SUBLANE):
                for k in range(TOP_K):
                    row_copy(idx_ref, first, g, j, k, s).start(priority=k % 2)
            return c

        lax.fori_loop(0, groups, issue, 0)

    @pl.when(i == 0)
    def _():
        gather(inv_ref, 0, 0)

    @pl.when(i + 1 < pl.num_programs(0))
    def _():
        gather(inv_next_ref, i + 1, 1 - slot)

    def drain(g, c):
        for j in range(SUBLANE):
            for k in range(TOP_K):
                row_copy(inv_ref, 0, g, j, k, slot).wait()
        return c

    lax.fori_loop(0, groups, drain, 0)
    gate = gate_ref[...]
    out = x2_ref[...]
    for k in range(TOP_K):
        out = out + gate[:, k:k + 1] * rows[slot, k].reshape(tm, D_MODEL)
    o_ref[...] = _rms(out, fn_ref[...]) if last_layer else out


def _combine(inv3, x2, gates, final_norm, y, last_layer, tm=256):
    T = x2.shape[0]
    route_tile = inv3.shape[2] // TOP_K
    per_block = route_tile // tm
    idx_block = (1, 1, TOP_K * route_tile)
    return pl.pallas_call(
        functools.partial(_combine_kernel, tm=tm, route_tile=route_tile, last_layer=last_layer),
        grid=(T // tm,),
        in_specs=[pl.BlockSpec(idx_block, lambda i: (i // per_block, 0, 0), memory_space=pltpu.SMEM),
                  pl.BlockSpec(idx_block, lambda i: (jnp.minimum(i + 1, T // tm - 1) // per_block, 0, 0),
                               memory_space=pltpu.SMEM),
                  pl.BlockSpec((tm, D_MODEL), lambda i: (i, 0)),
                  pl.BlockSpec((tm, LANE), lambda i: (i, 0)),
                  pl.BlockSpec((1, D_MODEL), lambda i: (0, 0)),
                  pl.BlockSpec(memory_space=pl.ANY)],
        out_specs=pl.BlockSpec((tm, D_MODEL), lambda i: (i, 0)),
        out_shape=jax.ShapeDtypeStruct((T, D_MODEL), F32),
        scratch_shapes=[pltpu.VMEM((2, TOP_K, tm // SUBLANE, SUBLANE, D_MODEL), F32),
                        pltpu.SemaphoreType.DMA((2,))],
        compiler_params=pltpu.CompilerParams(dimension_semantics=("arbitrary",),
                                             vmem_limit_bytes=V7X_VMEM_LIMIT_BYTES),
        name="combine",
    )(inv3, inv3, x2, gates, final_norm, y)


def _reorder_w_in_kernel(wt_ref, o_ref):
    tk = wt_ref.shape[1]
    a = 2 * GLA_W + 2 * GLA_VW
    lat = a + GLA_RANK
    kr = lat + Q_LORA + KV_LORA
    half = QK_ROPE // 2

    def put(dst_block, pieces):
        rows = [wt_ref[start:start + n, :] for start, n in pieces]
        used = sum(n for _, n in pieces)
        if used < LANE:
            rows.append(jnp.zeros((LANE - used, tk), F32))
        block = rows[0] if len(rows) == 1 else jnp.concatenate(rows, axis=0)
        o_ref[:, dst_block * LANE:(dst_block + 1) * LANE] = block.T.astype(BF16)

    for j in range(a // LANE):
        put(j, [(j * LANE, LANE)])
    for j in range((Q_LORA + KV_LORA) // LANE):
        put(a // LANE + j, [(lat + j * LANE, LANE)])
    put(32, [(kr, QK_ROPE)])
    put(33, [(kr + half, half), (kr, half)])
    put(34, [(a, GLA_RANK)])


def _reorder_w_in(w_in, tk=256):
    in_cols = w_in.shape[1]
    return pl.pallas_call(
        _reorder_w_in_kernel,
        grid=(D_MODEL // tk,),
        in_specs=[pl.BlockSpec((in_cols, tk), lambda i: (0, i))],
        out_specs=pl.BlockSpec((tk, PROJ_W_COLS), lambda i: (i, 0)),
        out_shape=jax.ShapeDtypeStruct((D_MODEL, PROJ_W_COLS), BF16),
        compiler_params=pltpu.CompilerParams(dimension_semantics=("parallel",)),
        name="reorder_w_in",
    )(w_in.T)


def _reorder_w_uq(w_uq):
    w = w_uq.reshape(Q_LORA, MLA_HEADS, QK_NOPE + QK_ROPE)
    nope = w[:, :, :QK_NOPE].reshape(Q_LORA, MLA_W)
    pe = w[:, :, QK_NOPE:]
    half = QK_ROPE // 2
    z = jnp.zeros((Q_LORA, MLA_HEADS, LANE - QK_ROPE), w_uq.dtype)
    pe_pad = jnp.concatenate([pe, z], axis=2).reshape(Q_LORA, MLA_W)
    pe_sw = jnp.concatenate([pe[:, :, half:], pe[:, :, :half], z], axis=2).reshape(Q_LORA, MLA_W)
    return jnp.concatenate([nope, pe_pad, pe_sw], axis=1).astype(BF16)


def _reorder_w_ukv(w_ukv):
    w = w_ukv.reshape(KV_LORA, MLA_HEADS, QK_NOPE + V_DIM)
    return jnp.concatenate([w[:, :, :QK_NOPE].reshape(KV_LORA, MLA_W),
                            w[:, :, QK_NOPE:].reshape(KV_LORA, MLA_W)], axis=1).astype(BF16)


def _rope_tables(positions):
    half = QK_ROPE // 2
    inv_freq = 1.0 / (ROPE_THETA ** (jnp.arange(0, QK_ROPE, 2, dtype=F32) / QK_ROPE))
    lane = jnp.arange(LANE)
    freq = jnp.where(lane < QK_ROPE, jnp.tile(inv_freq, LANE // half), 0.0)
    sign = jnp.where(lane < half, -1.0, jnp.where(lane < QK_ROPE, 1.0, 0.0))
    ang = positions.astype(F32).reshape(-1, 1) * freq[None, :]
    return jnp.where(lane < QK_ROPE, jnp.cos(ang), 0.0), jnp.sin(ang) * sign


def _routing_plan(counts, topi, rank, n_items):
    padded = (counts + ROW_TILE - 1) // ROW_TILE * ROW_TILE
    pend = jnp.cumsum(padded)
    pstart = pend - padded
    experts = jnp.arange(N_EXPERTS, dtype=topi.dtype)
    seg_start = jnp.sum(jnp.where(topi[..., None] == experts, pstart.astype(topi.dtype), 0), axis=-1)
    inv = (seg_start + rank).reshape(topi.shape[0], 1, TOP_K * topi.shape[2])
    nt_e = padded // ROW_TILE
    ns_e = (nt_e + TILES_PER_ITEM - 1) // TILES_PER_ITEM
    cs = jnp.cumsum(ns_e)
    j = jnp.arange(n_items, dtype=I32)
    valid = j < cs[-1]
    jj = jnp.where(valid, j, cs[-1] - 1)
    e_j = jnp.sum((jj[:, None] >= cs[None, :]).astype(I32), axis=1)
    local = jj - (cs[e_j] - ns_e[e_j])
    row0 = pstart[e_j] + local * (TILES_PER_ITEM * ROW_TILE)
    nt_j = jnp.where(valid, jnp.clip(nt_e[e_j] - local * TILES_PER_ITEM, 0, TILES_PER_ITEM), 0)
    row0_and_used = jnp.concatenate([row0, pend[-1:]])
    return pend.astype(I32), inv.astype(I32), e_j.astype(I32), row0_and_used.astype(I32), nt_j.astype(I32)


def kernel(x, positions, attn_norm, w_in, gla_gate_up, gla_gate_bias, gla_out_norm, mla_q_norm, w_uq, mla_kv_norm, w_ukv, mla_out_norm, w_out, ffn_norm, router_w, router_b, w_gate_up, b_gate_up, w_down, b_down, final_norm):
    B, S, D = x.shape
    T = B * S
    A = T * TOP_K
    n_tiles = (A + N_EXPERTS * (ROW_TILE - 1)) // ROW_TILE
    n_rows = n_tiles * ROW_TILE
    n_items = (n_tiles + N_EXPERTS * (TILES_PER_ITEM - 1)) // TILES_PER_ITEM
    row2 = lambda v: v.reshape(1, -1)

    xf = x.reshape(T, D)
    cos_t, sin_t = _rope_tables(positions)
    for l in range(w_in.shape[0]):
        gup_pad = jnp.zeros((LANE, GLA_W), F32).at[:GLA_RANK].set(gla_gate_up[l]).astype(BF16)
        proj, loga = _in_proj(xf, row2(attn_norm[l]), _reorder_w_in(w_in[l]), gup_pad, row2(gla_gate_bias[l]))
        o_gla = _gla(proj, loga, row2(gla_out_norm[l]), B, S)
        q, k, v = _mla_proj(proj, cos_t, sin_t, row2(mla_q_norm[l]), row2(mla_kv_norm[l]),
                            _reorder_w_uq(w_uq[l]), _reorder_w_ukv(w_ukv[l]))
        o_mla = _mla_attn(q, k, v, B, S)
        rw_pad = jnp.zeros((D, LANE), F32).at[:, :N_EXPERTS].set(router_w[l])
        rw_hi = rw_pad.astype(BF16)
        rw_lo = (rw_pad - rw_hi.astype(F32)).astype(BF16)
        rb_pad = jnp.zeros((1, LANE), F32).at[0, :N_EXPERTS].set(router_b[l])
        x2, h2, topi, gates, rank, counts = _out_router(
            xf, o_gla, o_mla, w_out[l].astype(BF16), row2(mla_out_norm[l]), row2(ffn_norm[l]), rw_hi, rw_lo, rb_pad)
        pend, inv, item_e, item_row0, item_nt = _routing_plan(
            counts[0, :N_EXPERTS], topi[:, :TOP_K, :], rank[:, :TOP_K, :], n_items)
        xs = _dispatch(pend, inv, h2, n_rows)
        y = _experts(item_e, item_row0, item_nt, xs, w_gate_up[l], b_gate_up[l], w_down[l], b_down[l])
        xf = _combine(inv, x2, gates, row2(final_norm), y, last_layer=(l == w_in.shape[0] - 1))
    return xf.reshape(B, S, D)
```

```python
import functools

import jax
import jax.numpy as jnp
from jax import lax
from jax.experimental import pallas as pl
from jax.experimental.pallas import tpu as pltpu

F32 = jnp.float32
BF16 = jnp.bfloat16
I32 = jnp.int32
HIGHEST = lax.Precision.HIGHEST

LANE = 128
SUBLANE = 8
V7X_VMEM_LIMIT_BYTES = 60000 * 1024

D_MODEL = 2048
GLA_HEADS, GLA_DK, GLA_DV, GLA_RANK, GLA_TAU, GLA_CHUNK = 4, 128, 256, 16, 16.0, 64
MLA_HEADS, Q_LORA, KV_LORA, QK_NOPE, QK_ROPE, V_DIM = 8, 512, 512, 128, 64, 128
ROPE_THETA = 10000.0
N_EXPERTS, TOP_K, D_FF = 32, 4, 2048
SWIGLU_LIMIT, SWIGLU_ALPHA = 7.0, 1.702
EPS = 1e-6
LOG2_E = 1.4426950408889634
GLA_W = GLA_HEADS * GLA_DK
GLA_VW = GLA_HEADS * GLA_DV
MLA_W = MLA_HEADS * V_DIM
QK_PAD = 2 * LANE

PROJ_OUT_COLS = 34 * LANE
PROJ_W_COLS = 35 * LANE

ROW_TILE = 128
TILES_PER_ITEM = 10
MATMUL_TILES = 8
FF_CHUNK = 512
N_FF = D_FF // FF_CHUNK


def _rms(x, g):
    return x * lax.rsqrt(jnp.mean(x * x, axis=-1, keepdims=True) + EPS) * g


def _inproj_kernel(x_ref, g_ref, w_ref, gup_ref, gb_ref, proj_ref, loga_ref):
    h = _rms(x_ref[...], g_ref[...]).astype(BF16)
    p = jnp.dot(h, w_ref[...], preferred_element_type=F32)
    proj_ref[...] = p[:, :PROJ_OUT_COLS].astype(BF16)
    z = jnp.dot(p[:, PROJ_OUT_COLS:].astype(BF16), gup_ref[...], preferred_element_type=F32) + gb_ref[...]
    log_sig = jnp.minimum(z, 0.0) - jnp.log(1.0 + jnp.exp(-jnp.abs(z)))
    loga_ref[...] = log_sig * (1.0 / GLA_TAU)


def _in_proj(xf, g, w_r, gup_pad, gbias, tm=256):
    T = xf.shape[0]
    return pl.pallas_call(
        _inproj_kernel,
        grid=(T // tm,),
        in_specs=[pl.BlockSpec((tm, D_MODEL), lambda i: (i, 0)),
                  pl.BlockSpec((1, D_MODEL), lambda i: (0, 0)),
                  pl.BlockSpec((D_MODEL, PROJ_W_COLS), lambda i: (0, 0)),
                  pl.BlockSpec((LANE, GLA_W), lambda i: (0, 0)),
                  pl.BlockSpec((1, GLA_W), lambda i: (0, 0))],
        out_specs=[pl.BlockSpec((tm, PROJ_OUT_COLS), lambda i: (i, 0)),
                   pl.BlockSpec((tm, GLA_W), lambda i: (i, 0))],
        out_shape=[jax.ShapeDtypeStruct((T, PROJ_OUT_COLS), BF16),
                   jax.ShapeDtypeStruct((T, GLA_W), F32)],
        compiler_params=pltpu.CompilerParams(dimension_semantics=("parallel",),
                                             vmem_limit_bytes=V7X_VMEM_LIMIT_BYTES),
        name="in_proj",
    )(xf, g, w_r, gup_pad, gbias)


def _gla_kernel(q_ref, k_ref, v_ref, r_ref, la_ref, gn_ref, o_ref, st_ref, *, n_chunks):
    @pl.when(pl.program_id(1) == 0)
    def _():
        st_ref[...] = jnp.zeros_like(st_ref)

    C = GLA_CHUNK
    row = lax.broadcasted_iota(I32, (C, C), 0)
    col = lax.broadcasted_iota(I32, (C, C), 1)
    causal = col <= row
    tri = causal.astype(F32)
    for c in range(n_chunks):
        sl = pl.ds(c * C, C)
        b = jnp.dot(tri, la_ref[sl, :], precision=HIGHEST, preferred_element_type=F32)
        b_last = b[C - 1:C, :]
        q_dec = (q_ref[sl, :].astype(F32) * (GLA_DK ** -0.5) * jnp.exp(b)).astype(BF16)
        kf = k_ref[sl, :].astype(F32)
        k_intra = (kf * jnp.exp(-b)).astype(BF16)
        k_state = (kf * jnp.exp(b_last - b)).astype(BF16)
        decay = jnp.exp(b_last)
        for h in range(GLA_HEADS):
            hs = slice(h * GLA_DK, (h + 1) * GLA_DK)
            vs = slice(h * GLA_DV, (h + 1) * GLA_DV)
            v = v_ref[sl, vs]
            attn = lax.dot_general(q_dec[:, hs], k_intra[:, hs], (((1,), (1,)), ((), ())),
                                   preferred_element_type=F32)
            attn = jnp.where(causal, attn, 0.0).astype(BF16)
            st = st_ref[h]
            o = jnp.dot(attn, v, preferred_element_type=F32)
            o = o + lax.dot_general(q_dec[:, hs], st.astype(BF16), (((1,), (1,)), ((), ())),
                                    preferred_element_type=F32)
            kv_t = lax.dot_general(v, k_state[:, hs], (((0,), (0,)), ((), ())),
                                   preferred_element_type=F32)
            st_ref[h] = st * decay[:, hs] + kv_t
            r = r_ref[sl, vs].astype(F32)
            o_ref[sl, vs] = (_rms(o, gn_ref[...]) * (r * jax.nn.sigmoid(r))).astype(BF16)


def _gla(proj, loga, gn, batch, seq, ct=256):
    nct = seq // ct
    T = batch * seq
    row_map = lambda blk: (lambda b, c: (b * nct + c, blk))
    return pl.pallas_call(
        functools.partial(_gla_kernel, n_chunks=ct // GLA_CHUNK),
        grid=(batch, nct),
        in_specs=[pl.BlockSpec((ct, GLA_W), row_map(0)),
                  pl.BlockSpec((ct, GLA_W), row_map(1)),
                  pl.BlockSpec((ct, GLA_VW), row_map(1)),
                  pl.BlockSpec((ct, GLA_VW), row_map(2)),
                  pl.BlockSpec((ct, GLA_W), row_map(0)),
                  pl.BlockSpec((1, GLA_DV), lambda b, c: (0, 0))],
        out_specs=pl.BlockSpec((ct, GLA_VW), row_map(0)),
        out_shape=jax.ShapeDtypeStruct((T, GLA_VW), BF16),
        scratch_shapes=[pltpu.VMEM((GLA_HEADS, GLA_DV, GLA_DK), F32)],
        compiler_params=pltpu.CompilerParams(dimension_semantics=("parallel", "arbitrary")),
        name="gla",
    )(proj, proj, proj, proj, loga, gn)


def _mla_proj_kernel(ql_ref, kvl_ref, kr_ref, krs_ref, cos_ref, sin_ref, qn_ref, kvn_ref,
                     wq_ref, wkv_ref, q_ref, k_ref, vt_ref):
    cos = cos_ref[...]
    sin = sin_ref[...]
    scale = (QK_NOPE + QK_ROPE) ** -0.5 * LOG2_E
    qlat = _rms(ql_ref[...].astype(F32), qn_ref[...]).astype(BF16)
    qall = jnp.dot(qlat, wq_ref[...], preferred_element_type=F32)
    kvlat = _rms(kvl_ref[...].astype(F32), kvn_ref[...]).astype(BF16)
    kv = jnp.dot(kvlat, wkv_ref[...], preferred_element_type=F32)
    k_pe = (kr_ref[...].astype(F32) * cos + krs_ref[...].astype(F32) * sin).astype(BF16)
    for h in range(MLA_HEADS):
        hl = slice(h * LANE, (h + 1) * LANE)
        pe = slice(MLA_W + h * LANE, MLA_W + (h + 1) * LANE)
        ps = slice(2 * MLA_W + h * LANE, 2 * MLA_W + (h + 1) * LANE)
        q_ref[h, :, 0:LANE] = (qall[:, hl] * scale).astype(BF16)
        q_ref[h, :, LANE:QK_PAD] = ((qall[:, pe] * cos + qall[:, ps] * sin) * scale).astype(BF16)
        k_ref[h, :, 0:LANE] = kv[:, hl].astype(BF16)
        k_ref[h, :, LANE:QK_PAD] = k_pe
        vt_ref[h] = kv[:, MLA_W + h * LANE:MLA_W + (h + 1) * LANE].T.astype(BF16)


def _mla_proj(proj, cos_t, sin_t, qn, kvn, wq, wkv, tm=512):
    T = proj.shape[0]
    H = MLA_HEADS
    return pl.pallas_call(
        _mla_proj_kernel,
        grid=(T // tm,),
        in_specs=[pl.BlockSpec((tm, Q_LORA), lambda i: (i, 6)),
                  pl.BlockSpec((tm, KV_LORA), lambda i: (i, 7)),
                  pl.BlockSpec((tm, LANE), lambda i: (i, 32)),
                  pl.BlockSpec((tm, LANE), lambda i: (i, 33)),
                  pl.BlockSpec((tm, LANE), lambda i: (i, 0)),
                  pl.BlockSpec((tm, LANE), lambda i: (i, 0)),
                  pl.BlockSpec((1, Q_LORA), lambda i: (0, 0)),
                  pl.BlockSpec((1, KV_LORA), lambda i: (0, 0)),
                  pl.BlockSpec((Q_LORA, 3 * MLA_W), lambda i: (0, 0)),
                  pl.BlockSpec((KV_LORA, 2 * MLA_W), lambda i: (0, 0))],
        out_specs=[pl.BlockSpec((H, tm, QK_PAD), lambda i: (0, i, 0)),
                   pl.BlockSpec((H, tm, QK_PAD), lambda i: (0, i, 0)),
                   pl.BlockSpec((H, V_DIM, tm), lambda i: (0, 0, i))],
        out_shape=[jax.ShapeDtypeStruct((H, T, QK_PAD), BF16),
                   jax.ShapeDtypeStruct((H, T, QK_PAD), BF16),
                   jax.ShapeDtypeStruct((H, V_DIM, T), BF16)],
        compiler_params=pltpu.CompilerParams(dimension_semantics=("parallel",)),
        name="mla_proj",
    )(proj, proj, proj, proj, cos_t, sin_t, qn, kvn, wq, wkv)


def _attn_kernel(q_ref, k_ref, vt_ref, o_ref, s0_ref, s1_ref, p0_ref, p1_ref, a0_ref, a1_ref,
                 m_ref, l_ref, acc_ref, *, tq):
    s_buf, p_buf, a_buf = (s0_ref, s1_ref), (p0_ref, p1_ref), (a0_ref, a1_ref)

    def chunk(c):
        return pl.ds(pl.multiple_of(c * tq, tq), tq)

    def query_tile(qi, carry):
        q = q_ref[0, chunk(qi), :]
        m_ref[...] = jnp.full_like(m_ref, -jnp.inf)
        l_ref[...] = jnp.zeros_like(l_ref)
        acc_ref[...] = jnp.zeros_like(acc_ref)

        def score(c, slot):
            s_buf[slot][...] = lax.dot_general(k_ref[0, chunk(c), :], q, (((1,), (1,)), ((), ())),
                                               preferred_element_type=F32)

        def softmax(slot, diagonal):
            s_t = s_buf[slot][...]
            if diagonal:
                key = lax.broadcasted_iota(I32, (tq, tq), 0)
                qry = lax.broadcasted_iota(I32, (tq, tq), 1)
                s_t = jnp.where(key <= qry, s_t, -jnp.inf)
            m_old = m_ref[...]
            m_new = jnp.maximum(m_old, jnp.max(s_t, axis=0, keepdims=True))
            p_t = jnp.exp2(s_t - m_new)
            alpha = jnp.exp2(m_old - m_new)
            l_ref[...] = alpha * l_ref[...] + jnp.sum(p_t, axis=0, keepdims=True)
            m_ref[...] = m_new
            a_buf[slot][...] = alpha
            p_buf[slot][...] = p_t.astype(BF16)

        def value(c, slot):
            acc_ref[...] = a_buf[slot][...] * acc_ref[...] + jnp.dot(vt_ref[0, :, chunk(c)], p_buf[slot][...],
                                                                    preferred_element_type=F32)

        def stages(c, slot):
            value(c, slot)
            softmax(1 - slot, False)
            score(c + 2, slot)

        score(0, 0)

        @pl.when(qi == 0)
        def _():
            softmax(0, True)
            value(0, 0)

        @pl.when(qi > 0)
        def _():
            score(1, 1)
            softmax(0, False)
            n_full = qi - 1

            def pair(j, c):
                stages(2 * j, 0)
                stages(2 * j + 1, 1)
                return c

            lax.fori_loop(0, n_full // 2, pair, 0)

            @pl.when(n_full % 2 == 1)
            def _():
                stages(qi - 2, 0)

            for parity in (0, 1):
                @pl.when(qi % 2 == parity)
                def _():
                    value(qi - 1, 1 - parity)
                    softmax(parity, True)
                    value(qi, parity)

        o_ref[chunk(qi), :] = (acc_ref[...] / l_ref[...]).T
        return carry

    lax.fori_loop(0, q_ref.shape[1] // tq, query_tile, 0)


def _mla_attn(q, k, vt, batch, seq, tq=512):
    H, T, _ = q.shape
    return pl.pallas_call(
        functools.partial(_attn_kernel, tq=tq),
        grid=(batch, H),
        in_specs=[pl.BlockSpec((1, seq, QK_PAD), lambda b, h: (h, b, 0)),
                  pl.BlockSpec((1, seq, QK_PAD), lambda b, h: (h, b, 0)),
                  pl.BlockSpec((1, V_DIM, seq), lambda b, h: (h, 0, b))],
        out_specs=pl.BlockSpec((seq, V_DIM), lambda b, h: (b, h)),
        out_shape=jax.ShapeDtypeStruct((T, MLA_W), F32),
        scratch_shapes=[pltpu.VMEM((tq, tq), F32), pltpu.VMEM((tq, tq), F32),
                        pltpu.VMEM((tq, tq), BF16), pltpu.VMEM((tq, tq), BF16),
                        pltpu.VMEM((1, tq), F32), pltpu.VMEM((1, tq), F32),
                        pltpu.VMEM((1, tq), F32), pltpu.VMEM((1, tq), F32), pltpu.VMEM((V_DIM, tq), F32)],
        compiler_params=pltpu.CompilerParams(dimension_semantics=("parallel", "parallel")),
        name="mla_attn",
    )(q, k, vt)


def _out_router_kernel(x_ref, og_ref, om_ref, wo_ref, mn_ref, fn_ref, rwh_ref, rwl_ref, rb_ref,
                       x2_ref, h2_ref, topi_ref, gate_ref, rank_ref, cnt_ref, carry_ref, *, tm):
    @pl.when(pl.program_id(0) == 0)
    def _():
        carry_ref[...] = jnp.zeros_like(carry_ref)

    om = _rms(om_ref[...], mn_ref[...]).astype(BF16)
    mix = jnp.dot(og_ref[...], wo_ref[0:GLA_VW, :], preferred_element_type=F32)
    mix = mix + jnp.dot(om, wo_ref[GLA_VW:GLA_VW + MLA_W, :], preferred_element_type=F32)
    x2 = x_ref[...] + mix
    x2_ref[...] = x2
    h2 = _rms(x2, fn_ref[...])
    h2_ref[...] = h2
    h2_hi = h2.astype(BF16)
    h2_lo = (h2 - h2_hi.astype(F32)).astype(BF16)
    logits = (jnp.dot(h2_hi, rwh_ref[...], preferred_element_type=F32)
              + jnp.dot(h2_lo, rwh_ref[...], preferred_element_type=F32)
              + jnp.dot(h2_hi, rwl_ref[...], preferred_element_type=F32)) + rb_ref[...]
    lane = lax.broadcasted_iota(I32, (tm, LANE), 1).astype(F32)
    cur = jnp.where(lane < N_EXPERTS, logits, -jnp.inf)
    vals, idxs = [], []
    for _ in range(TOP_K):
        m = jnp.max(cur, axis=-1, keepdims=True)
        idx = jnp.min(jnp.where(cur == m, lane, float(N_EXPERTS - 1)), axis=-1, keepdims=True)
        vals.append(m)
        idxs.append(idx)
        cur = jnp.where(lane == idx, -jnp.inf, cur)
    es = [jnp.exp(v - vals[0]) for v in vals]
    den = es[0] + es[1] + es[2] + es[3]
    gates = jnp.zeros((tm, LANE), F32)
    topi = jnp.zeros((tm, LANE), F32)
    multi = jnp.zeros((tm, LANE), F32)
    for k in range(TOP_K):
        gates = jnp.where(lane == k, es[k] / den, gates)
        topi = jnp.where(lane == k, idxs[k], topi)
        multi = multi + (lane == idxs[k]).astype(F32)
    gate_ref[...] = gates
    topi_ref[0] = topi.T[0:SUBLANE, :].astype(I32)
    r = lax.broadcasted_iota(I32, (tm, tm), 0)
    c = lax.broadcasted_iota(I32, (tm, tm), 1)
    strict = (c < r).astype(BF16)
    pref = jnp.dot(strict, multi.astype(BF16), preferred_element_type=F32) + carry_ref[...]
    rank = jnp.zeros((tm, LANE), F32)
    for k in range(TOP_K):
        rk = jnp.sum(jnp.where(lane == idxs[k], pref, 0.0), axis=-1, keepdims=True)
        rank = jnp.where(lane == k, rk, rank)
    rank_ref[0] = rank.T[0:SUBLANE, :].astype(I32)
    carry_ref[...] = carry_ref[...] + jnp.sum(multi, axis=0, keepdims=True)
    cnt_ref[...] = carry_ref[...].astype(I32)


def _out_router(xf, o_gla, o_mla, w_out, mn, fn, rw_hi, rw_lo, rb_pad, tm=256):
    T = xf.shape[0]
    row = lambda i: (i, 0)
    fixed = lambda i: (0, 0)
    return pl.pallas_call(
        functools.partial(_out_router_kernel, tm=tm),
        grid=(T // tm,),
        in_specs=[pl.BlockSpec((tm, D_MODEL), row),
                  pl.BlockSpec((tm, GLA_VW), row),
                  pl.BlockSpec((tm, MLA_W), row),
                  pl.BlockSpec((GLA_VW + MLA_W, D_MODEL), fixed),
                  pl.BlockSpec((1, MLA_W), fixed),
                  pl.BlockSpec((1, D_MODEL), fixed),
                  pl.BlockSpec((D_MODEL, LANE), fixed),
                  pl.BlockSpec((D_MODEL, LANE), fixed),
                  pl.BlockSpec((1, LANE), fixed)],
        out_specs=[pl.BlockSpec((tm, D_MODEL), row),
                   pl.BlockSpec((tm, D_MODEL), row),
                   pl.BlockSpec((1, SUBLANE, tm), lambda i: (i, 0, 0)),
                   pl.BlockSpec((tm, LANE), row),
                   pl.BlockSpec((1, SUBLANE, tm), lambda i: (i, 0, 0)),
                   pl.BlockSpec((1, LANE), fixed)],
        out_shape=[jax.ShapeDtypeStruct((T, D_MODEL), F32),
                   jax.ShapeDtypeStruct((T, D_MODEL), F32),
                   jax.ShapeDtypeStruct((T // tm, SUBLANE, tm), I32),
                   jax.ShapeDtypeStruct((T, LANE), F32),
                   jax.ShapeDtypeStruct((T // tm, SUBLANE, tm), I32),
                   jax.ShapeDtypeStruct((1, LANE), I32)],
        scratch_shapes=[pltpu.VMEM((1, LANE), F32)],
        compiler_params=pltpu.CompilerParams(dimension_semantics=("arbitrary",),
                                             vmem_limit_bytes=V7X_VMEM_LIMIT_BYTES),
        name="out_router",
    )(xf, o_gla, o_mla, w_out, mn, fn, rw_hi, rw_lo, rb_pad)


N_DISPATCH_SLOTS = 3


def _dispatch_kernel(pend_ref, inv_ref, h_hbm, xs_ref, hbuf, zero_ref, sem_in, sem_out, sem_fill, *, tm, n_tiles):
    i = pl.program_id(0)
    n_steps = pl.num_programs(0)

    groups = tm // SUBLANE

    def load(tile, s):
        return pltpu.make_async_copy(h_hbm.at[pl.ds(tile * groups, groups)], hbuf.at[s], sem_in.at[s])

    def row_copy(g, j, dst_row, s):
        return pltpu.make_async_copy(hbuf.at[s, g, pl.ds(j, 1), :], xs_ref.at[pl.ds(dst_row, 1), :], sem_out.at[s])

    def drain(s):
        def body(g, c):
            for j in range(SUBLANE):
                for k in range(TOP_K):
                    row_copy(g, j, 0, s).wait()
            return c

        lax.fori_loop(0, groups, body, 0)

    @pl.when(i == 0)
    def _():
        load(0, 0).start()
        zero_ref[...] = jnp.zeros_like(zero_ref)

        def fill_copy(t):
            return pltpu.make_async_copy(zero_ref, xs_ref.at[pl.ds(pl.multiple_of(t * ROW_TILE, ROW_TILE), ROW_TILE), :],
                                         sem_fill)

        def fill(e, wait):
            prev = jnp.where(e == 0, 0, pend_ref[jnp.maximum(e - 1, 0)])

            @pl.when(pend_ref[e] > prev)
            def _():
                cp = fill_copy(pend_ref[e] // ROW_TILE - 1)
                if wait:
                    cp.wait()
                else:
                    cp.start()

        def tail(t, wait):
            if wait:
                fill_copy(t).wait()
            else:
                fill_copy(t).start()

        first_unused = pend_ref[N_EXPERTS - 1] // ROW_TILE
        lax.fori_loop(0, N_EXPERTS, lambda e, c: (fill(e, False), c)[1], 0)
        lax.fori_loop(first_unused, n_tiles, lambda t, c: (tail(t, False), c)[1], 0)
        lax.fori_loop(0, N_EXPERTS, lambda e, c: (fill(e, True), c)[1], 0)
        lax.fori_loop(first_unused, n_tiles, lambda t, c: (tail(t, True), c)[1], 0)

    slot = i % N_DISPATCH_SLOTS
    next_slot = (i + 1) % N_DISPATCH_SLOTS

    @pl.when(i >= 2)
    def _():
        drain(next_slot)

    @pl.when(i + 1 < n_steps)
    def _():
        load(i + 1, next_slot).start()

    load(i, slot).wait()

    def issue(g, c):
        for j in range(SUBLANE):
            for k in range(TOP_K):
                dst_row = inv_ref[0, 0, k * tm + g * SUBLANE + j]
                row_copy(g, j, dst_row, slot).start(priority=k % 2)
        return c

    lax.fori_loop(0, groups, issue, 0)

    @pl.when(i == n_steps - 1)
    def _():
        @pl.when(i >= 1)
        def _():
            drain((i + 2) % N_DISPATCH_SLOTS)

        drain(slot)


def _dispatch(pend, inv3, h2, n_rows):
    T = h2.shape[0]
    tm = inv3.shape[2] // TOP_K
    return pl.pallas_call(
        functools.partial(_dispatch_kernel, tm=tm, n_tiles=n_rows // ROW_TILE),
        grid=(T // tm,),
        in_specs=[pl.BlockSpec(memory_space=pltpu.SMEM),
                  pl.BlockSpec((1, 1, tm * TOP_K), lambda i: (i, 0, 0), memory_space=pltpu.SMEM),
                  pl.BlockSpec(memory_space=pl.ANY)],
        out_specs=pl.BlockSpec(memory_space=pl.ANY),
        out_shape=jax.ShapeDtypeStruct((n_rows, D_MODEL), F32),
        scratch_shapes=[pltpu.VMEM((N_DISPATCH_SLOTS, tm // SUBLANE, SUBLANE, D_MODEL), F32),
                        pltpu.VMEM((ROW_TILE, D_MODEL), F32),
                        pltpu.SemaphoreType.DMA((N_DISPATCH_SLOTS,)),
                        pltpu.SemaphoreType.DMA((N_DISPATCH_SLOTS,)),
                        pltpu.SemaphoreType.DMA(())],
        compiler_params=pltpu.CompilerParams(dimension_semantics=("arbitrary",)),
        name="dispatch",
    )(pend, inv3, h2.reshape(T // SUBLANE, SUBLANE, D_MODEL))


def _experts_kernel(ie_ref, row0_ref, nt_ref, xs_ref, wgu_hbm, wdn_hbm, bgu_ref, bd_ref,
                    y_ref, x_stage, x_bf, acc, wg_buf, wu_buf, wd_buf, sem_x, sem_y, sem_w):
    i = pl.program_id(0)
    n_items = nt_ref.shape[0]
    nt = nt_ref[i]
    row0 = row0_ref[i]
    expert = ie_ref[i]
    nt_prev = jnp.where(i > 0, nt_ref[jnp.maximum(i - 1, 0)], 0)
    nxt = jnp.minimum(i + 1, n_items - 1)
    nt_next = jnp.where(i + 1 < n_items, nt_ref[nxt], 0)
    row0_next = row0_ref[nxt]
    expert_next = ie_ref[nxt]

    def rows_of(t):
        return pl.ds(pl.multiple_of(t * ROW_TILE, ROW_TILE), ROW_TILE)

    def x_copy(first_row, t):
        return pltpu.make_async_copy(xs_ref.at[pl.ds(pl.multiple_of(first_row + t * ROW_TILE, ROW_TILE), ROW_TILE), :],
                                     x_stage.at[rows_of(t), :], sem_x)

    def y_copy(t, dst_tile):
        return pltpu.make_async_copy(acc.at[rows_of(t), :], y_ref.at[rows_of(dst_tile), :], sem_y.at[t])

    def w_copies(e, f):
        s = f % 2
        cols = pl.ds(f * FF_CHUNK, FF_CHUNK)
        return (pltpu.make_async_copy(wgu_hbm.at[e, :, cols], wg_buf.at[s], sem_w.at[s]),
                pltpu.make_async_copy(wgu_hbm.at[e, :, pl.ds(D_FF + f * FF_CHUNK, FF_CHUNK)], wu_buf.at[s], sem_w.at[s]),
                pltpu.make_async_copy(wdn_hbm.at[e, cols, :], wd_buf.at[s], sem_w.at[s]))

    def loop(lo, hi, fn):
        lax.fori_loop(lo, hi, lambda t, c: (fn(t), c)[1], 0)

    @pl.when(i == 0)
    def _():
        acc[pl.ds(0, ROW_TILE), :] = jnp.zeros((ROW_TILE, D_MODEL), F32)
        first_unused = row0_ref[row0_ref.shape[0] - 1] // ROW_TILE
        n_tiles = y_ref.shape[0] // ROW_TILE
        loop(first_unused, n_tiles, lambda t: y_copy(0, t).start())
        loop(first_unused, n_tiles, lambda t: y_copy(0, t).wait())

    def tile(t, width, f):
        first, last, s = f == 0, f == N_FF - 1, f % 2
        if first:
            for j in range(width):
                @pl.when(t + j < nt_prev)
                def _():
                    y_copy(t + j, 0).wait()
        rows = pl.ds(pl.multiple_of(t * ROW_TILE, ROW_TILE), width * ROW_TILE)
        x = x_bf[rows, :]
        b_gate = bgu_ref[0, :, f * FF_CHUNK:(f + 1) * FF_CHUNK]
        b_up = bgu_ref[0, :, D_FF + f * FF_CHUNK:D_FF + (f + 1) * FF_CHUNK]
        gate = jnp.dot(x, wg_buf[s].astype(BF16), preferred_element_type=F32) + b_gate
        up = jnp.dot(x, wu_buf[s].astype(BF16), preferred_element_type=F32) + b_up
        gate = jnp.minimum(gate, SWIGLU_LIMIT)
        up = jnp.clip(up, -SWIGLU_LIMIT, SWIGLU_LIMIT)
        act = (up + 1.0) * (gate * jax.nn.sigmoid(gate * SWIGLU_ALPHA))
        contrib = jnp.dot(act.astype(BF16), wd_buf[s].astype(BF16), preferred_element_type=F32)
        if first:
            acc[rows, :] = contrib + bd_ref[0]
        else:
            acc[rows, :] += contrib
        if last:
            for j in range(width):
                y_copy(t + j, row0 // ROW_TILE + t + j).start()

    def all_tiles(f):
        loop(0, nt // MATMUL_TILES, lambda p: tile(MATMUL_TILES * p, MATMUL_TILES, f))
        done = nt // MATMUL_TILES * MATMUL_TILES
        width = MATMUL_TILES // 2
        while width >= 1:
            left = nt - done

            @pl.when(left >= width)
            def _(start=done, width=width):
                tile(start, width, f)

            done = done + jnp.where(left >= width, width, 0)
            width //= 2

    @pl.when(nt > 0)
    def _():
        @pl.when(i == 0)
        def _():
            for n, cp in enumerate(w_copies(expert, 0)):
                cp.start(priority=n % 2)
            loop(0, nt, lambda t: x_copy(row0, t).start())

        loop(0, nt, lambda t: x_copy(row0, t).wait())

        def convert(t):
            x_bf[rows_of(t), :] = x_stage[rows_of(t), :].astype(BF16)

        loop(0, nt, convert)

        def start_weights(e, f):
            for n, cp in enumerate(w_copies(e, f)):
                cp.start(priority=n % 2)

        for f in range(N_FF):
            if f + 1 < N_FF:
                start_weights(expert, f + 1)
            else:
                @pl.when(nt_next > 0)
                def _():
                    start_weights(expert_next, 0)
            if f == 0:
                loop(0, nt_next, lambda t: x_copy(row0_next, t).start())
            for cp in w_copies(expert, f):
                cp.wait()
            all_tiles(f)
            if f == 0:
                loop(nt, nt_prev, lambda t: y_copy(t, 0).wait())

        @pl.when(nt_next == 0)
        def _():
            loop(0, nt, lambda t: y_copy(t, 0).wait())


def _experts(item_e, item_row0, item_nt, xs, w_gu, b_gu, w_dn, b_dn):
    n_items = item_e.shape[0]
    n_rows = xs.shape[0]
    E = N_EXPERTS
    item_rows = TILES_PER_ITEM * ROW_TILE
    grid_spec = pltpu.PrefetchScalarGridSpec(
        num_scalar_prefetch=3,
        grid=(n_items,),
        in_specs=[pl.BlockSpec(memory_space=pl.ANY),
                  pl.BlockSpec(memory_space=pl.ANY),
                  pl.BlockSpec(memory_space=pl.ANY),
                  pl.BlockSpec((1, 1, 2 * D_FF), lambda i, ie, r0, nt: (ie[i], 0, 0)),
                  pl.BlockSpec((1, 1, D_MODEL), lambda i, ie, r0, nt: (ie[i], 0, 0))],
        out_specs=pl.BlockSpec(memory_space=pl.ANY),
        scratch_shapes=[pltpu.VMEM((item_rows, D_MODEL), F32),
                        pltpu.VMEM((item_rows, D_MODEL), BF16),
                        pltpu.VMEM((item_rows, D_MODEL), F32),
                        pltpu.VMEM((2, D_MODEL, FF_CHUNK), F32),
                        pltpu.VMEM((2, D_MODEL, FF_CHUNK), F32),
                        pltpu.VMEM((2, FF_CHUNK, D_MODEL), F32),
                        pltpu.SemaphoreType.DMA(()),
                        pltpu.SemaphoreType.DMA((TILES_PER_ITEM,)),
                        pltpu.SemaphoreType.DMA((2,))],
    )
    return pl.pallas_call(
        _experts_kernel,
        grid_spec=grid_spec,
        out_shape=jax.ShapeDtypeStruct((n_rows, D_MODEL), F32),
        compiler_params=pltpu.CompilerParams(dimension_semantics=("arbitrary",),
                                             vmem_limit_bytes=V7X_VMEM_LIMIT_BYTES),
        name="experts",
    )(item_e, item_row0, item_nt, xs, w_gu, w_dn, b_gu.reshape(E, 1, 2 * D_FF), b_dn.reshape(E, 1, D_MODEL))


def _combine_kernel(inv_ref, inv_next_ref, x2_ref, gate_ref, fn_ref, y_ref, o_ref, rows, sem, *,
                    tm, route_tile, last_layer):
    i = pl.program_id(0)
    slot = i % 2
    per_block = route_tile // tm

    groups = tm // SUBLANE

    def row_copy(idx_ref, first, g, j, k, s):
        src_row = idx_ref[0, 0, k * route_tile + first + g * SUBLANE + j]
        return pltpu.make_async_copy(y_ref.at[pl.ds(src_row, 1), :], rows.at[s, k, g, pl.ds(j, 1), :], sem.at[s])

    def gather(idx_ref, step, s):
        first = (step % per_block) * tm if per_block > 1 else 0

        def issue(g, c):
            for j in range(SUBLANE):
                for k in range(TOP_K):
                    row_copy(idx_ref, first, g, j, k, s).start(priority=k % 2)
            return c

        lax.fori_loop(0, groups, issue, 0)

    @pl.when(i == 0)
    def _():
        gather(inv_ref, 0, 0)

    @pl.when(i + 1 < pl.num_programs(0))
    def _():
        gather(inv_next_ref, i + 1, 1 - slot)

    def drain(g, c):
        for j in range(SUBLANE):
            for k in range(TOP_K):
                row_copy(inv_ref, 0, g, j, k, slot).wait()
        return c

    lax.fori_loop(0, groups, drain, 0)
    gate = gate_ref[...]
    out = x2_ref[...]
    for k in range(TOP_K):
        out = out + gate[:, k:k + 1] * rows[slot, k].reshape(tm, D_MODEL)
    o_ref[...] = _rms(out, fn_ref[...]) if last_layer else out


def _combine(inv3, x2, gates, final_norm, y, last_layer, tm=256):
    T = x2.shape[0]
    route_tile = inv3.shape[2] // TOP_K
    per_block = route_tile // tm
    idx_block = (1, 1, TOP_K * route_tile)
    return pl.pallas_call(
        functools.partial(_combine_kernel, tm=tm, route_tile=route_tile, last_layer=last_layer),
        grid=(T // tm,),
        in_specs=[pl.BlockSpec(idx_block, lambda i: (i // per_block, 0, 0), memory_space=pltpu.SMEM),
                  pl.BlockSpec(idx_block, lambda i: (jnp.minimum(i + 1, T // tm - 1) // per_block, 0, 0),
                               memory_space=pltpu.SMEM),
                  pl.BlockSpec((tm, D_MODEL), lambda i: (i, 0)),
                  pl.BlockSpec((tm, LANE), lambda i: (i, 0)),
                  pl.BlockSpec((1, D_MODEL), lambda i: (0, 0)),
                  pl.BlockSpec(memory_space=pl.ANY)],
        out_specs=pl.BlockSpec((tm, D_MODEL), lambda i: (i, 0)),
        out_shape=jax.ShapeDtypeStruct((T, D_MODEL), F32),
        scratch_shapes=[pltpu.VMEM((2, TOP_K, tm // SUBLANE, SUBLANE, D_MODEL), F32),
                        pltpu.SemaphoreType.DMA((2,))],
        compiler_params=pltpu.CompilerParams(dimension_semantics=("arbitrary",),
                                             vmem_limit_bytes=V7X_VMEM_LIMIT_BYTES),
        name="combine",
    )(inv3, inv3, x2, gates, final_norm, y)


def _reorder_w_in_kernel(wt_ref, o_ref):
    tk = wt_ref.shape[1]
    a = 2 * GLA_W + 2 * GLA_VW
    lat = a + GLA_RANK
    kr = lat + Q_LORA + KV_LORA
    half = QK_ROPE // 2

    def put(dst_block, pieces):
        rows = [wt_ref[start:start + n, :] for start, n in pieces]
        used = sum(n for _, n in pieces)
        if used < LANE:
            rows.append(jnp.zeros((LANE - used, tk), F32))
        block = rows[0] if len(rows) == 1 else jnp.concatenate(rows, axis=0)
        o_ref[:, dst_block * LANE:(dst_block + 1) * LANE] = block.T.astype(BF16)

    for j in range(a // LANE):
        put(j, [(j * LANE, LANE)])
    for j in range((Q_LORA + KV_LORA) // LANE):
        put(a // LANE + j, [(lat + j * LANE, LANE)])
    put(32, [(kr, QK_ROPE)])
    put(33, [(kr + half, half), (kr, half)])
    put(34, [(a, GLA_RANK)])


def _reorder_w_in(w_in, tk=256):
    in_cols = w_in.shape[1]
    return pl.pallas_call(
        _reorder_w_in_kernel,
        grid=(D_MODEL // tk,),
        in_specs=[pl.BlockSpec((in_cols, tk), lambda i: (0, i))],
        out_specs=pl.BlockSpec((tk, PROJ_W_COLS), lambda i: (i, 0)),
        out_shape=jax.ShapeDtypeStruct((D_MODEL, PROJ_W_COLS), BF16),
        compiler_params=pltpu.CompilerParams(dimension_semantics=("parallel",)),
        name="reorder_w_in",
    )(w_in.T)


def _reorder_w_uq(w_uq):
    w = w_uq.reshape(Q_LORA, MLA_HEADS, QK_NOPE + QK_ROPE)
    nope = w[:, :, :QK_NOPE].reshape(Q_LORA, MLA_W)
    pe = w[:, :, QK_NOPE:]
    half = QK_ROPE // 2
    z = jnp.zeros((Q_LORA, MLA_HEADS, LANE - QK_ROPE), w_uq.dtype)
    pe_pad = jnp.concatenate([pe, z], axis=2).reshape(Q_LORA, MLA_W)
    pe_sw = jnp.concatenate([pe[:, :, half:], pe[:, :, :half], z], axis=2).reshape(Q_LORA, MLA_W)
    return jnp.concatenate([nope, pe_pad, pe_sw], axis=1).astype(BF16)


def _reorder_w_ukv(w_ukv):
    w = w_ukv.reshape(KV_LORA, MLA_HEADS, QK_NOPE + V_DIM)
    return jnp.concatenate([w[:, :, :QK_NOPE].reshape(KV_LORA, MLA_W),
                            w[:, :, QK_NOPE:].reshape(KV_LORA, MLA_W)], axis=1).astype(BF16)


def _rope_tables(positions):
    half = QK_ROPE // 2
    inv_freq = 1.0 / (ROPE_THETA ** (jnp.arange(0, QK_ROPE, 2, dtype=F32) / QK_ROPE))
    lane = jnp.arange(LANE)
    freq = jnp.where(lane < QK_ROPE, jnp.tile(inv_freq, LANE // half), 0.0)
    sign = jnp.where(lane < half, -1.0, jnp.where(lane < QK_ROPE, 1.0, 0.0))
    ang = positions.astype(F32).reshape(-1, 1) * freq[None, :]
    return jnp.where(lane < QK_ROPE, jnp.cos(ang), 0.0), jnp.sin(ang) * sign


def _routing_plan(counts, topi, rank, n_items):
    padded = (counts + ROW_TILE - 1) // ROW_TILE * ROW_TILE
    pend = jnp.cumsum(padded)
    pstart = pend - padded
    experts = jnp.arange(N_EXPERTS, dtype=topi.dtype)
    seg_start = jnp.sum(jnp.where(topi[..., None] == experts, pstart.astype(topi.dtype), 0), axis=-1)
    inv = (seg_start + rank).reshape(topi.shape[0], 1, TOP_K * topi.shape[2])
    nt_e = padded // ROW_TILE
    ns_e = (nt_e + TILES_PER_ITEM - 1) // TILES_PER_ITEM
    cs = jnp.cumsum(ns_e)
    j = jnp.arange(n_items, dtype=I32)
    valid = j < cs[-1]
    jj = jnp.where(valid, j, cs[-1] - 1)
    e_j = jnp.sum((jj[:, None] >= cs[None, :]).astype(I32), axis=1)
    local = jj - (cs[e_j] - ns_e[e_j])
    row0 = pstart[e_j] + local * (TILES_PER_ITEM * ROW_TILE)
    nt_j = jnp.where(valid, jnp.clip(nt_e[e_j] - local * TILES_PER_ITEM, 0, TILES_PER_ITEM), 0)
    row0_and_used = jnp.concatenate([row0, pend[-1:]])
    return pend.astype(I32), inv.astype(I32), e_j.astype(I32), row0_and_used.astype(I32), nt_j.astype(I32)


def kernel(x, positions, attn_norm, w_in, gla_gate_up, gla_gate_bias, gla_out_norm, mla_q_norm, w_uq, mla_kv_norm, w_ukv, mla_out_norm, w_out, ffn_norm, router_w, router_b, w_gate_up, b_gate_up, w_down, b_down, final_norm):
    B, S, D = x.shape
    T = B * S
    A = T * TOP_K
    n_tiles = (A + N_EXPERTS * (ROW_TILE - 1)) // ROW_TILE
    n_rows = n_tiles * ROW_TILE
    n_items = (n_tiles + N_EXPERTS * (TILES_PER_ITEM - 1)) // TILES_PER_ITEM
    row2 = lambda v: v.reshape(1, -1)

    xf = x.reshape(T, D)
    cos_t, sin_t = _rope_tables(positions)
    for l in range(w_in.shape[0]):
        gup_pad = jnp.zeros((LANE, GLA_W), F32).at[:GLA_RANK].set(gla_gate_up[l]).astype(BF16)
        proj, loga = _in_proj(xf, row2(attn_norm[l]), _reorder_w_in(w_in[l]), gup_pad, row2(gla_gate_bias[l]))
        o_gla = _gla(proj, loga, row2(gla_out_norm[l]), B, S)
        q, k, v = _mla_proj(proj, cos_t, sin_t, row2(mla_q_norm[l]), row2(mla_kv_norm[l]),
                            _reorder_w_uq(w_uq[l]), _reorder_w_ukv(w_ukv[l]))
        o_mla = _mla_attn(q, k, v, B, S)
        rw_pad = jnp.zeros((D, LANE), F32).at[:, :N_EXPERTS].set(router_w[l])
        rw_hi = rw_pad.astype(BF16)
        rw_lo = (rw_pad - rw_hi.astype(F32)).astype(BF16)
        rb_pad = jnp.zeros((1, LANE), F32).at[0, :N_EXPERTS].set(router_b[l])
        x2, h2, topi, gates, rank, counts = _out_router(
            xf, o_gla, o_mla, w_out[l].astype(BF16), row2(mla_out_norm[l]), row2(ffn_norm[l]), rw_hi, rw_lo, rb_pad)
        pend, inv, item_e, item_row0, item_nt = _routing_plan(
            counts[0, :N_EXPERTS], topi[:, :TOP_K, :], rank[:, :TOP_K, :], n_items)
        xs = _dispatch(pend, inv, h2, n_rows)
        y = _experts(item_e, item_row0, item_nt, xs, w_gate_up[l], b_gate_up[l], w_down[l], b_down[l])
        xf = _combine(inv, x2, gates, row2(final_norm), y, last_layer=(l == w_in.shape[0] - 1))
    return xf.reshape(B, S, D)
```

```python
import functools

import jax
import jax.numpy as jnp
from jax import lax
from jax.experimental import pallas as pl
from jax.experimental.pallas import tpu as pltpu

F32 = jnp.float32
BF16 = jnp.bfloat16
I32 = jnp.int32
HIGHEST = lax.Precision.HIGHEST

LANE = 128
SUBLANE = 8
V7X_VMEM_LIMIT_BYTES = 60000 * 1024

D_MODEL = 2048
GLA_HEADS, GLA_DK, GLA_DV, GLA_RANK, GLA_TAU, GLA_CHUNK = 4, 128, 256, 16, 16.0, 64
MLA_HEADS, Q_LORA, KV_LORA, QK_NOPE, QK_ROPE, V_DIM = 8, 512, 512, 128, 64, 128
ROPE_THETA = 10000.0
N_EXPERTS, TOP_K, D_FF = 32, 4, 2048
SWIGLU_LIMIT, SWIGLU_ALPHA = 7.0, 1.702
EPS = 1e-6
LOG2_E = 1.4426950408889634
GLA_W = GLA_HEADS * GLA_DK
GLA_VW = GLA_HEADS * GLA_DV
MLA_W = MLA_HEADS * V_DIM
QK_PAD = 2 * LANE

PROJ_OUT_COLS = 34 * LANE
PROJ_W_COLS = 35 * LANE

ROW_TILE = 128
TILES_PER_ITEM = 10
MATMUL_TILES = 4
FF_CHUNK = 512
FF_HALF = 256
N_FF = D_FF // FF_CHUNK


def _rms(x, g):
    return x * lax.rsqrt(jnp.mean(x * x, axis=-1, keepdims=True) + EPS) * g


def _inproj_kernel(x_ref, g_ref, w_ref, gup_ref, gb_ref, proj_ref, loga_ref):
    h = _rms(x_ref[...], g_ref[...]).astype(BF16)
    p = jnp.dot(h, w_ref[...], preferred_element_type=F32)
    proj_ref[...] = p[:, :PROJ_OUT_COLS].astype(BF16)
    z = jnp.dot(p[:, PROJ_OUT_COLS:].astype(BF16), gup_ref[...], preferred_element_type=F32) + gb_ref[...]
    log_sig = jnp.minimum(z, 0.0) - jnp.log(1.0 + jnp.exp(-jnp.abs(z)))
    loga_ref[...] = log_sig * (1.0 / GLA_TAU)


def _in_proj(xf, g, w_r, gup_pad, gbias, tm=256):
    T = xf.shape[0]
    return pl.pallas_call(
        _inproj_kernel,
        grid=(T // tm,),
        in_specs=[pl.BlockSpec((tm, D_MODEL), lambda i: (i, 0)),
                  pl.BlockSpec((1, D_MODEL), lambda i: (0, 0)),
                  pl.BlockSpec((D_MODEL, PROJ_W_COLS), lambda i: (0, 0)),
                  pl.BlockSpec((LANE, GLA_W), lambda i: (0, 0)),
                  pl.BlockSpec((1, GLA_W), lambda i: (0, 0))],
        out_specs=[pl.BlockSpec((tm, PROJ_OUT_COLS), lambda i: (i, 0)),
                   pl.BlockSpec((tm, GLA_W), lambda i: (i, 0))],
        out_shape=[jax.ShapeDtypeStruct((T, PROJ_OUT_COLS), BF16),
                   jax.ShapeDtypeStruct((T, GLA_W), F32)],
        compiler_params=pltpu.CompilerParams(dimension_semantics=("parallel",),
                                             vmem_limit_bytes=V7X_VMEM_LIMIT_BYTES),
        name="in_proj",
    )(xf, g, w_r, gup_pad, gbias)


def _gla_kernel(q_ref, k_ref, v_ref, r_ref, la_ref, gn_ref, o_ref, st_ref, *, n_chunks):
    @pl.when(pl.program_id(1) == 0)
    def _():
        st_ref[...] = jnp.zeros_like(st_ref)

    C = GLA_CHUNK
    row = lax.broadcasted_iota(I32, (C, C), 0)
    col = lax.broadcasted_iota(I32, (C, C), 1)
    causal = col <= row
    tri = causal.astype(F32)
    for c in range(n_chunks):
        sl = pl.ds(c * C, C)
        b = jnp.dot(tri, la_ref[sl, :], precision=HIGHEST, preferred_element_type=F32)
        b_last = b[C - 1:C, :]
        q_dec = (q_ref[sl, :].astype(F32) * (GLA_DK ** -0.5) * jnp.exp(b)).astype(BF16)
        kf = k_ref[sl, :].astype(F32)
        k_intra = (kf * jnp.exp(-b)).astype(BF16)
        k_state = (kf * jnp.exp(b_last - b)).astype(BF16)
        decay = jnp.exp(b_last)
        for h in range(GLA_HEADS):
            hs = slice(h * GLA_DK, (h + 1) * GLA_DK)
            vs = slice(h * GLA_DV, (h + 1) * GLA_DV)
            v = v_ref[sl, vs]
            attn = lax.dot_general(q_dec[:, hs], k_intra[:, hs], (((1,), (1,)), ((), ())),
                                   preferred_element_type=F32)
            attn = jnp.where(causal, attn, 0.0).astype(BF16)
            st = st_ref[h]
            o = jnp.dot(attn, v, preferred_element_type=F32)
            o = o + lax.dot_general(q_dec[:, hs], st.astype(BF16), (((1,), (1,)), ((), ())),
                                    preferred_element_type=F32)
            kv_t = lax.dot_general(v, k_state[:, hs], (((0,), (0,)), ((), ())),
                                   preferred_element_type=F32)
            st_ref[h] = st * decay[:, hs] + kv_t
            r = r_ref[sl, vs].astype(F32)
            o_ref[sl, vs] = (_rms(o, gn_ref[...]) * (r * jax.nn.sigmoid(r))).astype(BF16)


def _gla(proj, loga, gn, batch, seq, ct=256):
    nct = seq // ct
    T = batch * seq
    row_map = lambda blk: (lambda b, c: (b * nct + c, blk))
    return pl.pallas_call(
        functools.partial(_gla_kernel, n_chunks=ct // GLA_CHUNK),
        grid=(batch, nct),
        in_specs=[pl.BlockSpec((ct, GLA_W), row_map(0)),
                  pl.BlockSpec((ct, GLA_W), row_map(1)),
                  pl.BlockSpec((ct, GLA_VW), row_map(1)),
                  pl.BlockSpec((ct, GLA_VW), row_map(2)),
                  pl.BlockSpec((ct, GLA_W), row_map(0)),
                  pl.BlockSpec((1, GLA_DV), lambda b, c: (0, 0))],
        out_specs=pl.BlockSpec((ct, GLA_VW), row_map(0)),
        out_shape=jax.ShapeDtypeStruct((T, GLA_VW), BF16),
        scratch_shapes=[pltpu.VMEM((GLA_HEADS, GLA_DV, GLA_DK), F32)],
        compiler_params=pltpu.CompilerParams(dimension_semantics=("parallel", "arbitrary")),
        name="gla",
    )(proj, proj, proj, proj, loga, gn)


def _mla_proj_kernel(ql_ref, kvl_ref, kr_ref, krs_ref, cos_ref, sin_ref, qn_ref, kvn_ref,
                     wq_ref, wkv_ref, q_ref, k_ref, vt_ref):
    cos = cos_ref[...]
    sin = sin_ref[...]
    scale = (QK_NOPE + QK_ROPE) ** -0.5 * LOG2_E
    qlat = _rms(ql_ref[...].astype(F32), qn_ref[...]).astype(BF16)
    qall = jnp.dot(qlat, wq_ref[...], preferred_element_type=F32)
    kvlat = _rms(kvl_ref[...].astype(F32), kvn_ref[...]).astype(BF16)
    kv = jnp.dot(kvlat, wkv_ref[...], preferred_element_type=F32)
    k_pe = (kr_ref[...].astype(F32) * cos + krs_ref[...].astype(F32) * sin).astype(BF16)
    for h in range(MLA_HEADS):
        hl = slice(h * LANE, (h + 1) * LANE)
        pe = slice(MLA_W + h * LANE, MLA_W + (h + 1) * LANE)
        ps = slice(2 * MLA_W + h * LANE, 2 * MLA_W + (h + 1) * LANE)
        q_ref[h, :, 0:LANE] = (qall[:, hl] * scale).astype(BF16)
        q_ref[h, :, LANE:QK_PAD] = ((qall[:, pe] * cos + qall[:, ps] * sin) * scale).astype(BF16)
        k_ref[h, :, 0:LANE] = kv[:, hl].astype(BF16)
        k_ref[h, :, LANE:QK_PAD] = k_pe
        vt_ref[h] = kv[:, MLA_W + h * LANE:MLA_W + (h + 1) * LANE].T.astype(BF16)


def _mla_proj(proj, cos_t, sin_t, qn, kvn, wq, wkv, tm=512):
    T = proj.shape[0]
    H = MLA_HEADS
    return pl.pallas_call(
        _mla_proj_kernel,
        grid=(T // tm,),
        in_specs=[pl.BlockSpec((tm, Q_LORA), lambda i: (i, 6)),
                  pl.BlockSpec((tm, KV_LORA), lambda i: (i, 7)),
                  pl.BlockSpec((tm, LANE), lambda i: (i, 32)),
                  pl.BlockSpec((tm, LANE), lambda i: (i, 33)),
                  pl.BlockSpec((tm, LANE), lambda i: (i, 0)),
                  pl.BlockSpec((tm, LANE), lambda i: (i, 0)),
                  pl.BlockSpec((1, Q_LORA), lambda i: (0, 0)),
                  pl.BlockSpec((1, KV_LORA), lambda i: (0, 0)),
                  pl.BlockSpec((Q_LORA, 3 * MLA_W), lambda i: (0, 0)),
                  pl.BlockSpec((KV_LORA, 2 * MLA_W), lambda i: (0, 0))],
        out_specs=[pl.BlockSpec((H, tm, QK_PAD), lambda i: (0, i, 0)),
                   pl.BlockSpec((H, tm, QK_PAD), lambda i: (0, i, 0)),
                   pl.BlockSpec((H, V_DIM, tm), lambda i: (0, 0, i))],
        out_shape=[jax.ShapeDtypeStruct((H, T, QK_PAD), BF16),
                   jax.ShapeDtypeStruct((H, T, QK_PAD), BF16),
                   jax.ShapeDtypeStruct((H, V_DIM, T), BF16)],
        compiler_params=pltpu.CompilerParams(dimension_semantics=("parallel",)),
        name="mla_proj",
    )(proj, proj, proj, proj, cos_t, sin_t, qn, kvn, wq, wkv)


def _attn_kernel(q_ref, k_ref, vt_ref, o_ref, s0_ref, s1_ref, p0_ref, p1_ref, a0_ref, a1_ref,
                 m_ref, l_ref, acc_ref, *, tq):
    s_buf, p_buf, a_buf = (s0_ref, s1_ref), (p0_ref, p1_ref), (a0_ref, a1_ref)

    def chunk(c):
        return pl.ds(pl.multiple_of(c * tq, tq), tq)

    def query_tile(qi, carry):
        q = q_ref[0, chunk(qi), :]
        m_ref[...] = jnp.full_like(m_ref, -jnp.inf)
        l_ref[...] = jnp.zeros_like(l_ref)
        acc_ref[...] = jnp.zeros_like(acc_ref)

        def score(c, slot):
            s_buf[slot][...] = lax.dot_general(k_ref[0, chunk(c), :], q, (((1,), (1,)), ((), ())),
                                               preferred_element_type=F32)

        def softmax(slot, diagonal):
            s_t = s_buf[slot][...]
            if diagonal:
                key = lax.broadcasted_iota(I32, (tq, tq), 0)
                qry = lax.broadcasted_iota(I32, (tq, tq), 1)
                s_t = jnp.where(key <= qry, s_t, -jnp.inf)
            m_old = m_ref[...]
            m_new = jnp.maximum(m_old, jnp.max(s_t, axis=0, keepdims=True))
            p_t = jnp.exp2(s_t - m_new)
            alpha = jnp.exp2(m_old - m_new)
            l_ref[...] = alpha * l_ref[...] + jnp.sum(p_t, axis=0, keepdims=True)
            m_ref[...] = m_new
            a_buf[slot][...] = alpha
            p_buf[slot][...] = p_t.astype(BF16)

        def value(c, slot):
            acc_ref[...] = a_buf[slot][...] * acc_ref[...] + jnp.dot(vt_ref[0, :, chunk(c)], p_buf[slot][...],
                                                                    preferred_element_type=F32)

        def stages(c, slot):
            value(c, slot)
            softmax(1 - slot, False)
            score(c + 2, slot)

        score(0, 0)

        @pl.when(qi == 0)
        def _():
            softmax(0, True)
            value(0, 0)

        @pl.when(qi > 0)
        def _():
            score(1, 1)
            softmax(0, False)
            n_full = qi - 1

            def pair(j, c):
                stages(2 * j, 0)
                stages(2 * j + 1, 1)
                return c

            lax.fori_loop(0, n_full // 2, pair, 0)

            @pl.when(n_full % 2 == 1)
            def _():
                stages(qi - 2, 0)

            for parity in (0, 1):
                @pl.when(qi % 2 == parity)
                def _():
                    value(qi - 1, 1 - parity)
                    softmax(parity, True)
                    value(qi, parity)

        o_ref[chunk(qi), :] = (acc_ref[...] / l_ref[...]).T
        return carry

    lax.fori_loop(0, q_ref.shape[1] // tq, query_tile, 0)


def _mla_attn(q, k, vt, batch, seq, tq=512):
    H, T, _ = q.shape
    return pl.pallas_call(
        functools.partial(_attn_kernel, tq=tq),
        grid=(batch, H),
        in_specs=[pl.BlockSpec((1, seq, QK_PAD), lambda b, h: (h, b, 0)),
                  pl.BlockSpec((1, seq, QK_PAD), lambda b, h: (h, b, 0)),
                  pl.BlockSpec((1, V_DIM, seq), lambda b, h: (h, 0, b))],
        out_specs=pl.BlockSpec((seq, V_DIM), lambda b, h: (b, h)),
        out_shape=jax.ShapeDtypeStruct((T, MLA_W), F32),
        scratch_shapes=[pltpu.VMEM((tq, tq), F32), pltpu.VMEM((tq, tq), F32),
                        pltpu.VMEM((tq, tq), BF16), pltpu.VMEM((tq, tq), BF16),
                        pltpu.VMEM((1, tq), F32), pltpu.VMEM((1, tq), F32),
                        pltpu.VMEM((1, tq), F32), pltpu.VMEM((1, tq), F32), pltpu.VMEM((V_DIM, tq), F32)],
        compiler_params=pltpu.CompilerParams(dimension_semantics=("parallel", "parallel")),
        name="mla_attn",
    )(q, k, vt)


def _out_router_kernel(x_ref, og_ref, om_ref, wo_ref, mn_ref, fn_ref, rwh_ref, rwl_ref, rb_ref,
                       x2_ref, h2_ref, topi_ref, gate_ref, rank_ref, cnt_ref, carry_ref, *, tm):
    @pl.when(pl.program_id(0) == 0)
    def _():
        carry_ref[...] = jnp.zeros_like(carry_ref)

    om = _rms(om_ref[...], mn_ref[...]).astype(BF16)
    mix = jnp.dot(og_ref[...], wo_ref[0:GLA_VW, :], preferred_element_type=F32)
    mix = mix + jnp.dot(om, wo_ref[GLA_VW:GLA_VW + MLA_W, :], preferred_element_type=F32)
    x2 = x_ref[...] + mix
    x2_ref[...] = x2
    h2 = _rms(x2, fn_ref[...])
    h2_ref[...] = h2
    h2_hi = h2.astype(BF16)
    h2_lo = (h2 - h2_hi.astype(F32)).astype(BF16)
    logits = (jnp.dot(h2_hi, rwh_ref[...], preferred_element_type=F32)
              + jnp.dot(h2_lo, rwh_ref[...], preferred_element_type=F32)
              + jnp.dot(h2_hi, rwl_ref[...], preferred_element_type=F32)) + rb_ref[...]
    lane = lax.broadcasted_iota(I32, (tm, LANE), 1).astype(F32)
    cur = jnp.where(lane < N_EXPERTS, logits, -jnp.inf)
    vals, idxs = [], []
    for _ in range(TOP_K):
        m = jnp.max(cur, axis=-1, keepdims=True)
        idx = jnp.min(jnp.where(cur == m, lane, float(N_EXPERTS - 1)), axis=-1, keepdims=True)
        vals.append(m)
        idxs.append(idx)
        cur = jnp.where(lane == idx, -jnp.inf, cur)
    es = [jnp.exp(v - vals[0]) for v in vals]
    den = es[0] + es[1] + es[2] + es[3]
    gates = jnp.zeros((tm, LANE), F32)
    topi = jnp.zeros((tm, LANE), F32)
    multi = jnp.zeros((tm, LANE), F32)
    for k in range(TOP_K):
        gates = jnp.where(lane == k, es[k] / den, gates)
        topi = jnp.where(lane == k, idxs[k], topi)
        multi = multi + (lane == idxs[k]).astype(F32)
    gate_ref[...] = gates
    topi_ref[0] = topi.T[0:SUBLANE, :].astype(I32)
    r = lax.broadcasted_iota(I32, (tm, tm), 0)
    c = lax.broadcasted_iota(I32, (tm, tm), 1)
    strict = (c < r).astype(BF16)
    pref = jnp.dot(strict, multi.astype(BF16), preferred_element_type=F32) + carry_ref[...]
    rank = jnp.zeros((tm, LANE), F32)
    for k in range(TOP_K):
        rk = jnp.sum(jnp.where(lane == idxs[k], pref, 0.0), axis=-1, keepdims=True)
        rank = jnp.where(lane == k, rk, rank)
    rank_ref[0] = rank.T[0:SUBLANE, :].astype(I32)
    carry_ref[...] = carry_ref[...] + jnp.sum(multi, axis=0, keepdims=True)
    cnt_ref[...] = carry_ref[...].astype(I32)


def _out_router(xf, o_gla, o_mla, w_out, mn, fn, rw_hi, rw_lo, rb_pad, tm=256):
    T = xf.shape[0]
    row = lambda i: (i, 0)
    fixed = lambda i: (0, 0)
    return pl.pallas_call(
        functools.partial(_out_router_kernel, tm=tm),
        grid=(T // tm,),
        in_specs=[pl.BlockSpec((tm, D_MODEL), row),
                  pl.BlockSpec((tm, GLA_VW), row),
                  pl.BlockSpec((tm, MLA_W), row),
                  pl.BlockSpec((GLA_VW + MLA_W, D_MODEL), fixed),
                  pl.BlockSpec((1, MLA_W), fixed),
                  pl.BlockSpec((1, D_MODEL), fixed),
                  pl.BlockSpec((D_MODEL, LANE), fixed),
                  pl.BlockSpec((D_MODEL, LANE), fixed),
                  pl.BlockSpec((1, LANE), fixed)],
        out_specs=[pl.BlockSpec((tm, D_MODEL), row),
                   pl.BlockSpec((tm, D_MODEL), row),
                   pl.BlockSpec((1, SUBLANE, tm), lambda i: (i, 0, 0)),
                   pl.BlockSpec((tm, LANE), row),
                   pl.BlockSpec((1, SUBLANE, tm), lambda i: (i, 0, 0)),
                   pl.BlockSpec((1, LANE), fixed)],
        out_shape=[jax.ShapeDtypeStruct((T, D_MODEL), F32),
                   jax.ShapeDtypeStruct((T, D_MODEL), F32),
                   jax.ShapeDtypeStruct((T // tm, SUBLANE, tm), I32),
                   jax.ShapeDtypeStruct((T, LANE), F32),
                   jax.ShapeDtypeStruct((T // tm, SUBLANE, tm), I32),
                   jax.ShapeDtypeStruct((1, LANE), I32)],
        scratch_shapes=[pltpu.VMEM((1, LANE), F32)],
        compiler_params=pltpu.CompilerParams(dimension_semantics=("arbitrary",),
                                             vmem_limit_bytes=V7X_VMEM_LIMIT_BYTES),
        name="out_router",
    )(xf, o_gla, o_mla, w_out, mn, fn, rw_hi, rw_lo, rb_pad)


N_DISPATCH_SLOTS = 3


def _dispatch_kernel(pend_ref, inv_ref, h_hbm, xs_ref, hbuf, zero_ref, sem_in, sem_out, sem_fill, *, tm, n_tiles):
    i = pl.program_id(0)
    n_steps = pl.num_programs(0)

    groups = tm // SUBLANE

    def load(tile, s):
        return pltpu.make_async_copy(h_hbm.at[pl.ds(tile * groups, groups)], hbuf.at[s], sem_in.at[s])

    def row_copy(g, j, dst_row, s):
        return pltpu.make_async_copy(hbuf.at[s, g, pl.ds(j, 1), :], xs_ref.at[pl.ds(dst_row, 1), :], sem_out.at[s])

    def drain(s):
        def body(g, c):
            for j in range(SUBLANE):
                for k in range(TOP_K):
                    row_copy(g, j, 0, s).wait()
            return c

        lax.fori_loop(0, groups, body, 0)

    @pl.when(i == 0)
    def _():
        load(0, 0).start()
        zero_ref[...] = jnp.zeros_like(zero_ref)

        def fill_copy(t):
            return pltpu.make_async_copy(zero_ref, xs_ref.at[pl.ds(pl.multiple_of(t * ROW_TILE, ROW_TILE), ROW_TILE), :],
                                         sem_fill)

        def fill(e, wait):
            prev = jnp.where(e == 0, 0, pend_ref[jnp.maximum(e - 1, 0)])

            @pl.when(pend_ref[e] > prev)
            def _():
                cp = fill_copy(pend_ref[e] // ROW_TILE - 1)
                if wait:
                    cp.wait()
                else:
                    cp.start()

        def tail(t, wait):
            if wait:
                fill_copy(t).wait()
            else:
                fill_copy(t).start()

        first_unused = pend_ref[N_EXPERTS - 1] // ROW_TILE
        lax.fori_loop(0, N_EXPERTS, lambda e, c: (fill(e, False), c)[1], 0)
        lax.fori_loop(first_unused, n_tiles, lambda t, c: (tail(t, False), c)[1], 0)
        lax.fori_loop(0, N_EXPERTS, lambda e, c: (fill(e, True), c)[1], 0)
        lax.fori_loop(first_unused, n_tiles, lambda t, c: (tail(t, True), c)[1], 0)

    slot = i % N_DISPATCH_SLOTS
    next_slot = (i + 1) % N_DISPATCH_SLOTS

    @pl.when(i >= 2)
    def _():
        drain(next_slot)

    @pl.when(i + 1 < n_steps)
    def _():
        load(i + 1, next_slot).start()

    load(i, slot).wait()

    def issue(g, c):
        for j in range(SUBLANE):
            for k in range(TOP_K):
                dst_row = inv_ref[0, 0, k * tm + g * SUBLANE + j]
                row_copy(g, j, dst_row, slot).start(priority=k % 2)
        return c

    lax.fori_loop(0, groups, issue, 0)

    @pl.when(i == n_steps - 1)
    def _():
        @pl.when(i >= 1)
        def _():
            drain((i + 2) % N_DISPATCH_SLOTS)

        drain(slot)


def _dispatch(pend, inv3, h2, n_rows):
    T = h2.shape[0]
    tm = inv3.shape[2] // TOP_K
    return pl.pallas_call(
        functools.partial(_dispatch_kernel, tm=tm, n_tiles=n_rows // ROW_TILE),
        grid=(T // tm,),
        in_specs=[pl.BlockSpec(memory_space=pltpu.SMEM),
                  pl.BlockSpec((1, 1, tm * TOP_K), lambda i: (i, 0, 0), memory_space=pltpu.SMEM),
                  pl.BlockSpec(memory_space=pl.ANY)],
        out_specs=pl.BlockSpec(memory_space=pl.ANY),
        out_shape=jax.ShapeDtypeStruct((n_rows, D_MODEL), F32),
        scratch_shapes=[pltpu.VMEM((N_DISPATCH_SLOTS, tm // SUBLANE, SUBLANE, D_MODEL), F32),
                        pltpu.VMEM((ROW_TILE, D_MODEL), F32),
                        pltpu.SemaphoreType.DMA((N_DISPATCH_SLOTS,)),
                        pltpu.SemaphoreType.DMA((N_DISPATCH_SLOTS,)),
                        pltpu.SemaphoreType.DMA(())],
        compiler_params=pltpu.CompilerParams(dimension_semantics=("arbitrary",)),
        name="dispatch",
    )(pend, inv3, h2.reshape(T // SUBLANE, SUBLANE, D_MODEL))


def _experts_kernel(ie_ref, row0_ref, nt_ref, xs_ref, wgu_hbm, wdn_hbm, bgu_ref, bd_ref,
                    y_ref, x_stage, x_bf, acc, wg_buf, wu_buf, wd_buf, sem_x, sem_y, sem_w):
    i = pl.program_id(0)
    n_items = nt_ref.shape[0]
    nt = nt_ref[i]
    row0 = row0_ref[i]
    expert = ie_ref[i]
    nt_prev = jnp.where(i > 0, nt_ref[jnp.maximum(i - 1, 0)], 0)
    nxt = jnp.minimum(i + 1, n_items - 1)
    nt_next = jnp.where(i + 1 < n_items, nt_ref[nxt], 0)
    row0_next = row0_ref[nxt]
    expert_next = ie_ref[nxt]

    def rows_of(t):
        return pl.ds(pl.multiple_of(t * ROW_TILE, ROW_TILE), ROW_TILE)

    def x_copy(first_row, t):
        return pltpu.make_async_copy(xs_ref.at[pl.ds(pl.multiple_of(first_row + t * ROW_TILE, ROW_TILE), ROW_TILE), :],
                                     x_stage.at[rows_of(t), :], sem_x)

    def y_copy(t, dst_tile):
        return pltpu.make_async_copy(acc.at[rows_of(t), :], y_ref.at[rows_of(dst_tile), :], sem_y.at[t])

    def w_copies(e, f):
        s = f % 2
        cols = pl.ds(f * FF_CHUNK, FF_CHUNK)
        return (pltpu.make_async_copy(wgu_hbm.at[e, :, cols], wg_buf.at[s], sem_w.at[s]),
                pltpu.make_async_copy(wgu_hbm.at[e, :, pl.ds(D_FF + f * FF_CHUNK, FF_CHUNK)], wu_buf.at[s], sem_w.at[s]),
                pltpu.make_async_copy(wdn_hbm.at[e, cols, :], wd_buf.at[s], sem_w.at[s]))

    def loop(lo, hi, fn):
        lax.fori_loop(lo, hi, lambda t, c: (fn(t), c)[1], 0)

    @pl.when(i == 0)
    def _():
        acc[pl.ds(0, ROW_TILE), :] = jnp.zeros((ROW_TILE, D_MODEL), F32)
        first_unused = row0_ref[row0_ref.shape[0] - 1] // ROW_TILE
        n_tiles = y_ref.shape[0] // ROW_TILE
        loop(first_unused, n_tiles, lambda t: y_copy(0, t).start())
        loop(first_unused, n_tiles, lambda t: y_copy(0, t).wait())

    def tile(t, width, f):
        first, last, s = f == 0, f == N_FF - 1, f % 2
        if first:
            for j in range(width):
                @pl.when(t + j < nt_prev)
                def _():
                    y_copy(t + j, 0).wait()
        rows = pl.ds(pl.multiple_of(t * ROW_TILE, ROW_TILE), width * ROW_TILE)
        x = x_bf[rows, :]
        acts = []
        for h in range(FF_CHUNK // FF_HALF):
            cols = slice(h * FF_HALF, (h + 1) * FF_HALF)
            b_gate = bgu_ref[0, :, f * FF_CHUNK + h * FF_HALF:f * FF_CHUNK + (h + 1) * FF_HALF]
            b_up = bgu_ref[0, :, D_FF + f * FF_CHUNK + h * FF_HALF:D_FF + f * FF_CHUNK + (h + 1) * FF_HALF]
            gate = jnp.dot(x, wg_buf[s, :, cols].astype(BF16), preferred_element_type=F32) + b_gate
            up = jnp.dot(x, wu_buf[s, :, cols].astype(BF16), preferred_element_type=F32) + b_up
            gate = jnp.minimum(gate, SWIGLU_LIMIT)
            up = jnp.clip(up, -SWIGLU_LIMIT, SWIGLU_LIMIT)
            acts.append(((up + 1.0) * (gate * jax.nn.sigmoid(gate * SWIGLU_ALPHA))).astype(BF16))
        act = jnp.concatenate(acts, axis=1)
        contrib = jnp.dot(act, wd_buf[s].astype(BF16), preferred_element_type=F32)
        if first:
            acc[rows, :] = contrib + bd_ref[0]
        else:
            acc[rows, :] += contrib
        if last:
            for j in range(width):
                y_copy(t + j, row0 // ROW_TILE + t + j).start()

    def all_tiles(f):
        loop(0, nt // MATMUL_TILES, lambda p: tile(MATMUL_TILES * p, MATMUL_TILES, f))
        done = nt // MATMUL_TILES * MATMUL_TILES
        left = nt - done

        @pl.when(left >= 2)
        def _():
            tile(done, 2, f)

        @pl.when(left % 2 == 1)
        def _():
            tile(nt - 1, 1, f)

    @pl.when(nt > 0)
    def _():
        @pl.when(i == 0)
        def _():
            for n, cp in enumerate(w_copies(expert, 0)):
                cp.start(priority=n % 2)
            loop(0, nt, lambda t: x_copy(row0, t).start())

        loop(0, nt, lambda t: x_copy(row0, t).wait())

        def convert(t):
            x_bf[rows_of(t), :] = x_stage[rows_of(t), :].astype(BF16)

        loop(0, nt, convert)

        def start_weights(e, f):
            for n, cp in enumerate(w_copies(e, f)):
                cp.start(priority=n % 2)

        for f in range(N_FF):
            if f + 1 < N_FF:
                start_weights(expert, f + 1)
            else:
                @pl.when(nt_next > 0)
                def _():
                    start_weights(expert_next, 0)
            if f == 0:
                loop(0, nt_next, lambda t: x_copy(row0_next, t).start())
            for cp in w_copies(expert, f):
                cp.wait()
            all_tiles(f)
            if f == 0:
                loop(nt, nt_prev, lambda t: y_copy(t, 0).wait())

        @pl.when(nt_next == 0)
        def _():
            loop(0, nt, lambda t: y_copy(t, 0).wait())


def _experts(item_e, item_row0, item_nt, xs, w_gu, b_gu, w_dn, b_dn):
    n_items = item_e.shape[0]
    n_rows = xs.shape[0]
    E = N_EXPERTS
    item_rows = TILES_PER_ITEM * ROW_TILE
    grid_spec = pltpu.PrefetchScalarGridSpec(
        num_scalar_prefetch=3,
        grid=(n_items,),
        in_specs=[pl.BlockSpec(memory_space=pl.ANY),
                  pl.BlockSpec(memory_space=pl.ANY),
                  pl.BlockSpec(memory_space=pl.ANY),
                  pl.BlockSpec((1, 1, 2 * D_FF), lambda i, ie, r0, nt: (ie[i], 0, 0)),
                  pl.BlockSpec((1, 1, D_MODEL), lambda i, ie, r0, nt: (ie[i], 0, 0))],
        out_specs=pl.BlockSpec(memory_space=pl.ANY),
        scratch_shapes=[pltpu.VMEM((item_rows, D_MODEL), F32),
                        pltpu.VMEM((item_rows, D_MODEL), BF16),
                        pltpu.VMEM((item_rows, D_MODEL), F32),
                        pltpu.VMEM((2, D_MODEL, FF_CHUNK), F32),
                        pltpu.VMEM((2, D_MODEL, FF_CHUNK), F32),
                        pltpu.VMEM((2, FF_CHUNK, D_MODEL), F32),
                        pltpu.SemaphoreType.DMA(()),
                        pltpu.SemaphoreType.DMA((TILES_PER_ITEM,)),
                        pltpu.SemaphoreType.DMA((2,))],
    )
    return pl.pallas_call(
        _experts_kernel,
        grid_spec=grid_spec,
        out_shape=jax.ShapeDtypeStruct((n_rows, D_MODEL), F32),
        compiler_params=pltpu.CompilerParams(dimension_semantics=("arbitrary",),
                                             vmem_limit_bytes=V7X_VMEM_LIMIT_BYTES),
        name="experts",
    )(item_e, item_row0, item_nt, xs, w_gu, w_dn, b_gu.reshape(E, 1, 2 * D_FF), b_dn.reshape(E, 1, D_MODEL))


def _combine_kernel(inv_ref, inv_next_ref, x2_ref, gate_ref, fn_ref, y_ref, o_ref, rows, sem, *,
                    tm, route_tile, last_layer):
    i = pl.program_id(0)
    slot = i % 2
    per_block = route_tile // tm

    groups = tm // SUBLANE

    def row_copy(idx_ref, first, g, j, k, s):
        src_row = idx_ref[0, 0, k * route_tile + first + g * SUBLANE + j]
        return pltpu.make_async_copy(y_ref.at[pl.ds(src_row, 1), :], rows.at[s, k, g, pl.ds(j, 1), :], sem.at[s])

    def gather(idx_ref, step, s):
        first = (step % per_block) * tm if per_block > 1 else 0

        def issue(g, c):
            for j in range(SUBLANE):
                for k in range(TOP_K):
                    row_copy(idx_ref, first, g, j, k, s).start(priority=k % 2)
            return c

        lax.fori_loop(0, groups, issue, 0)

    @pl.when(i == 0)
    def _():
        gather(inv_ref, 0, 0)

    @pl.when(i + 1 < pl.num_programs(0))
    def _():
        gather(inv_next_ref, i + 1, 1 - slot)

    def drain(g, c):
        for j in range(SUBLANE):
            for k in range(TOP_K):
                row_copy(inv_ref, 0, g, j, k, slot).wait()
        return c

    lax.fori_loop(0, groups, drain, 0)
    gate = gate_ref[...]
    out = x2_ref[...]
    for k in range(TOP_K):
        out = out + gate[:, k:k + 1] * rows[slot, k].reshape(tm, D_MODEL)
    o_ref[...] = _rms(out, fn_ref[...]) if last_layer else out


def _combine(inv3, x2, gates, final_norm, y, last_layer, tm=256):
    T = x2.shape[0]
    route_tile = inv3.shape[2] // TOP_K
    per_block = route_tile // tm
    idx_block = (1, 1, TOP_K * route_tile)
    return pl.pallas_call(
        functools.partial(_combine_kernel, tm=tm, route_tile=route_tile, last_layer=last_layer),
        grid=(T // tm,),
        in_specs=[pl.BlockSpec(idx_block, lambda i: (i // per_block, 0, 0), memory_space=pltpu.SMEM),
                  pl.BlockSpec(idx_block, lambda i: (jnp.minimum(i + 1, T // tm - 1) // per_block, 0, 0),
                               memory_space=pltpu.SMEM),
                  pl.BlockSpec((tm, D_MODEL), lambda i: (i, 0)),
                  pl.BlockSpec((tm, LANE), lambda i: (i, 0)),
                  pl.BlockSpec((1, D_MODEL), lambda i: (0, 0)),
                  pl.BlockSpec(memory_space=pl.ANY)],
        out_specs=pl.BlockSpec((tm, D_MODEL), lambda i: (i, 0)),
        out_shape=jax.ShapeDtypeStruct((T, D_MODEL), F32),
        scratch_shapes=[pltpu.VMEM((2, TOP_K, tm // SUBLANE, SUBLANE, D_MODEL), F32),
                        pltpu.SemaphoreType.DMA((2,))],
        compiler_params=pltpu.CompilerParams(dimension_semantics=("arbitrary",),
                                             vmem_limit_bytes=V7X_VMEM_LIMIT_BYTES),
        name="combine",
    )(inv3, inv3, x2, gates, final_norm, y)


def _reorder_w_in_kernel(wt_ref, o_ref):
    tk = wt_ref.shape[1]
    a = 2 * GLA_W + 2 * GLA_VW
    lat = a + GLA_RANK
    kr = lat + Q_LORA + KV_LORA
    half = QK_ROPE // 2

    def put(dst_block, pieces):
        rows = [wt_ref[start:start + n, :] for start, n in pieces]
        used = sum(n for _, n in pieces)
        if used < LANE:
            rows.append(jnp.zeros((LANE - used, tk), F32))
        block = rows[0] if len(rows) == 1 else jnp.concatenate(rows, axis=0)
        o_ref[:, dst_block * LANE:(dst_block + 1) * LANE] = block.T.astype(BF16)

    for j in range(a // LANE):
        put(j, [(j * LANE, LANE)])
    for j in range((Q_LORA + KV_LORA) // LANE):
        put(a // LANE + j, [(lat + j * LANE, LANE)])
    put(32, [(kr, QK_ROPE)])
    put(33, [(kr + half, half), (kr, half)])
    put(34, [(a, GLA_RANK)])


def _reorder_w_in(w_in, tk=256):
    in_cols = w_in.shape[1]
    return pl.pallas_call(
        _reorder_w_in_kernel,
        grid=(D_MODEL // tk,),
        in_specs=[pl.BlockSpec((in_cols, tk), lambda i: (0, i))],
        out_specs=pl.BlockSpec((tk, PROJ_W_COLS), lambda i: (i, 0)),
        out_shape=jax.ShapeDtypeStruct((D_MODEL, PROJ_W_COLS), BF16),
        compiler_params=pltpu.CompilerParams(dimension_semantics=("parallel",)),
        name="reorder_w_in",
    )(w_in.T)


def _reorder_w_uq(w_uq):
    w = w_uq.reshape(Q_LORA, MLA_HEADS, QK_NOPE + QK_ROPE)
    nope = w[:, :, :QK_NOPE].reshape(Q_LORA, MLA_W)
    pe = w[:, :, QK_NOPE:]
    half = QK_ROPE // 2
    z = jnp.zeros((Q_LORA, MLA_HEADS, LANE - QK_ROPE), w_uq.dtype)
    pe_pad = jnp.concatenate([pe, z], axis=2).reshape(Q_LORA, MLA_W)
    pe_sw = jnp.concatenate([pe[:, :, half:], pe[:, :, :half], z], axis=2).reshape(Q_LORA, MLA_W)
    return jnp.concatenate([nope, pe_pad, pe_sw], axis=1).astype(BF16)


def _reorder_w_ukv(w_ukv):
    w = w_ukv.reshape(KV_LORA, MLA_HEADS, QK_NOPE + V_DIM)
    return jnp.concatenate([w[:, :, :QK_NOPE].reshape(KV_LORA, MLA_W),
                            w[:, :, QK_NOPE:].reshape(KV_LORA, MLA_W)], axis=1).astype(BF16)


def _rope_tables(positions):
    half = QK_ROPE // 2
    inv_freq = 1.0 / (ROPE_THETA ** (jnp.arange(0, QK_ROPE, 2, dtype=F32) / QK_ROPE))
    lane = jnp.arange(LANE)
    freq = jnp.where(lane < QK_ROPE, jnp.tile(inv_freq, LANE // half), 0.0)
    sign = jnp.where(lane < half, -1.0, jnp.where(lane < QK_ROPE, 1.0, 0.0))
    ang = positions.astype(F32).reshape(-1, 1) * freq[None, :]
    return jnp.where(lane < QK_ROPE, jnp.cos(ang), 0.0), jnp.sin(ang) * sign


def _routing_plan(counts, topi, rank, n_items):
    padded = (counts + ROW_TILE - 1) // ROW_TILE * ROW_TILE
    pend = jnp.cumsum(padded)
    pstart = pend - padded
    experts = jnp.arange(N_EXPERTS, dtype=topi.dtype)
    seg_start = jnp.sum(jnp.where(topi[..., None] == experts, pstart.astype(topi.dtype), 0), axis=-1)
    inv = (seg_start + rank).reshape(topi.shape[0], 1, TOP_K * topi.shape[2])
    nt_e = padded // ROW_TILE
    ns_e = (nt_e + TILES_PER_ITEM - 1) // TILES_PER_ITEM
    cs = jnp.cumsum(ns_e)
    j = jnp.arange(n_items, dtype=I32)
    valid = j < cs[-1]
    jj = jnp.where(valid, j, cs[-1] - 1)
    e_j = jnp.sum((jj[:, None] >= cs[None, :]).astype(I32), axis=1)
    local = jj - (cs[e_j] - ns_e[e_j])
    row0 = pstart[e_j] + local * (TILES_PER_ITEM * ROW_TILE)
    nt_j = jnp.where(valid, jnp.clip(nt_e[e_j] - local * TILES_PER_ITEM, 0, TILES_PER_ITEM), 0)
    row0_and_used = jnp.concatenate([row0, pend[-1:]])
    return pend.astype(I32), inv.astype(I32), e_j.astype(I32), row0_and_used.astype(I32), nt_j.astype(I32)


def kernel(x, positions, attn_norm, w_in, gla_gate_up, gla_gate_bias, gla_out_norm, mla_q_norm, w_uq, mla_kv_norm, w_ukv, mla_out_norm, w_out, ffn_norm, router_w, router_b, w_gate_up, b_gate_up, w_down, b_down, final_norm):
    B, S, D = x.shape
    T = B * S
    A = T * TOP_K
    n_tiles = (A + N_EXPERTS * (ROW_TILE - 1)) // ROW_TILE
    n_rows = n_tiles * ROW_TILE
    n_items = (n_tiles + N_EXPERTS * (TILES_PER_ITEM - 1)) // TILES_PER_ITEM
    row2 = lambda v: v.reshape(1, -1)

    xf = x.reshape(T, D)
    cos_t, sin_t = _rope_tables(positions)
    for l in range(w_in.shape[0]):
        gup_pad = jnp.zeros((LANE, GLA_W), F32).at[:GLA_RANK].set(gla_gate_up[l]).astype(BF16)
        proj, loga = _in_proj(xf, row2(attn_norm[l]), _reorder_w_in(w_in[l]), gup_pad, row2(gla_gate_bias[l]))
        o_gla = _gla(proj, loga, row2(gla_out_norm[l]), B, S)
        q, k, v = _mla_proj(proj, cos_t, sin_t, row2(mla_q_norm[l]), row2(mla_kv_norm[l]),
                            _reorder_w_uq(w_uq[l]), _reorder_w_ukv(w_ukv[l]))
        o_mla = _mla_attn(q, k, v, B, S)
        rw_pad = jnp.zeros((D, LANE), F32).at[:, :N_EXPERTS].set(router_w[l])
        rw_hi = rw_pad.astype(BF16)
        rw_lo = (rw_pad - rw_hi.astype(F32)).astype(BF16)
        rb_pad = jnp.zeros((1, LANE), F32).at[0, :N_EXPERTS].set(router_b[l])
        x2, h2, topi, gates, rank, counts = _out_router(
            xf, o_gla, o_mla, w_out[l].astype(BF16), row2(mla_out_norm[l]), row2(ffn_norm[l]), rw_hi, rw_lo, rb_pad)
        pend, inv, item_e, item_row0, item_nt = _routing_plan(
            counts[0, :N_EXPERTS], topi[:, :TOP_K, :], rank[:, :TOP_K, :], n_items)
        xs = _dispatch(pend, inv, h2, n_rows)
        y = _experts(item_e, item_row0, item_nt, xs, w_gate_up[l], b_gate_up[l], w_down[l], b_down[l])
        xf = _combine(inv, x2, gates, row2(final_norm), y, last_layer=(l == w_in.shape[0] - 1))
    return xf.reshape(B, S, D)
```
